```python
import math
import jax
import jax.numpy as jnp
from jax import lax
import numpy as np

D_MODEL = 1024
BATCH = 16
SEQ = 4096
DEPTH = 4

CHUNK = 64
N_MIXERS = 4
MEM_LEN = 256
ALPHA = (2 * DEPTH) ** 0.25
BETA = (8 * DEPTH) ** -0.25
LN_EPS = 1e-5
NEG_INF = -1e30
Q_BLOCK = 128

S5_GROUP = 16
S5_GROUPS = D_MODEL // S5_GROUP
S5_STATE = 64

DA_HEAD = 64
DA_HEADS = D_MODEL // (2 * DA_HEAD)

M2_INNER = 2 * D_MODEL
M2_HEADDIM = 64
M2_HEADS = M2_INNER // M2_HEADDIM
M2_GROUPS = 4
M2_STATE = 128
M2_CONV = 4
M2_XBC = M2_INNER + 2 * M2_GROUPS * M2_STATE

ML_INNER = 2 * D_MODEL
ML_HEADS = 4
ML_HEADDIM = ML_INNER // ML_HEADS
ML_CONV = 4

XA_HEADS = 4
XA_HEADDIM = D_MODEL // XA_HEADS

PK_HEADS = 8
PK_NKEYS = 128
PK_EXPERTS = PK_NKEYS * PK_NKEYS
PK_QDIM = 256
PK_TOPK = 16
PK_BLOCK = 128

kernel_name = 'hybrid_chunk_causal_encoder'


def layer_norm(x, g, b):
    xf = x.astype(jnp.float32)
    mu = jnp.mean(xf, axis=-1, keepdims=True)
    var = jnp.mean(jnp.square(xf - mu), axis=-1, keepdims=True)
    y = (xf - mu) * lax.rsqrt(var + LN_EPS) * g.astype(jnp.float32) + b.astype(jnp.float32)
    return y.astype(x.dtype)


def rms_norm(x):
    xf = x.astype(jnp.float32)
    return xf * lax.rsqrt(jnp.mean(jnp.square(xf), axis=-1, keepdims=True) + LN_EPS)


def head_layer_norm(x):
    xf = x.astype(jnp.float32)
    mu = jnp.mean(xf, axis=-1, keepdims=True)
    var = jnp.mean(jnp.square(xf - mu), axis=-1, keepdims=True)
    return (xf - mu) * lax.rsqrt(var + LN_EPS)


def causal_dwconv(x, w, b):
    k, c = w.shape
    y = lax.conv_general_dilated(x, w.astype(x.dtype)[:, None, :], window_strides=(1,),
                                 padding=[(k - 1, 0)], dimension_numbers=('NWC', 'WIO', 'NWC'),
                                 feature_group_count=c)
    return y + b.astype(x.dtype)


def to_chunks(a):
    b, l = a.shape[:2]
    return jnp.moveaxis(a.reshape(b, l // CHUNK, CHUNK, *a.shape[2:]), 1, 0)


def from_chunks(a):
    nc, b, q = a.shape[:3]
    return jnp.moveaxis(a, 0, 1).reshape(b, nc * q, *a.shape[3:])


def alibi_slopes(n_heads):
    return 2.0 ** (-8.0 * jnp.arange(1, n_heads + 1, dtype=jnp.float32) / n_heads)


def s5_mixer(x, lam_re, lam_im, log_dt, b_re, b_im, c_re, c_im, d_skip, w_glu, b_glu):
    bsz, seq, _ = x.shape
    f32 = jnp.float32
    lam = lax.complex(lam_re.astype(f32), lam_im.astype(f32))
    dt = jnp.exp(log_dt.astype(f32))[:, None]
    lam_bar = jnp.exp(lam * dt)
    b_bar = ((lam_bar - 1.0) / lam)[..., None] * lax.complex(b_re.astype(f32), b_im.astype(f32))
    c_mat = lax.complex(c_re.astype(f32), c_im.astype(f32))
    xf = x.astype(f32)
    xc = to_chunks(xf.reshape(bsz, seq, S5_GROUPS, S5_GROUP))
    a_seq = jnp.broadcast_to(lam_bar, (bsz, CHUNK, S5_GROUPS, S5_STATE))

    def combine(left, right):
        a_l, b_l = left
        a_r, b_r = right
        return a_r * a_l, a_r * b_l + b_r

    def step(h, xq):
        bu = jnp.einsum('gph,bqgh->bqgp', b_bar, xq)
        a_cum, h_loc = lax.associative_scan(combine, (a_seq, bu), axis=1)
        hs = h_loc + a_cum * h[:, None]
        y = jnp.einsum('ghp,bqgp->bqgh', c_mat, hs).real
        return hs[:, -1], y

    h0 = jnp.zeros((bsz, S5_GROUPS, S5_STATE), jnp.complex64)
    _, y = lax.scan(step, h0, xc)
    y = from_chunks(y).reshape(bsz, seq, D_MODEL) + d_skip.astype(f32) * xf
    y = jax.nn.gelu(y)
    val, gate = jnp.split(y @ w_glu.astype(f32) + b_glu.astype(f32), 2, axis=-1)
    return (val * jax.nn.sigmoid(gate)).astype(x.dtype)


def diff_attention(x, w_qkv, lam, subln_g, w_o, layer_idx):
    bsz, seq, _ = x.shape
    f32 = jnp.float32
    n_h, d = DA_HEADS, DA_HEAD
    q, k, v = jnp.split(x @ w_qkv, 3, axis=-1)
    q = q.reshape(bsz, seq, n_h, 2, d)
    k = k.reshape(bsz, seq, n_h, 2, d)
    v = v.reshape(bsz, seq, n_h, 2 * d)
    lam_init = 0.8 - 0.6 * math.exp(-0.3 * layer_idx)
    lf = lam.astype(f32)
    lam_full = jnp.exp(jnp.sum(lf[0] * lf[1])) - jnp.exp(jnp.sum(lf[2] * lf[3])) + lam_init
    slopes = alibi_slopes(n_h)
    k_pos = jnp.arange(seq)
    n_blk = seq // Q_BLOCK
    q_blocks = jnp.moveaxis(q.reshape(bsz, n_blk, Q_BLOCK, n_h, 2, d), 1, 0)
    scale = d ** -0.5

    def attend_block(args):
        qb, blk = args
        t = blk * Q_BLOCK + jnp.arange(Q_BLOCK)
        s = jnp.einsum('bqhjd,bkhjd->bhjqk', qb, k).astype(f32) * scale
        dist = jnp.abs(t[:, None] - k_pos[None, :]).astype(f32)
        bias = -slopes[:, None, None] * dist
        allowed = (k_pos // CHUNK)[None, :] <= (t // CHUNK)[:, None]
        s = jnp.where(allowed, s + bias[None, :, None], NEG_INF)
        p = jax.nn.softmax(s, axis=-1)
        a = p[:, :, 0] - lam_full * p[:, :, 1]
        return jnp.einsum('bhqk,bkhe->bqhe', a.astype(v.dtype), v)

    o = lax.map(attend_block, (q_blocks, jnp.arange(n_blk)))
    o = jnp.moveaxis(o, 0, 1).reshape(bsz, seq, n_h, 2 * d)
    o = rms_norm(o) * subln_g.astype(f32) * (1.0 - lam_init)
    return o.reshape(bsz, seq, D_MODEL).astype(x.dtype) @ w_o


def mamba2_mixer(x, w_in, conv_w, conv_b, dt_bias, a_log, d_skip, norm_g, w_out):
    bsz, seq, _ = x.shape
    f32 = jnp.float32
    n_g, n_r, n_p, n_n = M2_GROUPS, M2_HEADS // M2_GROUPS, M2_HEADDIM, M2_STATE
    z, xbc, dt = jnp.split(x @ w_in, [M2_INNER, M2_INNER + M2_XBC], axis=-1)
    xbc = jax.nn.silu(causal_dwconv(xbc, conv_w, conv_b))
    xs, bm, cm = jnp.split(xbc, [M2_INNER, M2_INNER + n_g * n_n], axis=-1)
    dt = jax.nn.softplus(dt.astype(f32) + dt_bias.astype(f32))
    a_gr = -jnp.exp(a_log.astype(f32)).reshape(n_g, n_r)
    xs = xs.astype(f32).reshape(bsz, seq, n_g, n_r, n_p)
    tri = jnp.tril(jnp.ones((CHUNK, CHUNK), dtype=bool))

    def step(h, inp):
        xq, dtq, bq, cq = inp
        a_cs = jnp.cumsum(dtq * a_gr, axis=1)
        seg = a_cs[:, :, None] - a_cs[:, None, :]
        lmat = jnp.exp(jnp.where(tri[None, :, :, None, None], seg, NEG_INF))
        xdt = xq * dtq[..., None]
        cb = jnp.einsum('bign,bjgn->bijg', cq, bq)
        y_diag = jnp.einsum('bijg,bijgr,bjgrp->bigrp', cb, lmat, xdt)
        y_off = jnp.einsum('bign,bgrpn->bigrp', cq, h) * jnp.exp(a_cs)[..., None]
        decay = jnp.exp(a_cs[:, -1:] - a_cs)
        h_new = h * jnp.exp(a_cs[:, -1])[..., None, None] + jnp.einsum('bjgn,bjgr,bjgrp->bgrpn', bq, decay, xdt)
        return h_new, y_diag + y_off

    h0 = jnp.zeros((bsz, n_g, n_r, n_p, n_n), f32)
    inputs = (to_chunks(xs), to_chunks(dt.reshape(bsz, seq, n_g, n_r)),
              to_chunks(bm.astype(f32).reshape(bsz, seq, n_g, n_n)),
              to_chunks(cm.astype(f32).reshape(bsz, seq, n_g, n_n)))
    _, y = lax.scan(step, h0, inputs)
    y = from_chunks(y) + d_skip.astype(f32).reshape(n_g, n_r)[:, :, None] * xs
    y = y.reshape(bsz, seq, M2_INNER) * jax.nn.silu(z.astype(f32))
    y = rms_norm(y.reshape(bsz, seq, n_g, M2_INNER // n_g)).reshape(bsz, seq, M2_INNER) * norm_g.astype(f32)
    return y.astype(x.dtype) @ w_out


def mlstm_mixer(x, w_in, conv_w, conv_b, w_q, w_k, w_v, w_gates, b_gates, norm_g, skip, w_down):
    bsz, seq, _ = x.shape
    f32 = jnp.float32
    n_h, dh = ML_HEADS, ML_HEADDIM
    xm, og = jnp.split(x @ w_in, 2, axis=-1)
    xc = jax.nn.silu(causal_dwconv(xm, conv_w, conv_b))
    xch = xc.reshape(bsz, seq, n_h, dh)
    q = jnp.einsum('blhd,hde->blhe', xch, w_q)
    k = jnp.einsum('blhd,hde->blhe', xch, w_k) * dh ** -0.5
    v = jnp.einsum('blhd,hde->blhe', xm.reshape(bsz, seq, n_h, dh), w_v)
    gates = (q.reshape(bsz, seq, ML_INNER) @ w_gates[0] + k.reshape(bsz, seq, ML_INNER) @ w_gates[1]
             + v.reshape(bsz, seq, ML_INNER) @ w_gates[2] + b_gates).astype(f32)
    log_i = gates[..., :n_h]
    log_f = jax.nn.log_sigmoid(gates[..., n_h:])
    tri = jnp.tril(jnp.ones((CHUNK, CHUNK), dtype=bool))

    def heads_first(a):
        return jnp.moveaxis(to_chunks(a.astype(f32)), 3, 2)

    def step(carry, inp):
        c_st, n_st, m_st = carry
        qq, kk, vv, ii, ff = inp
        bcs = jnp.cumsum(ff, axis=-1)
        dmat = jnp.where(tri, bcs[..., :, None] - bcs[..., None, :] + ii[..., None, :], NEG_INF)
        inter = bcs + m_st[..., None]
        m_row = jnp.maximum(jnp.max(dmat, axis=-1), inter)
        s = jnp.einsum('bhid,bhjd->bhij', qq, kk) * jnp.exp(dmat - m_row[..., None])
        w_inter = jnp.exp(inter - m_row)
        num = jnp.einsum('bhij,bhjd->bhid', s, vv) + w_inter[..., None] * jnp.einsum('bhid,bhde->bhie', qq, c_st)
        den = jnp.sum(s, axis=-1) + w_inter * jnp.einsum('bhid,bhd->bhi', qq, n_st)
        h = num / jnp.maximum(jnp.abs(den), jnp.exp(-m_row))[..., None]
        b_last = bcs[..., -1]
        g = b_last[..., None] - bcs + ii
        m_new = jnp.maximum(b_last + m_st, jnp.max(g, axis=-1))
        wk = jnp.exp(g - m_new[..., None])
        carry_decay = jnp.exp(b_last + m_st - m_new)
        c_new = carry_decay[..., None, None] * c_st + jnp.einsum('bhj,bhjd,bhje->bhde', wk, kk, vv)
        n_new = carry_decay[..., None] * n_st + jnp.einsum('bhj,bhjd->bhd', wk, kk)
        return (c_new, n_new, m_new), h

    carry0 = (jnp.zeros((bsz, n_h, dh, dh), f32), jnp.zeros((bsz, n_h, dh), f32), jnp.zeros((bsz, n_h), f32))
    inputs = (heads_first(q), heads_first(k), heads_first(v), heads_first(log_i), heads_first(log_f))
    _, h = lax.scan(step, carry0, inputs)
    h = from_chunks(jnp.moveaxis(h, 2, 3))
    hn = head_layer_norm(h) * norm_g.astype(f32).reshape(n_h, dh)
    hn = hn + skip.astype(f32).reshape(n_h, dh) * xch.astype(f32)
    out = hn.reshape(bsz, seq, ML_INNER) * jax.nn.sigmoid(og.astype(f32))
    return out.astype(x.dtype) @ w_down


def mem_cross_attention(x, mem, w_q, w_kv, w_o):
    bsz, seq, _ = x.shape
    q = (x @ w_q).reshape(bsz, seq, XA_HEADS, XA_HEADDIM)
    k, v = jnp.split(mem @ w_kv, 2, axis=-1)
    k = k.reshape(bsz, -1, XA_HEADS, XA_HEADDIM)
    v = v.reshape(bsz, -1, XA_HEADS, XA_HEADDIM)
    s = jnp.einsum('blhd,bmhd->bhlm', q, k).astype(jnp.float32) * XA_HEADDIM ** -0.5
    p = jax.nn.softmax(s, axis=-1)
    o = jnp.einsum('bhlm,bmhd->blhd', p.astype(v.dtype), v)
    return o.reshape(bsz, seq, D_MODEL) @ w_o


def peer_ffn(x, w_query, sub_keys, u, v):
    bsz, seq, dm = x.shape
    f32 = jnp.float32
    xb = x.reshape(bsz * seq // PK_BLOCK, PK_BLOCK, dm)
    half = PK_QDIM // 2

    def block(xt):
        tb = xt.shape[0]
        q = (xt @ w_query).reshape(tb, PK_HEADS, 2, half).astype(f32)
        s = jnp.einsum('thjd,jkd->thjk', q, sub_keys.astype(f32))
        sv, si = lax.top_k(s, PK_TOPK)
        cand = sv[:, :, 0, :, None] + sv[:, :, 1, None, :]
        cand_idx = si[:, :, 0, :, None] * PK_NKEYS + si[:, :, 1, None, :]
        cv, ci = lax.top_k(cand.reshape(tb, PK_HEADS, PK_TOPK * PK_TOPK), PK_TOPK)
        eidx = jnp.take_along_axis(cand_idx.reshape(tb, PK_HEADS, PK_TOPK * PK_TOPK), ci, axis=-1)
        g = jax.nn.softmax(cv, axis=-1)
        ue = u[eidx]
        ve = v[eidx]
        act = jax.nn.gelu(jnp.einsum('thkd,td->thk', ue, xt).astype(f32)) * g
        return jnp.einsum('thk,thkd->td', act.astype(ve.dtype), ve)

    y = lax.map(block, xb)
    return y.reshape(bsz, seq, dm).astype(x.dtype)


def setup_inputs(seed: int = 0) -> dict:
    key = jax.random.key(seed)
    ks = iter(jax.random.split(key, 64))
    f32 = jnp.float32

    def nrm(shape, scale):
        return jax.random.normal(next(ks), shape, f32) * scale

    def unif(shape, lo, hi):
        return jax.random.uniform(next(ks), shape, f32, lo, hi)

    n_a, n_b, n_c, n_d = [(DEPTH - t + N_MIXERS - 1) // N_MIXERS for t in range(N_MIXERS)]
    dm = D_MODEL
    inp = {}
    inp['x'] = nrm((BATCH, SEQ, dm), 1.0)
    inp['mem'] = nrm((BATCH, MEM_LEN, dm), 1.0)
    n_idx = jnp.arange(S5_STATE, dtype=f32)
    inp['s5_lam_re'] = -0.5 + nrm((n_a, S5_GROUPS, S5_STATE), 0.01)
    inp['s5_lam_im'] = math.pi * n_idx + nrm((n_a, S5_GROUPS, S5_STATE), 0.01)
    inp['s5_log_dt'] = unif((n_a, S5_GROUPS), math.log(1e-3), math.log(1e-1))
    inp['s5_b_re'] = nrm((n_a, S5_GROUPS, S5_STATE, S5_GROUP), (2 * S5_GROUP) ** -0.5)
    inp['s5_b_im'] = nrm((n_a, S5_GROUPS, S5_STATE, S5_GROUP), (2 * S5_GROUP) ** -0.5)
    inp['s5_c_re'] = nrm((n_a, S5_GROUPS, S5_GROUP, S5_STATE), 0.5)
    inp['s5_c_im'] = nrm((n_a, S5_GROUPS, S5_GROUP, S5_STATE), 0.5)
    inp['s5_d'] = nrm((n_a, dm), 1.0)
    inp['s5_w_glu'] = jnp.concatenate([nrm((n_a, dm, dm), dm ** -0.5 * BETA), nrm((n_a, dm, dm), dm ** -0.5)], axis=-1)
    inp['s5_b_glu'] = nrm((n_a, 2 * dm), 0.01)
    inp['da_w_qkv'] = nrm((n_b, dm, 3 * dm), dm ** -0.5)
    inp['da_lambda'] = nrm((n_b, 4, DA_HEAD), 0.1)
    inp['da_subln_g'] = 1.0 + nrm((n_b, 2 * DA_HEAD), 0.01)
    inp['da_w_o'] = nrm((n_b, dm, dm), dm ** -0.5 * BETA)
    dt0 = jnp.exp(unif((n_c, M2_HEADS), math.log(1e-3), math.log(1e-1)))
    inp['m2_w_in'] = nrm((n_c, dm, M2_INNER + M2_XBC + M2_HEADS), dm ** -0.5)
    inp['m2_conv_w'] = nrm((n_c, M2_CONV, M2_XBC), M2_CONV ** -0.5)
    inp['m2_conv_b'] = nrm((n_c, M2_XBC), 0.01)
    inp['m2_dt_bias'] = dt0 + jnp.log(-jnp.expm1(-dt0))
    inp['m2_a_log'] = jnp.log(unif((n_c, M2_HEADS), 1.0, 16.0))
    inp['m2_d'] = 1.0 + nrm((n_c, M2_HEADS), 0.01)
    inp['m2_norm_g'] = 1.0 + nrm((n_c, M2_INNER), 0.01)
    inp['m2_w_out'] = nrm((n_c, M2_INNER, dm), M2_INNER ** -0.5 * BETA)
    inp['ml_w_in'] = nrm((n_d, dm, 2 * ML_INNER), dm ** -0.5)
    inp['ml_conv_w'] = nrm((n_d, ML_CONV, ML_INNER), ML_CONV ** -0.5)
    inp['ml_conv_b'] = nrm((n_d, ML_INNER), 0.01)
    inp['ml_w_q'] = nrm((n_d, ML_HEADS, ML_HEADDIM, ML_HEADDIM), ML_HEADDIM ** -0.5)
    inp['ml_w_k'] = nrm((n_d, ML_HEADS, ML_HEADDIM, ML_HEADDIM), ML_HEADDIM ** -0.5)
    inp['ml_w_v'] = nrm((n_d, ML_HEADS, ML_HEADDIM, ML_HEADDIM), ML_HEADDIM ** -0.5)
    inp['ml_w_gates'] = nrm((n_d, 3, ML_INNER, 2 * ML_HEADS), (3 * ML_INNER) ** -0.5)
    inp['ml_b_gates'] = jnp.concatenate([nrm((n_d, ML_HEADS), 0.1),
                                         jnp.linspace(3.0, 6.0, ML_HEADS, dtype=f32) + nrm((n_d, ML_HEADS), 0.01)], axis=-1)
    inp['ml_norm_g'] = 1.0 + nrm((n_d, ML_INNER), 0.01)
    inp['ml_skip'] = 1.0 + nrm((n_d, ML_INNER), 0.01)
    inp['ml_w_down'] = nrm((n_d, ML_INNER, dm), ML_INNER ** -0.5 * BETA)
    inp['xa_w_q'] = nrm((DEPTH, dm, dm), dm ** -0.5)
    inp['xa_w_kv'] = nrm((DEPTH, dm, 2 * dm), dm ** -0.5)
    inp['xa_w_o'] = nrm((DEPTH, dm, dm), dm ** -0.5 * BETA)
    inp['pk_w_query'] = nrm((DEPTH, dm, PK_HEADS * PK_QDIM), dm ** -0.5)
    inp['pk_sub_keys'] = nrm((DEPTH, 2, PK_NKEYS, PK_QDIM // 2), (PK_QDIM // 2) ** -0.5)
    inp['pk_u'] = nrm((DEPTH, PK_EXPERTS, dm), dm ** -0.5)
    inp['pk_v'] = nrm((DEPTH, PK_EXPERTS, dm), BETA * PK_HEADS ** -0.5)
    inp['ln_g'] = 1.0 + nrm((DEPTH, 3, dm), 0.01)
    inp['ln_b'] = nrm((DEPTH, 3, dm), 0.01)
    return inp


def reference(x, mem,
              s5_lam_re, s5_lam_im, s5_log_dt, s5_b_re, s5_b_im, s5_c_re, s5_c_im, s5_d, s5_w_glu, s5_b_glu,
              da_w_qkv, da_lambda, da_subln_g, da_w_o,
              m2_w_in, m2_conv_w, m2_conv_b, m2_dt_bias, m2_a_log, m2_d, m2_norm_g, m2_w_out,
              ml_w_in, ml_conv_w, ml_conv_b, ml_w_q, ml_w_k, ml_w_v, ml_w_gates, ml_b_gates, ml_norm_g, ml_skip, ml_w_down,
              xa_w_q, xa_w_kv, xa_w_o,
              pk_w_query, pk_sub_keys, pk_u, pk_v,
              ln_g, ln_b):
    h = x
    for i in range(DEPTH):
        kind, j = i % N_MIXERS, i // N_MIXERS
        if kind == 0:
            y = s5_mixer(h, s5_lam_re[j], s5_lam_im[j], s5_log_dt[j], s5_b_re[j], s5_b_im[j],
                         s5_c_re[j], s5_c_im[j], s5_d[j], s5_w_glu[j], s5_b_glu[j])
        elif kind == 1:
            y = diff_attention(h, da_w_qkv[j], da_lambda[j], da_subln_g[j], da_w_o[j], i)
        elif kind == 2:
            y = mamba2_mixer(h, m2_w_in[j], m2_conv_w[j], m2_conv_b[j], m2_dt_bias[j], m2_a_log[j],
                             m2_d[j], m2_norm_g[j], m2_w_out[j])
        else:
            y = mlstm_mixer(h, ml_w_in[j], ml_conv_w[j], ml_conv_b[j], ml_w_q[j], ml_w_k[j], ml_w_v[j],
                            ml_w_gates[j], ml_b_gates[j], ml_norm_g[j], ml_skip[j], ml_w_down[j])
        h = layer_norm(ALPHA * h + y, ln_g[i, 0], ln_b[i, 0])
        h = layer_norm(ALPHA * h + mem_cross_attention(h, mem, xa_w_q[i], xa_w_kv[i], xa_w_o[i]), ln_g[i, 1], ln_b[i, 1])
        h = layer_norm(ALPHA * h + peer_ffn(h, pk_w_query[i], pk_sub_keys[i], pk_u[i], pk_v[i]), ln_g[i, 2], ln_b[i, 2])
    return h
```

```python
import functools
import math

import jax
import jax.numpy as jnp
from jax import lax
from jax.experimental import pallas as pl
from jax.experimental.pallas import tpu as pltpu

F32 = jnp.float32
BF16 = jnp.bfloat16
HIGHEST = lax.Precision.HIGHEST

D_MODEL = 1024
DEPTH = 4
ALPHA = (2 * DEPTH) ** 0.25
LN_EPS = 1e-5
NEG_INF = -1e30
CHUNK = 64

S5_GROUP = 16
S5_STATE = 64
S5_Q = 64

DA_HEAD = 64
DA_HEADS = 8
DA_TILE = 256

M2_INNER = 2048
M2_HEADS = 32
M2_HEADDIM = 64
M2_GROUPS = 4
M2_STATE = 128
M2_CONV = 4
M2_Q = 256

ML_INNER = 2048
ML_HEADS = 4
ML_HEADDIM = 512
ML_CONV = 4
ML_Q = 256

XA_HEADS = 4
XA_HEADDIM = 256

PK_HEADS = 8
PK_NKEYS = 128
PK_QDIM = 256
PK_TOPK = 16
PK_SLOTS = PK_HEADS * PK_TOPK
ROW_WORDS = D_MODEL // 2 // 128

VMEM_LIMIT = 56 * 1024 * 1024


def _cparams(*sem):
    return pltpu.CompilerParams(dimension_semantics=sem, vmem_limit_bytes=VMEM_LIMIT)


def _gelu(x):
    return 0.5 * x * (1.0 + jnp.tanh(math.sqrt(2.0 / math.pi) * (x + 0.044715 * x * x * x)))


def _sigmoid(x):
    return 1.0 / (1.0 + jnp.exp(-x))


def _silu(x):
    return x * _sigmoid(x)


def _softplus(x):
    return jnp.maximum(x, 0.0) + jnp.log(1.0 + jnp.exp(-jnp.abs(x)))


def _layer_norm_rows(v, g, b):
    mu = jnp.mean(v, axis=-1, keepdims=True)
    c = v - mu
    var = jnp.mean(c * c, axis=-1, keepdims=True)
    return c * lax.rsqrt(var + LN_EPS) * g + b


def _bdot(a, b):
    return jnp.dot(a.astype(BF16), b.astype(BF16), preferred_element_type=F32)


def _bdot_nt(a, b):
    return lax.dot_general(a.astype(BF16), b.astype(BF16), (((1,), (1,)), ((), ())), preferred_element_type=F32)


def _bdot_tn(a, b):
    return lax.dot_general(a.astype(BF16), b.astype(BF16), (((0,), (0,)), ((), ())), preferred_element_type=F32)


def _full(shape):
    return pl.BlockSpec(shape, lambda *_: (0,) * len(shape))


def _causal_conv(x, halo, w, b):
    q = x.shape[0]
    k = w.shape[0]
    ext = jnp.concatenate([halo, x], axis=0)
    out = x * w[k - 1:k, :] + b
    for d in range(1, k):
        out = out + ext[8 - d:8 - d + q, :] * w[k - 1 - d:k - d, :]
    return out


def _cumsum_rows(a, tri):
    return jnp.dot(tri, a, precision=HIGHEST, preferred_element_type=F32)


def _linear_kernel(*refs, n_out):
    a = refs[0][...].astype(BF16)
    for w_ref, o_ref in zip(refs[1:1 + n_out], refs[1 + n_out:]):
        o_ref[...] = jnp.dot(a, w_ref[...], preferred_element_type=F32).astype(o_ref.dtype)


def _linear(a, ws, out_dtypes, tm=512):
    m, k = a.shape
    tm = min(tm, m)
    return pl.pallas_call(
        functools.partial(_linear_kernel, n_out=len(ws)),
        name="linear",
        grid=(m // tm,),
        in_specs=[pl.BlockSpec((tm, k), lambda i: (i, 0))] + [_full(w.shape) for w in ws],
        out_specs=[pl.BlockSpec((tm, w.shape[1]), lambda i: (i, 0)) for w in ws],
        out_shape=[jax.ShapeDtypeStruct((m, w.shape[1]), dt) for w, dt in zip(ws, out_dtypes)],
        compiler_params=_cparams("parallel"),
    )(a, *ws)


def _linear_res_ln_kernel(a_ref, w_ref, h_ref, g_ref, b_ref, o_ref):
    y = jnp.dot(a_ref[...].astype(BF16), w_ref[...], preferred_element_type=F32)
    o_ref[...] = _layer_norm_rows(ALPHA * h_ref[...] + y, g_ref[...], b_ref[...])


def _linear_res_ln(a, w, h, g, b, tm=512):
    m, k = a.shape
    tm = min(tm, m)
    return pl.pallas_call(
        _linear_res_ln_kernel,
        name="linear_res_ln",
        grid=(m // tm,),
        in_specs=[pl.BlockSpec((tm, k), lambda i: (i, 0)), _full(w.shape),
                  pl.BlockSpec((tm, D_MODEL), lambda i: (i, 0)), _full((1, D_MODEL)), _full((1, D_MODEL))],
        out_specs=pl.BlockSpec((tm, D_MODEL), lambda i: (i, 0)),
        out_shape=jax.ShapeDtypeStruct((m, D_MODEL), F32),
        compiler_params=_cparams("parallel"),
    )(a, w, h, g.reshape(1, -1), b.reshape(1, -1))


def _s5_tables(lam_re, lam_im, log_dt, b_re, b_im, c_re, c_im):
    q = S5_Q
    g, p = lam_re.shape
    lam = lax.complex(lam_re.astype(F32), lam_im.astype(F32))
    dt = jnp.exp(log_dt.astype(F32))[:, None]
    lam_bar = jnp.exp(lam * dt)
    b_bar = ((lam_bar - 1.0) / lam)[..., None] * lax.complex(b_re.astype(F32), b_im.astype(F32))
    c_mat = lax.complex(c_re.astype(F32), c_im.astype(F32))
    pw = jnp.cumprod(jnp.concatenate([jnp.ones((1, g, p), lam_bar.dtype),
                                      jnp.broadcast_to(lam_bar, (q, g, p))], axis=0), axis=0)
    cp = c_mat[None] * pw[:q, :, None, :]
    taps = (jnp.einsum('tgop,gpi->gtoi', cp.real, b_bar.real, precision=HIGHEST)
            - jnp.einsum('tgop,gpi->gtoi', cp.imag, b_bar.imag, precision=HIGHEST))
    pos = jnp.arange(q)
    tau = pos[None, :] - pos[:, None]
    toep = jnp.where((tau >= 0)[None, :, :, None, None], taps[:, jnp.maximum(tau, 0)], 0.0)
    toep = toep.transpose(0, 1, 4, 2, 3).reshape(g, q * S5_GROUP, q * S5_GROUP)
    wb = pw[q - 1 - pos][:, :, :, None] * b_bar[None]
    wt = jnp.concatenate([wb.real, wb.imag], axis=2).transpose(1, 0, 3, 2).reshape(g, q * S5_GROUP, 2 * p)
    cv = c_mat[None] * pw[1:q + 1, :, None, :]
    vt = jnp.concatenate([cv.real, -cv.imag], axis=3).transpose(1, 3, 0, 2).reshape(g, 2 * p, q * S5_GROUP)
    aq = pw[q]
    a_re = jnp.concatenate([aq.real, aq.real], axis=1).reshape(g, 1, 2 * p)
    a_im = jnp.concatenate([-aq.imag, aq.imag], axis=1).reshape(g, 1, 2 * p)
    return toep.astype(BF16), wt.astype(BF16), vt.astype(BF16), a_re, a_im


def _s5_kernel(x_ref, tt_ref, wt_ref, vt_ref, ar_ref, ai_ref, y_ref, s_scr, h_scr, *, nb):
    x = x_ref[0]
    y_ref[0] = jnp.dot(x, tt_ref[0], preferred_element_type=F32)
    s_scr[...] = jnp.dot(x, wt_ref[0], preferred_element_type=F32)
    a_re = ar_ref[0]
    a_im = ai_ref[0]
    half = s_scr.shape[1] // 2

    def step(c, h):
        rows = pl.ds(pl.multiple_of(c * nb, nb), nb)
        h_scr[rows, :] = h
        return a_re * h + a_im * pltpu.roll(h, half, axis=1) + s_scr[rows, :]

    lax.fori_loop(0, x.shape[0] // nb, step, jnp.zeros((nb, s_scr.shape[1]), F32))
    y_ref[0] += jnp.dot(h_scr[...].astype(BF16), vt_ref[0], preferred_element_type=F32)


def _s5_scan(x3, lam_re, lam_im, log_dt, b_re, b_im, c_re, c_im):
    bsz, seq, dm = x3.shape
    g = dm // S5_GROUP
    q = S5_Q
    nc = seq // q
    m = nc * bsz
    toep, wt, vt, a_re, a_im = _s5_tables(lam_re, lam_im, log_dt, b_re, b_im, c_re, c_im)
    xg = x3.reshape(bsz, nc, q, g, S5_GROUP).transpose(3, 1, 0, 2, 4).reshape(g, m, q * S5_GROUP).astype(BF16)
    w = q * S5_GROUP
    y = pl.pallas_call(
        functools.partial(_s5_kernel, nb=bsz),
        name="s5_scan",
        grid=(g,),
        in_specs=[pl.BlockSpec((1, m, w), lambda i: (i, 0, 0)),
                  pl.BlockSpec((1, w, w), lambda i: (i, 0, 0)),
                  pl.BlockSpec((1, w, 2 * S5_STATE), lambda i: (i, 0, 0)),
                  pl.BlockSpec((1, 2 * S5_STATE, w), lambda i: (i, 0, 0)),
                  pl.BlockSpec((1, 1, 2 * S5_STATE), lambda i: (i, 0, 0)),
                  pl.BlockSpec((1, 1, 2 * S5_STATE), lambda i: (i, 0, 0))],
        out_specs=pl.BlockSpec((1, m, w), lambda i: (i, 0, 0)),
        out_shape=jax.ShapeDtypeStruct((g, m, w), F32),
        scratch_shapes=[pltpu.VMEM((m, 2 * S5_STATE), F32), pltpu.VMEM((m, 2 * S5_STATE), F32)],
        compiler_params=_cparams("parallel"),
    )(xg, toep, wt, vt, a_re, a_im)
    return y.reshape(g, nc, bsz, q, S5_GROUP).transpose(2, 1, 3, 0, 4).reshape(bsz, seq, dm)


def _s5_out_kernel(y_ref, x_ref, d_ref, w_ref, bias_ref, g_ref, b_ref, o_ref):
    x = x_ref[...]
    y = _gelu(y_ref[...] + d_ref[...] * x)
    z = jnp.dot(y.astype(BF16), w_ref[...], preferred_element_type=F32) + bias_ref[...]
    out = z[:, :D_MODEL] * _sigmoid(z[:, D_MODEL:])
    o_ref[...] = _layer_norm_rows(ALPHA * x + out, g_ref[...], b_ref[...])


def s5_layer(x3, lam_re, lam_im, log_dt, b_re, b_im, c_re, c_im, d_skip, w_glu, b_glu, g, b, tm=512):
    bsz, seq, dm = x3.shape
    t = bsz * seq
    tm = min(tm, t)
    y = _s5_scan(x3, lam_re, lam_im, log_dt, b_re, b_im, c_re, c_im).reshape(t, dm)
    row = pl.BlockSpec((tm, dm), lambda i: (i, 0))
    return pl.pallas_call(
        _s5_out_kernel,
        name="s5_out",
        grid=(t // tm,),
        in_specs=[row, row, _full((1, dm)), _full((dm, 2 * dm)), _full((1, 2 * dm)), _full((1, dm)), _full((1, dm))],
        out_specs=row,
        out_shape=jax.ShapeDtypeStruct((t, dm), F32),
        compiler_params=_cparams("parallel"),
    )(y, x3.reshape(t, dm), d_skip.reshape(1, -1), w_glu.astype(BF16), b_glu.reshape(1, -1),
      g.reshape(1, -1), b.reshape(1, -1))


def _diff_attn_kernel(sc_ref, q_ref, k_ref, v_ref, g_ref, o_ref, m_scr, l_scr, acc_scr):
    hh = pl.program_id(1)
    qi = pl.program_id(2)
    kj = pl.program_id(3)
    tq = q_ref.shape[0]
    slope = sc_ref[0, hh]

    @pl.when(kj == 0)
    def _():
        m_scr[...] = jnp.full(m_scr.shape, NEG_INF, F32)
        l_scr[...] = jnp.zeros(l_scr.shape, F32)
        acc_scr[...] = jnp.zeros(acc_scr.shape, F32)

    @pl.when(kj <= qi)
    def _():
        q = q_ref[...]
        k = k_ref[...]
        v = v_ref[...]
        lane = lax.broadcasted_iota(jnp.int32, q.shape, 1)
        row = lax.broadcasted_iota(jnp.int32, (tq, tq), 0)
        col = lax.broadcasted_iota(jnp.int32, (tq, tq), 1)
        dist = jnp.abs(row - col + (qi - kj) * tq).astype(F32)
        bias = -slope * dist
        allowed = jnp.logical_or(kj < qi, (col // CHUNK) <= (row // CHUNK))
        for j in range(2):
            qj = jnp.where((lane // DA_HEAD) == j, q, jnp.zeros_like(q)) * jnp.asarray(DA_HEAD ** -0.5, q.dtype)
            s = lax.dot_general(qj, k, (((1,), (1,)), ((), ())), preferred_element_type=F32)
            s = jnp.where(allowed, s + bias, NEG_INF)
            m_old = m_scr[j]
            m_new = jnp.maximum(m_old, jnp.max(s, axis=-1, keepdims=True))
            p = jnp.exp(s - m_new)
            corr = jnp.exp(m_old - m_new)
            l_scr[j] = corr * l_scr[j] + jnp.sum(p, axis=-1, keepdims=True)
            acc_scr[j] = corr * acc_scr[j] + jnp.dot(p.astype(BF16), v, preferred_element_type=F32)
            m_scr[j] = m_new

    @pl.when(kj == qi)
    def _():
        lam_full = sc_ref[1, 0]
        out_scale = sc_ref[1, 1]
        o = acc_scr[0] / l_scr[0] - lam_full * (acc_scr[1] / l_scr[1])
        o = o * lax.rsqrt(jnp.mean(o * o, axis=-1, keepdims=True) + LN_EPS)
        o_ref[...] = (o * g_ref[...] * out_scale).astype(o_ref.dtype)


def _diff_attention(qkv, scalars, subln_g, bsz, seq):
    t = bsz * seq
    tq = min(DA_TILE, seq)
    nq = seq // tq
    n_h = DA_HEADS
    return pl.pallas_call(
        _diff_attn_kernel,
        name="diff_attention",
        grid=(bsz, n_h, nq, nq),
        in_specs=[pl.BlockSpec(memory_space=pltpu.SMEM),
                  pl.BlockSpec((tq, 2 * DA_HEAD), lambda b, h, i, j: (b * nq + i, h)),
                  pl.BlockSpec((tq, 2 * DA_HEAD), lambda b, h, i, j: (b * nq + jnp.minimum(j, i), n_h + h)),
                  pl.BlockSpec((tq, 2 * DA_HEAD), lambda b, h, i, j: (b * nq + jnp.minimum(j, i), 2 * n_h + h)),
                  _full((1, 2 * DA_HEAD))],
        out_specs=pl.BlockSpec((tq, 2 * DA_HEAD), lambda b, h, i, j: (b * nq + i, h)),
        out_shape=jax.ShapeDtypeStruct((t, D_MODEL), BF16),
        scratch_shapes=[pltpu.VMEM((2, tq, 1), F32), pltpu.VMEM((2, tq, 1), F32),
                        pltpu.VMEM((2, tq, 2 * DA_HEAD), F32)],
        compiler_params=_cparams("parallel", "parallel", "parallel", "arbitrary"),
    )(scalars, qkv, qkv, qkv, subln_g.reshape(1, -1))


def diff_attention_layer(h2, bsz, seq, w_qkv, lam, subln_g, w_o, layer_idx, g, b):
    lam_init = 0.8 - 0.6 * math.exp(-0.3 * layer_idx)
    lf = lam.astype(F32)
    lam_full = jnp.exp(jnp.sum(lf[0] * lf[1])) - jnp.exp(jnp.sum(lf[2] * lf[3])) + lam_init
    slopes = 2.0 ** (-8.0 * jnp.arange(1, DA_HEADS + 1, dtype=F32) / DA_HEADS)
    scalars = jnp.stack([slopes, jnp.zeros((DA_HEADS,), F32).at[0].set(lam_full).at[1].set(1.0 - lam_init)])
    (qkv,) = _linear(h2, [w_qkv.astype(BF16)], [BF16])
    o = _diff_attention(qkv, scalars, subln_g, bsz, seq)
    return _linear_res_ln(o, w_o.astype(BF16), h2, g, b)


def _mamba_kernel(z_ref, xbc_ref, dt_ref, h_ref, cw_ref, cb_ref, dtb_ref, a_ref, d_ref, ng_ref, wo_ref,
                  lg_ref, lb_ref, o_ref, state_scr, halo_scr, y_scr):
    q = z_ref.shape[0]
    n = M2_STATE

    @pl.when(pl.program_id(1) == 0)
    def _():
        state_scr[...] = jnp.zeros(state_scr.shape, F32)
        halo_scr[...] = jnp.zeros(halo_scr.shape, F32)

    xbc_raw = xbc_ref[...]
    xbc = _silu(_causal_conv(xbc_raw, halo_scr[...], cw_ref[...], cb_ref[...]))
    halo_scr[...] = xbc_raw[q - 8:, :]
    row = lax.broadcasted_iota(jnp.int32, (q, q), 0)
    col = lax.broadcasted_iota(jnp.int32, (q, q), 1)
    lower = row >= col
    tri = lower.astype(F32)
    dt = _softplus(dt_ref[...] + dtb_ref[...])
    a_cs = _cumsum_rows(dt * a_ref[...], tri)
    a_cs_t = a_cs.T
    a_last = a_cs[q - 1:q, :]
    e_cs = jnp.exp(a_cs)
    e_dec = jnp.exp(a_last - a_cs)
    e_last = jnp.exp(a_last)
    lane = lax.broadcasted_iota(jnp.int32, (q, 2 * M2_HEADDIM), 1)
    first = lane < M2_HEADDIM
    srow = lax.broadcasted_iota(jnp.int32, (2 * M2_HEADDIM, n), 0)
    heads_per_group = M2_HEADS // M2_GROUPS
    for grp in range(M2_GROUPS):
        bm = xbc[:, M2_INNER + grp * n:M2_INNER + (grp + 1) * n]
        cm = xbc[:, M2_INNER + M2_GROUPS * n + grp * n:M2_INNER + M2_GROUPS * n + (grp + 1) * n]
        cb = _bdot_nt(cm, bm)
        for pair in range(heads_per_group // 2):
            h0 = grp * heads_per_group + 2 * pair
            cols = slice(h0 * M2_HEADDIM, (h0 + 2) * M2_HEADDIM)
            xs = xbc[:, cols]
            dtp = jnp.where(first, dt[:, h0:h0 + 1], dt[:, h0 + 1:h0 + 2])
            xdt = xs * dtp
            y = jnp.zeros((q, 2 * M2_HEADDIM), F32)
            for s in range(2):
                hd = h0 + s
                seg = a_cs[:, hd:hd + 1] - a_cs_t[hd:hd + 1, :]
                lmat = jnp.exp(jnp.where(lower, seg, NEG_INF))
                part = jnp.where(first if s == 0 else jnp.logical_not(first), xdt, 0.0)
                y = y + _bdot(cb * lmat, part)
            state = state_scr[pl.ds(h0 * M2_HEADDIM, 2 * M2_HEADDIM), :]
            y_off = _bdot_nt(cm, state)
            y = y + y_off * jnp.where(first, e_cs[:, h0:h0 + 1], e_cs[:, h0 + 1:h0 + 2])
            dec = jnp.where(first, e_dec[:, h0:h0 + 1], e_dec[:, h0 + 1:h0 + 2])
            grow = jnp.where(srow < M2_HEADDIM, e_last[:, h0:h0 + 1], e_last[:, h0 + 1:h0 + 2])
            state_scr[pl.ds(h0 * M2_HEADDIM, 2 * M2_HEADDIM), :] = state * grow + _bdot_tn(xdt * dec, bm)
            y_scr[:, cols] = y + d_ref[:, cols] * xs
    gate = _silu(z_ref[...])
    y = y_scr[...] * gate
    gw = M2_INNER // M2_GROUPS
    parts = []
    for grp in range(M2_GROUPS):
        yg = y[:, grp * gw:(grp + 1) * gw]
        parts.append(yg * lax.rsqrt(jnp.mean(yg * yg, axis=-1, keepdims=True) + LN_EPS))
    yn = jnp.concatenate(parts, axis=1) * ng_ref[...]
    out = jnp.dot(yn.astype(BF16), wo_ref[...], preferred_element_type=F32)
    o_ref[...] = _layer_norm_rows(ALPHA * h_ref[...] + out, lg_ref[...], lb_ref[...])


def mamba2_layer(h2, bsz, seq, w_in, conv_w, conv_b, dt_bias, a_log, d_skip, norm_g, w_out, g, b):
    t = bsz * seq
    q = min(M2_Q, seq)
    nc = seq // q
    xbc_w = M2_INNER + 2 * M2_GROUPS * M2_STATE
    wb = w_in.astype(BF16)
    hpad = 128 - M2_HEADS
    pad_heads = lambda a: jnp.pad(a.astype(F32).reshape(1, -1), ((0, 0), (0, hpad)))
    z, xbc, dt = _linear(h2, [wb[:, :M2_INNER], wb[:, M2_INNER:M2_INNER + xbc_w],
                              jnp.pad(wb[:, M2_INNER + xbc_w:], ((0, 0), (0, hpad)))],
                         [F32, F32, F32], tm=256)
    d_cols = jnp.repeat(d_skip.astype(F32), M2_HEADDIM).reshape(1, -1)
    rowspec = lambda w: pl.BlockSpec((q, w), lambda i, j: (i * nc + j, 0))
    return pl.pallas_call(
        _mamba_kernel,
        name="mamba2",
        grid=(bsz, nc),
        in_specs=[rowspec(M2_INNER), rowspec(xbc_w), rowspec(128), rowspec(D_MODEL),
                  _full((M2_CONV, xbc_w)), _full((1, xbc_w)), _full((1, 128)), _full((1, 128)),
                  _full((1, M2_INNER)), _full((1, M2_INNER)), _full((M2_INNER, D_MODEL)),
                  _full((1, D_MODEL)), _full((1, D_MODEL))],
        out_specs=rowspec(D_MODEL),
        out_shape=jax.ShapeDtypeStruct((t, D_MODEL), F32),
        scratch_shapes=[pltpu.VMEM((M2_HEADS * M2_HEADDIM, M2_STATE), F32),
                        pltpu.VMEM((8, xbc_w), F32),
                        pltpu.VMEM((q, M2_INNER), F32)],
        compiler_params=_cparams("parallel", "arbitrary"),
    )(z, xbc, dt, h2, conv_w.astype(F32), conv_b.reshape(1, -1), pad_heads(dt_bias),
      pad_heads(-jnp.exp(a_log.astype(F32))), d_cols, norm_g.reshape(1, -1), w_out.astype(BF16),
      g.reshape(1, -1), b.reshape(1, -1))


def _mlstm_proj_kernel(xm_ref, cw_ref, cb_ref, wq_ref, wk_ref, wv_ref, wg_ref, bg_ref,
                       xc_ref, q_ref, k_ref, v_ref, gate_ref, halo_scr):
    tq = xm_ref.shape[0]

    @pl.when(pl.program_id(1) == 0)
    def _():
        halo_scr[...] = jnp.zeros(halo_scr.shape, F32)

    xm = xm_ref[...]
    xc = _silu(_causal_conv(xm, halo_scr[...], cw_ref[...], cb_ref[...]))
    halo_scr[...] = xm[tq - 8:, :]
    xc_ref[...] = xc
    gates = jnp.zeros((tq, bg_ref.shape[1]), F32) + bg_ref[...]
    for hd in range(ML_HEADS):
        cols = slice(hd * ML_HEADDIM, (hd + 1) * ML_HEADDIM)
        qh = _bdot(xc[:, cols], wq_ref[hd])
        kh = _bdot(xc[:, cols], wk_ref[hd]) * ML_HEADDIM ** -0.5
        vh = _bdot(xm[:, cols], wv_ref[hd])
        q_ref[:, cols] = qh.astype(BF16)
        k_ref[:, cols] = kh.astype(BF16)
        v_ref[:, cols] = vh.astype(BF16)
        gates = gates + _bdot(qh, wg_ref[0, cols, :]) + _bdot(kh, wg_ref[1, cols, :]) + _bdot(vh, wg_ref[2, cols, :])
    gate_ref[...] = gates


def _mlstm_cell_kernel(q_ref, k_ref, v_ref, gate_ref, h_ref, c_scr, n_scr, m_scr):
    hd = pl.program_id(1)
    qn = q_ref.shape[0]

    @pl.when(pl.program_id(2) == 0)
    def _():
        c_scr[...] = jnp.zeros(c_scr.shape, F32)
        n_scr[...] = jnp.zeros(n_scr.shape, F32)
        m_scr[...] = jnp.zeros(m_scr.shape, F32)

    q = q_ref[...]
    k = k_ref[...]
    v = v_ref[...]
    gates = gate_ref[...]
    glane = lax.broadcasted_iota(jnp.int32, gates.shape, 1)
    gsub = lax.broadcasted_iota(jnp.int32, (gates.shape[1], qn), 0)
    row = lax.broadcasted_iota(jnp.int32, (qn, qn), 0)
    col = lax.broadcasted_iota(jnp.int32, (qn, qn), 1)
    lower = row >= col
    logf = jnp.minimum(gates, 0.0) - jnp.log(1.0 + jnp.exp(-jnp.abs(gates)))
    cum = _cumsum_rows(logf, lower.astype(F32))
    ii = jnp.sum(jnp.where(glane == hd, gates, 0.0), axis=-1, keepdims=True)
    bcs = jnp.sum(jnp.where(glane == ML_HEADS + hd, cum, 0.0), axis=-1, keepdims=True)
    ii_row = jnp.sum(jnp.where(gsub == hd, gates.T, 0.0), axis=0, keepdims=True)
    bcs_row = jnp.sum(jnp.where(gsub == ML_HEADS + hd, cum.T, 0.0), axis=0, keepdims=True)
    m_prev = m_scr[0:1, 0:1]
    dmat = jnp.where(lower, bcs - bcs_row + ii_row, NEG_INF)
    inter = bcs + m_prev
    m_row = jnp.maximum(jnp.max(dmat, axis=-1, keepdims=True), inter)
    s = _bdot_nt(q, k) * jnp.exp(dmat - m_row)
    w_inter = jnp.exp(inter - m_row)
    num = _bdot(s, v) + w_inter * _bdot(q, c_scr[...])
    den = jnp.sum(s, axis=-1, keepdims=True) + w_inter * jnp.sum(q.astype(F32) * n_scr[0:1, :], axis=-1, keepdims=True)
    h_ref[...] = num / jnp.maximum(jnp.abs(den), jnp.exp(-m_row))
    b_last = bcs[qn - 1:qn, :]
    gdec = b_last - bcs + ii
    m_new = jnp.maximum(b_last + m_prev, jnp.max(gdec, axis=0, keepdims=True))
    wk = jnp.exp(gdec - m_new)
    decay = jnp.exp(b_last + m_prev - m_new)
    kw = k.astype(F32) * wk
    c_scr[...] = decay * c_scr[...] + _bdot_tn(kw, v)
    n_scr[...] = decay * n_scr[...] + jnp.sum(kw, axis=0, keepdims=True)
    m_scr[...] = jnp.zeros(m_scr.shape, F32) + m_new


def _mlstm_out_kernel(hc_ref, xc_ref, og_ref, h_ref, ng_ref, sk_ref, wd_ref, lg_ref, lb_ref, o_ref):
    hc = hc_ref[...]
    parts = []
    for hd in range(ML_HEADS):
        x = hc[:, hd * ML_HEADDIM:(hd + 1) * ML_HEADDIM]
        mu = jnp.mean(x, axis=-1, keepdims=True)
        c = x - mu
        parts.append(c * lax.rsqrt(jnp.mean(c * c, axis=-1, keepdims=True) + LN_EPS))
    hn = jnp.concatenate(parts, axis=1) * ng_ref[...] + sk_ref[...] * xc_ref[...]
    out = hn * _sigmoid(og_ref[...])
    y = jnp.dot(out.astype(BF16), wd_ref[...], preferred_element_type=F32)
    o_ref[...] = _layer_norm_rows(ALPHA * h_ref[...] + y, lg_ref[...], lb_ref[...])


def mlstm_layer(h2, bsz, seq, w_in, conv_w, conv_b, w_q, w_k, w_v, w_gates, b_gates, norm_g, skip, w_down, g, b):
    t = bsz * seq
    wb = w_in.astype(BF16)
    xm, og = _linear(h2, [wb[:, :ML_INNER], wb[:, ML_INNER:]], [F32, F32])
    tq = min(256, seq)
    nt = seq // tq
    rows = lambda w: pl.BlockSpec((tq, w), lambda i, j: (i * nt + j, 0))
    gpad = 128 - 2 * ML_HEADS
    wg = jnp.pad(w_gates.astype(BF16), ((0, 0), (0, 0), (0, gpad)))
    bg = jnp.pad(b_gates.astype(F32).reshape(1, -1), ((0, 0), (0, gpad)))
    xc, q, k, v, gates = pl.pallas_call(
        _mlstm_proj_kernel,
        name="mlstm_proj",
        grid=(bsz, nt),
        in_specs=[rows(ML_INNER), _full((ML_CONV, ML_INNER)), _full((1, ML_INNER)),
                  _full(w_q.shape), _full(w_k.shape), _full(w_v.shape), _full(wg.shape), _full(bg.shape)],
        out_specs=[rows(ML_INNER), rows(ML_INNER), rows(ML_INNER), rows(ML_INNER), rows(128)],
        out_shape=[jax.ShapeDtypeStruct((t, ML_INNER), F32)] + [jax.ShapeDtypeStruct((t, ML_INNER), BF16)] * 3
        + [jax.ShapeDtypeStruct((t, 128), F32)],
        scratch_shapes=[pltpu.VMEM((8, ML_INNER), F32)],
        compiler_params=_cparams("parallel", "arbitrary"),
    )(xm, conv_w.astype(F32), conv_b.reshape(1, -1), w_q.astype(BF16), w_k.astype(BF16), w_v.astype(BF16), wg, bg)
    qn = min(ML_Q, seq)
    nc = seq // qn
    head = pl.BlockSpec((qn, ML_HEADDIM), lambda i, hd, j: (i * nc + j, hd))
    hc = pl.pallas_call(
        _mlstm_cell_kernel,
        name="mlstm_cell",
        grid=(bsz, ML_HEADS, nc),
        in_specs=[head, head, head, pl.BlockSpec((qn, 128), lambda i, hd, j: (i * nc + j, 0))],
        out_specs=head,
        out_shape=jax.ShapeDtypeStruct((t, ML_INNER), F32),
        scratch_shapes=[pltpu.VMEM((ML_HEADDIM, ML_HEADDIM), F32), pltpu.VMEM((8, ML_HEADDIM), F32),
                        pltpu.VMEM((8, 128), F32)],
        compiler_params=_cparams("parallel", "parallel", "arbitrary"),
    )(q, k, v, gates)
    tm = min(512, t)
    row = lambda w: pl.BlockSpec((tm, w), lambda i: (i, 0))
    return pl.pallas_call(
        _mlstm_out_kernel,
        name="mlstm_out",
        grid=(t // tm,),
        in_specs=[row(ML_INNER), row(ML_INNER), row(ML_INNER), row(D_MODEL), _full((1, ML_INNER)),
                  _full((1, ML_INNER)), _full((ML_INNER, D_MODEL)), _full((1, D_MODEL)), _full((1, D_MODEL))],
        out_specs=row(D_MODEL),
        out_shape=jax.ShapeDtypeStruct((t, D_MODEL), F32),
        compiler_params=_cparams("parallel"),
    )(hc, xc, og, h2, norm_g.reshape(1, -1), skip.reshape(1, -1), w_down.astype(BF16),
      g.reshape(1, -1), b.reshape(1, -1))


def _cross_attn_kernel(h_ref, k_ref, v_ref, wq_ref, wo_ref, g_ref, b_ref, o_ref):
    h = h_ref[...]
    q = jnp.dot(h.astype(BF16), wq_ref[...], preferred_element_type=F32)
    parts = []
    for hd in range(XA_HEADS):
        cols = slice(hd * XA_HEADDIM, (hd + 1) * XA_HEADDIM)
        s = _bdot_nt(q[:, cols], k_ref[:, cols]) * XA_HEADDIM ** -0.5
        p = jnp.exp(s - jnp.max(s, axis=-1, keepdims=True))
        p = p / jnp.sum(p, axis=-1, keepdims=True)
        parts.append(_bdot(p, v_ref[:, cols]))
    o = jnp.concatenate(parts, axis=1)
    y = jnp.dot(o.astype(BF16), wo_ref[...], preferred_element_type=F32)
    o_ref[...] = _layer_norm_rows(ALPHA * h + y, g_ref[...], b_ref[...])


def cross_attention_layer(h2, mem2, bsz, seq, w_q, w_kv, w_o, g, b, tq=512):
    t = bsz * seq
    mlen = mem2.shape[0] // bsz
    tq = min(tq, seq)
    nq = seq // tq
    (kv,) = _linear(mem2, [w_kv.astype(BF16)], [BF16], tm=256)
    return pl.pallas_call(
        _cross_attn_kernel,
        name="cross_attention",
        grid=(bsz, nq),
        in_specs=[pl.BlockSpec((tq, D_MODEL), lambda i, j: (i * nq + j, 0)),
                  pl.BlockSpec((mlen, D_MODEL), lambda i, j: (i, 0)),
                  pl.BlockSpec((mlen, D_MODEL), lambda i, j: (i, 1)),
                  _full((D_MODEL, D_MODEL)), _full((D_MODEL, D_MODEL)), _full((1, D_MODEL)), _full((1, D_MODEL))],
        out_specs=pl.BlockSpec((tq, D_MODEL), lambda i, j: (i * nq + j, 0)),
        out_shape=jax.ShapeDtypeStruct((t, D_MODEL), F32),
        compiler_params=_cparams("parallel", "parallel"),
    )(h2, kv, kv, w_q.astype(BF16), w_o.astype(BF16), g.reshape(1, -1), b.reshape(1, -1))


def _staircase():
    return [(a, b) for a in range(PK_TOPK) for b in range(PK_TOPK) if (a + 1) * (b + 1) <= PK_TOPK]


def _peer_select_kernel(h_ref, wq_ref, keys_ref, ea_ref, eb_ref, eidx_ref, gate_ref, q_scr):
    tm = h_ref.shape[0]
    q_scr[...] = jnp.dot(h_ref[...].astype(BF16), wq_ref[...], preferred_element_type=F32).astype(BF16)
    ncand = ea_ref.shape[0]
    key_id = lax.broadcasted_iota(jnp.int32, (PK_NKEYS, tm), 0)
    rank_id = lax.broadcasted_iota(jnp.int32, (PK_TOPK, tm), 0)
    cand_id = lax.broadcasted_iota(jnp.int32, (ncand, tm), 0)
    slot_id = lax.broadcasted_iota(jnp.int32, (PK_SLOTS, tm), 0)
    neg = jnp.float32(-jnp.inf)
    half = PK_QDIM // 2

    def head_body(hh, carry):
        e_all, g_all = carry
        tops = []
        for j in range(2):
            off = pl.multiple_of(hh * PK_QDIM + j * half, half)
            s0 = _bdot_nt(keys_ref[j], q_scr[:, pl.ds(off, half)])

            def top1(a, st):
                s, vals, idxs = st
                m = jnp.max(s, axis=0, keepdims=True)
                idx = jnp.min(jnp.where(s == m, key_id, PK_NKEYS), axis=0, keepdims=True)
                s = jnp.where(key_id == idx, neg, s)
                sel = rank_id == a
                return s, jnp.where(sel, m, vals), jnp.where(sel, idx.astype(F32), idxs)

            zero = jnp.zeros((PK_TOPK, tm), F32)
            _, vals, idxs = lax.fori_loop(0, PK_TOPK, top1, (s0, zero, zero))
            tops.append((vals, idxs))
        (vals0, idxs0), (vals1, idxs1) = tops
        ea = ea_ref[...]
        eb = eb_ref[...]
        cand = (jnp.dot(ea, vals0, precision=HIGHEST, preferred_element_type=F32)
                + jnp.dot(eb, vals1, precision=HIGHEST, preferred_element_type=F32))
        cidx = _bdot(ea, idxs0) * PK_NKEYS + _bdot(eb, idxs1)
        cand = jnp.where(cand_id < len(_staircase()), cand, neg)

        def top2(kk, st):
            cand, cv, ce = st
            m = jnp.max(cand, axis=0, keepdims=True)
            pos = jnp.min(jnp.where(cand == m, cand_id, ncand), axis=0, keepdims=True)
            hit = cand_id == pos
            e = jnp.max(jnp.where(hit, cidx, -1.0), axis=0, keepdims=True)
            cand = jnp.where(hit, neg, cand)
            sel = rank_id == kk
            return cand, jnp.where(sel, m, cv), jnp.where(sel, e, ce)

        zero = jnp.zeros((PK_TOPK, tm), F32)
        _, cv, ce = lax.fori_loop(0, PK_TOPK, top2, (cand, zero, zero))
        p = jnp.exp(cv - cv[0:1, :])
        gate = p / jnp.sum(p, axis=0, keepdims=True)
        mine = (slot_id // PK_TOPK) == hh
        tile8 = lambda x: jnp.concatenate([x] * PK_HEADS, axis=0)
        return jnp.where(mine, tile8(ce), e_all), jnp.where(mine, tile8(gate), g_all)

    zero = jnp.zeros((PK_SLOTS, tm), F32)
    e_all, g_all = lax.fori_loop(0, PK_HEADS, head_body, (zero, zero))
    eidx_ref[...] = (e_all.T * ROW_WORDS).astype(jnp.int32)
    gate_ref[...] = g_all.T


def _peer_select(h2, wq_bf, keys_bf, tm=128):
    t = h2.shape[0]
    tm = min(tm, t)
    pairs = _staircase()
    ncand = -(-len(pairs) // 8) * 8
    ea = jnp.zeros((ncand, PK_TOPK), F32).at[jnp.arange(len(pairs)), jnp.array([a for a, _ in pairs])].set(1.0)
    eb = jnp.zeros((ncand, PK_TOPK), F32).at[jnp.arange(len(pairs)), jnp.array([b for _, b in pairs])].set(1.0)
    return pl.pallas_call(
        _peer_select_kernel,
        name="peer_select",
        grid=(t // tm,),
        in_specs=[pl.BlockSpec((tm, D_MODEL), lambda i: (i, 0)), _full(wq_bf.shape), _full(keys_bf.shape),
                  _full(ea.shape), _full(eb.shape)],
        out_specs=[pl.BlockSpec((tm, PK_SLOTS), lambda i: (i, 0)),
                   pl.BlockSpec((tm, PK_SLOTS), lambda i: (i, 0))],
        out_shape=[jax.ShapeDtypeStruct((t, PK_SLOTS), jnp.int32),
                   jax.ShapeDtypeStruct((t, PK_SLOTS), F32)],
        scratch_shapes=[pltpu.VMEM((tm, PK_HEADS * PK_QDIM), BF16)],
        compiler_params=_cparams("parallel"),
    )(h2, wq_bf, keys_bf, ea, eb)


def _pack_rows(w):
    e, d = w.shape
    wb = lax.bitcast_convert_type(w.astype(BF16), jnp.uint16).astype(jnp.uint32)
    packed = (wb[:, d // 2:] << 16) | wb[:, :d // 2]
    return lax.bitcast_convert_type(packed, jnp.int32).reshape(e * ROW_WORDS, 128)


def _unpack_lo(w):
    return pltpu.bitcast(w << 16, F32)


def _unpack_hi(w):
    return pltpu.bitcast(w & jnp.int32(-65536), F32)


def _peer_score_kernel(idx_ref, x_ref, gate_ref, tab_ref, act_ref, tile_ref):
    tb = x_ref.shape[0]
    lane = lax.broadcasted_iota(jnp.int32, (PK_SLOTS, tb), 1)

    def token(t, cols):
        for r in range(PK_SLOTS):
            i = pl.multiple_of(idx_ref[t, r], ROW_WORDS)
            tile_ref[pl.ds(r * ROW_WORDS, ROW_WORDS), :] = tab_ref[pl.ds(i, ROW_WORDS), :]
        acc = jnp.zeros((PK_SLOTS, 128), F32)
        xrow = x_ref[t]
        for s in range(ROW_WORDS):
            w = tile_ref[pl.ds(s, PK_SLOTS, stride=ROW_WORDS), :]
            acc = acc + _unpack_lo(w) * xrow[s:s + 1, :] + _unpack_hi(w) * xrow[ROW_WORDS + s:ROW_WORDS + s + 1, :]
        col = jnp.sum(acc, axis=-1, keepdims=True)
        return jnp.where(lane == t, col, cols)

    cols = lax.fori_loop(0, tb, token, jnp.zeros((PK_SLOTS, tb), F32))
    act_ref[...] = _gelu(cols.T) * gate_ref[...]


def _peer_score(eidx, x2, gate, tab, tb=128):
    t = x2.shape[0]
    tb = min(tb, t)
    return pl.pallas_call(
        _peer_score_kernel,
        name="peer_score",
        grid=(t // tb,),
        in_specs=[pl.BlockSpec((tb, PK_SLOTS), lambda i: (i, 0), memory_space=pltpu.SMEM),
                  pl.BlockSpec((tb, 8, 128), lambda i: (i, 0, 0)),
                  pl.BlockSpec((tb, PK_SLOTS), lambda i: (i, 0)),
                  _full(tab.shape)],
        out_specs=pl.BlockSpec((tb, PK_SLOTS), lambda i: (i, 0)),
        out_shape=jax.ShapeDtypeStruct((t, PK_SLOTS), F32),
        scratch_shapes=[pltpu.VMEM((PK_SLOTS * ROW_WORDS, 128), jnp.int32)],
        compiler_params=_cparams("parallel"),
    )(eidx, x2.reshape(t, 8, 128), gate, tab)


def _peer_combine_kernel(idx_ref, act_ref, h_ref, g_ref, b_ref, tab_ref, out_ref, y_ref):
    tb = h_ref.shape[0]
    n_acc = 4

    def token(t, carry):
        lo = [jnp.zeros((ROW_WORDS, 128), F32) for _ in range(n_acc)]
        hi = [jnp.zeros((ROW_WORDS, 128), F32) for _ in range(n_acc)]
        for r in range(PK_SLOTS):
            i = pl.multiple_of(idx_ref[t, r], ROW_WORDS)
            w = tab_ref[pl.ds(i, ROW_WORDS), :]
            a = act_ref[t, r]
            lo[r % n_acc] = lo[r % n_acc] + a * _unpack_lo(w)
            hi[r % n_acc] = hi[r % n_acc] + a * _unpack_hi(w)
        lo_sum = (lo[0] + lo[1]) + (lo[2] + lo[3])
        hi_sum = (hi[0] + hi[1]) + (hi[2] + hi[3])
        y_ref[t] = jnp.concatenate([lo_sum, hi_sum], axis=0)
        return carry

    lax.fori_loop(0, tb, token, 0)
    v = ALPHA * h_ref[...] + y_ref[...]
    mu = jnp.mean(v, axis=(1, 2), keepdims=True)
    c = v - mu
    var = jnp.mean(c * c, axis=(1, 2), keepdims=True)
    out_ref[...] = c * lax.rsqrt(var + LN_EPS) * g_ref[...] + b_ref[...]


def _peer_combine(eidx, act, h2, g, b, tab, tb=128):
    t = h2.shape[0]
    tb = min(tb, t)
    tok = pl.BlockSpec((tb, 8, 128), lambda i: (i, 0, 0))
    smem = pl.BlockSpec((tb, PK_SLOTS), lambda i: (i, 0), memory_space=pltpu.SMEM)
    return pl.pallas_call(
        _peer_combine_kernel,
        name="peer_combine",
        grid=(t // tb,),
        in_specs=[smem, smem, tok, _full((1, 8, 128)), _full((1, 8, 128)), _full(tab.shape)],
        out_specs=tok,
        out_shape=jax.ShapeDtypeStruct((t, 8, 128), F32),
        scratch_shapes=[pltpu.VMEM((tb, 8, 128), F32)],
        compiler_params=_cparams("parallel"),
    )(eidx, act, h2.reshape(t, 8, 128), g.reshape(1, 8, 128), b.reshape(1, 8, 128), tab).reshape(t, D_MODEL)


def peer_layer(h2, w_query, sub_keys, u, v, g, b):
    eidx, gate = _peer_select(h2, w_query.astype(BF16), sub_keys.astype(BF16))
    act = _peer_score(eidx, h2, gate, _pack_rows(u))
    return _peer_combine(eidx, act, h2, g, b, _pack_rows(v))


def kernel(x, mem, s5_lam_re, s5_lam_im, s5_log_dt, s5_b_re, s5_b_im, s5_c_re, s5_c_im, s5_d, s5_w_glu, s5_b_glu, da_w_qkv, da_lambda, da_subln_g, da_w_o, m2_w_in, m2_conv_w, m2_conv_b, m2_dt_bias, m2_a_log, m2_d, m2_norm_g, m2_w_out, ml_w_in, ml_conv_w, ml_conv_b, ml_w_q, ml_w_k, ml_w_v, ml_w_gates, ml_b_gates, ml_norm_g, ml_skip, ml_w_down, xa_w_q, xa_w_kv, xa_w_o, pk_w_query, pk_sub_keys, pk_u, pk_v, ln_g, ln_b):
    bsz, seq, dm = x.shape
    t = bsz * seq
    mem2 = mem.reshape(-1, dm)
    h = x.reshape(t, dm)
    for i in range(DEPTH):
        kind, j = i % 4, i // 4
        g0, b0 = ln_g[i, 0], ln_b[i, 0]
        if kind == 0:
            h = s5_layer(h.reshape(bsz, seq, dm), s5_lam_re[j], s5_lam_im[j], s5_log_dt[j], s5_b_re[j], s5_b_im[j],
                         s5_c_re[j], s5_c_im[j], s5_d[j], s5_w_glu[j], s5_b_glu[j], g0, b0)
        elif kind == 1:
            h = diff_attention_layer(h, bsz, seq, da_w_qkv[j], da_lambda[j], da_subln_g[j], da_w_o[j], i, g0, b0)
        elif kind == 2:
            h = mamba2_layer(h, bsz, seq, m2_w_in[j], m2_conv_w[j], m2_conv_b[j], m2_dt_bias[j], m2_a_log[j],
                             m2_d[j], m2_norm_g[j], m2_w_out[j], g0, b0)
        else:
            h = mlstm_layer(h, bsz, seq, ml_w_in[j], ml_conv_w[j], ml_conv_b[j], ml_w_q[j], ml_w_k[j], ml_w_v[j],
                            ml_w_gates[j], ml_b_gates[j], ml_norm_g[j], ml_skip[j], ml_w_down[j], g0, b0)
        h = cross_attention_layer(h, mem2, bsz, seq, xa_w_q[i], xa_w_kv[i], xa_w_o[i], ln_g[i, 1], ln_b[i, 1])
        h = peer_layer(h, pk_w_query[i], pk_sub_keys[i], pk_u[i], pk_v[i], ln_g[i, 2], ln_b[i, 2])
    return h.reshape(bsz, seq, dm)
```

```python
import functools
import math

import jax
import jax.numpy as jnp
from jax import lax
from jax.experimental import pallas as pl
from jax.experimental.pallas import tpu as pltpu

F32 = jnp.float32
BF16 = jnp.bfloat16
HIGHEST = lax.Precision.HIGHEST

D_MODEL = 1024
DEPTH = 4
ALPHA = (2 * DEPTH) ** 0.25
LN_EPS = 1e-5
NEG_INF = -1e30
CHUNK = 64

S5_GROUP = 16
S5_STATE = 64
S5_Q = 64

DA_HEAD = 64
DA_HEADS = 8
DA_TILE = 512
DA_ROWS = 128

M2_INNER = 2048
M2_HEADS = 32
M2_HEADDIM = 64
M2_GROUPS = 4
M2_STATE = 128
M2_CONV = 4
M2_Q = 256

ML_INNER = 2048
ML_HEADS = 4
ML_HEADDIM = 512
ML_CONV = 4
ML_Q = 256

XA_HEADS = 4
XA_HEADDIM = 256

PK_HEADS = 8
PK_NKEYS = 128
PK_QDIM = 256
PK_TOPK = 16
PK_SLOTS = PK_HEADS * PK_TOPK
ROW_WORDS = D_MODEL // 2 // 128

VMEM_LIMIT = 56 * 1024 * 1024


def _cparams(*sem):
    return pltpu.CompilerParams(dimension_semantics=sem, vmem_limit_bytes=VMEM_LIMIT)


def _gelu(x):
    return 0.5 * x * (1.0 + jnp.tanh(math.sqrt(2.0 / math.pi) * (x + 0.044715 * x * x * x)))


def _sigmoid(x):
    return 1.0 / (1.0 + jnp.exp(-x))


def _silu(x):
    return x * _sigmoid(x)


def _softplus(x):
    return jnp.maximum(x, 0.0) + jnp.log(1.0 + jnp.exp(-jnp.abs(x)))


def _layer_norm_rows(v, g, b):
    mu = jnp.mean(v, axis=-1, keepdims=True)
    c = v - mu
    var = jnp.mean(c * c, axis=-1, keepdims=True)
    return c * lax.rsqrt(var + LN_EPS) * g + b


def _bdot(a, b):
    return jnp.dot(a.astype(BF16), b.astype(BF16), preferred_element_type=F32)


def _bdot_nt(a, b):
    return lax.dot_general(a.astype(BF16), b.astype(BF16), (((1,), (1,)), ((), ())), preferred_element_type=F32)


def _bdot_tn(a, b):
    return lax.dot_general(a.astype(BF16), b.astype(BF16), (((0,), (0,)), ((), ())), preferred_element_type=F32)


def _full(shape):
    return pl.BlockSpec(shape, lambda *_: (0,) * len(shape))


def _causal_conv(x, halo, w, b):
    q = x.shape[0]
    k = w.shape[0]
    ext = jnp.concatenate([halo, x], axis=0)
    out = x * w[k - 1:k, :] + b
    for d in range(1, k):
        out = out + ext[8 - d:8 - d + q, :] * w[k - 1 - d:k - d, :]
    return out


def _cumsum_rows(a, tri):
    return jnp.dot(tri, a, precision=HIGHEST, preferred_element_type=F32)


def _linear_kernel(*refs, n_out):
    a = refs[0][...].astype(BF16)
    for w_ref, o_ref in zip(refs[1:1 + n_out], refs[1 + n_out:]):
        o_ref[...] = jnp.dot(a, w_ref[...], preferred_element_type=F32).astype(o_ref.dtype)


def _linear(a, ws, out_dtypes, tm=512):
    m, k = a.shape
    tm = min(tm, m)
    return pl.pallas_call(
        functools.partial(_linear_kernel, n_out=len(ws)),
        name="linear",
        grid=(m // tm,),
        in_specs=[pl.BlockSpec((tm, k), lambda i: (i, 0))] + [_full(w.shape) for w in ws],
        out_specs=[pl.BlockSpec((tm, w.shape[1]), lambda i: (i, 0)) for w in ws],
        out_shape=[jax.ShapeDtypeStruct((m, w.shape[1]), dt) for w, dt in zip(ws, out_dtypes)],
        compiler_params=_cparams("parallel"),
    )(a, *ws)


def _linear_res_ln_kernel(a_ref, w_ref, h_ref, g_ref, b_ref, o_ref):
    y = jnp.dot(a_ref[...].astype(BF16), w_ref[...], preferred_element_type=F32)
    o_ref[...] = _layer_norm_rows(ALPHA * h_ref[...] + y, g_ref[...], b_ref[...])


def _linear_res_ln(a, w, h, g, b, tm=512):
    m, k = a.shape
    tm = min(tm, m)
    return pl.pallas_call(
        _linear_res_ln_kernel,
        name="linear_res_ln",
        grid=(m // tm,),
        in_specs=[pl.BlockSpec((tm, k), lambda i: (i, 0)), _full(w.shape),
                  pl.BlockSpec((tm, D_MODEL), lambda i: (i, 0)), _full((1, D_MODEL)), _full((1, D_MODEL))],
        out_specs=pl.BlockSpec((tm, D_MODEL), lambda i: (i, 0)),
        out_shape=jax.ShapeDtypeStruct((m, D_MODEL), F32),
        compiler_params=_cparams("parallel"),
    )(a, w, h, g.reshape(1, -1), b.reshape(1, -1))


def _s5_tables(lam_re, lam_im, log_dt, b_re, b_im, c_re, c_im):
    q = S5_Q
    g, p = lam_re.shape
    lam = lax.complex(lam_re.astype(F32), lam_im.astype(F32))
    dt = jnp.exp(log_dt.astype(F32))[:, None]
    lam_bar = jnp.exp(lam * dt)
    b_bar = ((lam_bar - 1.0) / lam)[..., None] * lax.complex(b_re.astype(F32), b_im.astype(F32))
    c_mat = lax.complex(c_re.astype(F32), c_im.astype(F32))
    pw = jnp.cumprod(jnp.concatenate([jnp.ones((1, g, p), lam_bar.dtype),
                                      jnp.broadcast_to(lam_bar, (q, g, p))], axis=0), axis=0)
    cp = c_mat[None] * pw[:q, :, None, :]
    taps = (jnp.einsum('tgop,gpi->gtoi', cp.real, b_bar.real, precision=HIGHEST)
            - jnp.einsum('tgop,gpi->gtoi', cp.imag, b_bar.imag, precision=HIGHEST))
    pos = jnp.arange(q)
    tau = pos[None, :] - pos[:, None]
    toep = jnp.where((tau >= 0)[None, :, :, None, None], taps[:, jnp.maximum(tau, 0)], 0.0)
    toep = toep.transpose(0, 1, 4, 2, 3).reshape(g, q * S5_GROUP, q * S5_GROUP)
    wb = pw[q - 1 - pos][:, :, :, None] * b_bar[None]
    wt = jnp.concatenate([wb.real, wb.imag], axis=2).transpose(1, 0, 3, 2).reshape(g, q * S5_GROUP, 2 * p)
    cv = c_mat[None] * pw[1:q + 1, :, None, :]
    vt = jnp.concatenate([cv.real, -cv.imag], axis=3).transpose(1, 3, 0, 2).reshape(g, 2 * p, q * S5_GROUP)
    aq = pw[q]
    a_re = jnp.concatenate([aq.real, aq.real], axis=1).reshape(g, 1, 2 * p)
    a_im = jnp.concatenate([-aq.imag, aq.imag], axis=1).reshape(g, 1, 2 * p)
    return toep.astype(BF16), wt.astype(BF16), vt.astype(BF16), a_re, a_im


def _s5_kernel(x_ref, tt_ref, wt_ref, vt_ref, ar_ref, ai_ref, y_ref, s_scr, h_scr, *, nb):
    x = x_ref[0]
    y_ref[0] = jnp.dot(x, tt_ref[0], preferred_element_type=F32)
    s_scr[...] = jnp.dot(x, wt_ref[0], preferred_element_type=F32)
    a_re = ar_ref[0]
    a_im = ai_ref[0]
    half = s_scr.shape[1] // 2

    def step(c, h):
        rows = pl.ds(pl.multiple_of(c * nb, nb), nb)
        h_scr[rows, :] = h
        return a_re * h + a_im * pltpu.roll(h, half, axis=1) + s_scr[rows, :]

    lax.fori_loop(0, x.shape[0] // nb, step, jnp.zeros((nb, s_scr.shape[1]), F32))
    y_ref[0] += jnp.dot(h_scr[...].astype(BF16), vt_ref[0], preferred_element_type=F32)


def _s5_scan(x3, lam_re, lam_im, log_dt, b_re, b_im, c_re, c_im):
    bsz, seq, dm = x3.shape
    g = dm // S5_GROUP
    q = S5_Q
    nc = seq // q
    m = nc * bsz
    toep, wt, vt, a_re, a_im = _s5_tables(lam_re, lam_im, log_dt, b_re, b_im, c_re, c_im)
    xg = x3.reshape(bsz, nc, q, g, S5_GROUP).transpose(3, 1, 0, 2, 4).reshape(g, m, q * S5_GROUP).astype(BF16)
    w = q * S5_GROUP
    y = pl.pallas_call(
        functools.partial(_s5_kernel, nb=bsz),
        name="s5_scan",
        grid=(g,),
        in_specs=[pl.BlockSpec((1, m, w), lambda i: (i, 0, 0)),
                  pl.BlockSpec((1, w, w), lambda i: (i, 0, 0)),
                  pl.BlockSpec((1, w, 2 * S5_STATE), lambda i: (i, 0, 0)),
                  pl.BlockSpec((1, 2 * S5_STATE, w), lambda i: (i, 0, 0)),
                  pl.BlockSpec((1, 1, 2 * S5_STATE), lambda i: (i, 0, 0)),
                  pl.BlockSpec((1, 1, 2 * S5_STATE), lambda i: (i, 0, 0))],
        out_specs=pl.BlockSpec((1, m, w), lambda i: (i, 0, 0)),
        out_shape=jax.ShapeDtypeStruct((g, m, w), F32),
        scratch_shapes=[pltpu.VMEM((m, 2 * S5_STATE), F32), pltpu.VMEM((m, 2 * S5_STATE), F32)],
        compiler_params=_cparams("parallel"),
    )(xg, toep, wt, vt, a_re, a_im)
    return y.reshape(g, nc, bsz, q, S5_GROUP).transpose(2, 1, 3, 0, 4).reshape(bsz, seq, dm)


def _s5_out_kernel(y_ref, x_ref, d_ref, w_ref, bias_ref, g_ref, b_ref, o_ref):
    x = x_ref[...]
    y = _gelu(y_ref[...] + d_ref[...] * x)
    z = jnp.dot(y.astype(BF16), w_ref[...], preferred_element_type=F32) + bias_ref[...]
    out = z[:, :D_MODEL] * _sigmoid(z[:, D_MODEL:])
    o_ref[...] = _layer_norm_rows(ALPHA * x + out, g_ref[...], b_ref[...])


def s5_layer(x3, lam_re, lam_im, log_dt, b_re, b_im, c_re, c_im, d_skip, w_glu, b_glu, g, b, tm=512):
    bsz, seq, dm = x3.shape
    t = bsz * seq
    tm = min(tm, t)
    y = _s5_scan(x3, lam_re, lam_im, log_dt, b_re, b_im, c_re, c_im).reshape(t, dm)
    row = pl.BlockSpec((tm, dm), lambda i: (i, 0))
    return pl.pallas_call(
        _s5_out_kernel,
        name="s5_out",
        grid=(t // tm,),
        in_specs=[row, row, _full((1, dm)), _full((dm, 2 * dm)), _full((1, 2 * dm)), _full((1, dm)), _full((1, dm))],
        out_specs=row,
        out_shape=jax.ShapeDtypeStruct((t, dm), F32),
        compiler_params=_cparams("parallel"),
    )(y, x3.reshape(t, dm), d_skip.reshape(1, -1), w_glu.astype(BF16), b_glu.reshape(1, -1),
      g.reshape(1, -1), b.reshape(1, -1))


def _diff_attn_kernel(qi_ref, kj_ref, sc_ref, q_ref, k_ref, v_ref, rel_ref, g_ref, o_ref, m_scr, l_scr, acc_scr):
    hh = pl.program_id(1)
    pair = pl.program_id(2)
    qi = qi_ref[pair]
    kj = kj_ref[pair]
    tq = q_ref.shape[0]
    slope = sc_ref[0, hh]

    @pl.when(kj == 0)
    def _():
        m_scr[...] = jnp.full(m_scr.shape, NEG_INF, F32)
        l_scr[...] = jnp.zeros(l_scr.shape, F32)
        acc_scr[...] = jnp.zeros(acc_scr.shape, F32)

    def accumulate(bias_fn, shift):
        k = k_ref[...]
        v = v_ref[...]
        rb = min(DA_ROWS, tq)
        lane = lax.broadcasted_iota(jnp.int32, (rb, q_ref.shape[1]), 1)
        old = [(m_scr[j], l_scr[j], acc_scr[j]) for j in range(2)]
        new = [([], [], []) for _ in range(2)]
        blocks = [slice(blk * rb, (blk + 1) * rb) for blk in range(tq // rb)]
        scores = {}
        for rows in blocks:
            q = q_ref[rows, :]
            for j in range(2):
                qj = jnp.where((lane // DA_HEAD) == j, q, jnp.zeros_like(q)) * jnp.asarray(DA_HEAD ** -0.5, q.dtype)
                scores[rows.start, j] = lax.dot_general(qj, k, (((1,), (1,)), ((), ())), preferred_element_type=F32)
        for rows in blocks:
            for j in range(2):
                s = bias_fn(scores[rows.start, j], rows)
                m_old = old[j][0][rows, :]
                m_new = jnp.maximum(m_old, jnp.max(s, axis=-1, keepdims=True) + shift)
                p = jnp.exp(s - (m_new - shift))
                corr = jnp.exp(m_old - m_new)
                new[j][0].append(m_new)
                new[j][1].append(corr * old[j][1][rows, :] + jnp.sum(p, axis=-1, keepdims=True))
                new[j][2].append(corr * old[j][2][rows, :] + jnp.dot(p.astype(BF16), v, preferred_element_type=F32))
        for j in range(2):
            m_scr[j] = jnp.concatenate(new[j][0], axis=0)
            l_scr[j] = jnp.concatenate(new[j][1], axis=0)
            acc_scr[j] = jnp.concatenate(new[j][2], axis=0)

    @pl.when(kj < qi)
    def _():
        accumulate(lambda s, rows: s + rel_ref[rows, :] * (-slope), -slope * ((qi - kj) * tq).astype(F32))

    @pl.when(kj == qi)
    def _():
        def masked(s, rows):
            row = lax.broadcasted_iota(jnp.int32, s.shape, 0) + rows.start
            col = lax.broadcasted_iota(jnp.int32, s.shape, 1)
            allowed = (col // CHUNK) <= (row // CHUNK)
            return jnp.where(allowed, s + jnp.abs(rel_ref[rows, :]) * (-slope), NEG_INF)

        accumulate(masked, 0.0)
        lam_full = sc_ref[1, 0]
        out_scale = sc_ref[1, 1]
        o = acc_scr[0] / l_scr[0] - lam_full * (acc_scr[1] / l_scr[1])
        o = o * lax.rsqrt(jnp.mean(o * o, axis=-1, keepdims=True) + LN_EPS)
        o_ref[...] = (o * g_ref[...] * out_scale).astype(o_ref.dtype)


def _diff_attention(qkv, scalars, subln_g, bsz, seq):
    t = bsz * seq
    tq = min(DA_TILE, seq)
    nq = seq // tq
    n_h = DA_HEADS
    pairs = [(i, j) for i in range(nq) for j in range(i + 1)]
    qi_of = jnp.array([i for i, _ in pairs], jnp.int32)
    kj_of = jnp.array([j for _, j in pairs], jnp.int32)
    pos = jnp.arange(tq, dtype=F32)
    rel = pos[:, None] - pos[None, :]
    grid_spec = pltpu.PrefetchScalarGridSpec(
        num_scalar_prefetch=2,
        grid=(bsz, n_h, len(pairs)),
        in_specs=[pl.BlockSpec(memory_space=pltpu.SMEM),
                  pl.BlockSpec((tq, 2 * DA_HEAD), lambda b, h, p, qi, kj: (b * nq + qi[p], h)),
                  pl.BlockSpec((tq, 2 * DA_HEAD), lambda b, h, p, qi, kj: (b * nq + kj[p], n_h + h)),
                  pl.BlockSpec((tq, 2 * DA_HEAD), lambda b, h, p, qi, kj: (b * nq + kj[p], 2 * n_h + h)),
                  pl.BlockSpec((tq, tq), lambda b, h, p, qi, kj: (0, 0)),
                  pl.BlockSpec((1, 2 * DA_HEAD), lambda b, h, p, qi, kj: (0, 0))],
        out_specs=pl.BlockSpec((tq, 2 * DA_HEAD), lambda b, h, p, qi, kj: (b * nq + qi[p], h)),
        scratch_shapes=[pltpu.VMEM((2, tq, 1), F32), pltpu.VMEM((2, tq, 1), F32),
                        pltpu.VMEM((2, tq, 2 * DA_HEAD), F32)])
    return pl.pallas_call(
        _diff_attn_kernel,
        name="diff_attention",
        grid_spec=grid_spec,
        out_shape=jax.ShapeDtypeStruct((t, D_MODEL), BF16),
        compiler_params=_cparams("parallel", "parallel", "arbitrary"),
    )(qi_of, kj_of, scalars, qkv, qkv, qkv, rel, subln_g.reshape(1, -1))


def diff_attention_layer(h2, bsz, seq, w_qkv, lam, subln_g, w_o, layer_idx, g, b):
    lam_init = 0.8 - 0.6 * math.exp(-0.3 * layer_idx)
    lf = lam.astype(F32)
    lam_full = jnp.exp(jnp.sum(lf[0] * lf[1])) - jnp.exp(jnp.sum(lf[2] * lf[3])) + lam_init
    slopes = 2.0 ** (-8.0 * jnp.arange(1, DA_HEADS + 1, dtype=F32) / DA_HEADS)
    scalars = jnp.stack([slopes, jnp.zeros((DA_HEADS,), F32).at[0].set(lam_full).at[1].set(1.0 - lam_init)])
    (qkv,) = _linear(h2, [w_qkv.astype(BF16)], [BF16])
    o = _diff_attention(qkv, scalars, subln_g, bsz, seq)
    return _linear_res_ln(o, w_o.astype(BF16), h2, g, b)


def _mamba_kernel(z_ref, xbc_ref, dt_ref, h_ref, cw_ref, cb_ref, dtb_ref, a_ref, d_ref, ng_ref, wo_ref,
                  lg_ref, lb_ref, o_ref, state_scr, halo_scr, y_scr):
    q = z_ref.shape[0]
    n = M2_STATE

    @pl.when(pl.program_id(1) == 0)
    def _():
        state_scr[...] = jnp.zeros(state_scr.shape, F32)
        halo_scr[...] = jnp.zeros(halo_scr.shape, F32)

    xbc_raw = xbc_ref[...]
    xbc = _silu(_causal_conv(xbc_raw, halo_scr[...], cw_ref[...], cb_ref[...]))
    halo_scr[...] = xbc_raw[q - 8:, :]
    row = lax.broadcasted_iota(jnp.int32, (q, q), 0)
    col = lax.broadcasted_iota(jnp.int32, (q, q), 1)
    lower = row >= col
    tri = lower.astype(F32)
    dt = _softplus(dt_ref[...] + dtb_ref[...])
    a_cs = _cumsum_rows(dt * a_ref[...], tri)
    a_cs_t = a_cs.T
    a_last = a_cs[q - 1:q, :]
    e_cs = jnp.exp(a_cs)
    e_dec = jnp.exp(a_last - a_cs)
    e_last = jnp.exp(a_last)
    lane = lax.broadcasted_iota(jnp.int32, (q, 2 * M2_HEADDIM), 1)
    first = lane < M2_HEADDIM
    srow = lax.broadcasted_iota(jnp.int32, (2 * M2_HEADDIM, n), 0)
    heads_per_group = M2_HEADS // M2_GROUPS
    for grp in range(M2_GROUPS):
        bm = xbc[:, M2_INNER + grp * n:M2_INNER + (grp + 1) * n]
        cm = xbc[:, M2_INNER + M2_GROUPS * n + grp * n:M2_INNER + M2_GROUPS * n + (grp + 1) * n]
        cb = _bdot_nt(cm, bm)
        for pair in range(heads_per_group // 2):
            h0 = grp * heads_per_group + 2 * pair
            cols = slice(h0 * M2_HEADDIM, (h0 + 2) * M2_HEADDIM)
            xs = xbc[:, cols]
            dtp = jnp.where(first, dt[:, h0:h0 + 1], dt[:, h0 + 1:h0 + 2])
            xdt = xs * dtp
            y = jnp.zeros((q, 2 * M2_HEADDIM), F32)
            for s in range(2):
                hd = h0 + s
                seg = a_cs[:, hd:hd + 1] - a_cs_t[hd:hd + 1, :]
                lmat = jnp.exp(jnp.where(lower, seg, NEG_INF))
                part = jnp.where(first if s == 0 else jnp.logical_not(first), xdt, 0.0)
                y = y + _bdot(cb * lmat, part)
            state = state_scr[pl.ds(h0 * M2_HEADDIM, 2 * M2_HEADDIM), :]
            y_off = _bdot_nt(cm, state)
            y = y + y_off * jnp.where(first, e_cs[:, h0:h0 + 1], e_cs[:, h0 + 1:h0 + 2])
            dec = jnp.where(first, e_dec[:, h0:h0 + 1], e_dec[:, h0 + 1:h0 + 2])
            grow = jnp.where(srow < M2_HEADDIM, e_last[:, h0:h0 + 1], e_last[:, h0 + 1:h0 + 2])
            state_scr[pl.ds(h0 * M2_HEADDIM, 2 * M2_HEADDIM), :] = state * grow + _bdot_tn(xdt * dec, bm)
            y_scr[:, cols] = y + d_ref[:, cols] * xs
    gate = _silu(z_ref[...])
    y = y_scr[...] * gate
    gw = M2_INNER // M2_GROUPS
    parts = []
    for grp in range(M2_GROUPS):
        yg = y[:, grp * gw:(grp + 1) * gw]
        parts.append(yg * lax.rsqrt(jnp.mean(yg * yg, axis=-1, keepdims=True) + LN_EPS))
    yn = jnp.concatenate(parts, axis=1) * ng_ref[...]
    out = jnp.dot(yn.astype(BF16), wo_ref[...], preferred_element_type=F32)
    o_ref[...] = _layer_norm_rows(ALPHA * h_ref[...] + out, lg_ref[...], lb_ref[...])


def mamba2_layer(h2, bsz, seq, w_in, conv_w, conv_b, dt_bias, a_log, d_skip, norm_g, w_out, g, b):
    t = bsz * seq
    q = min(M2_Q, seq)
    nc = seq // q
    xbc_w = M2_INNER + 2 * M2_GROUPS * M2_STATE
    wb = w_in.astype(BF16)
    hpad = 128 - M2_HEADS
    pad_heads = lambda a: jnp.pad(a.astype(F32).reshape(1, -1), ((0, 0), (0, hpad)))
    z, xbc, dt = _linear(h2, [wb[:, :M2_INNER], wb[:, M2_INNER:M2_INNER + xbc_w],
                              jnp.pad(wb[:, M2_INNER + xbc_w:], ((0, 0), (0, hpad)))],
                         [F32, F32, F32], tm=256)
    d_cols = jnp.repeat(d_skip.astype(F32), M2_HEADDIM).reshape(1, -1)
    rowspec = lambda w: pl.BlockSpec((q, w), lambda i, j: (i * nc + j, 0))
    return pl.pallas_call(
        _mamba_kernel,
        name="mamba2",
        grid=(bsz, nc),
        in_specs=[rowspec(M2_INNER), rowspec(xbc_w), rowspec(128), rowspec(D_MODEL),
                  _full((M2_CONV, xbc_w)), _full((1, xbc_w)), _full((1, 128)), _full((1, 128)),
                  _full((1, M2_INNER)), _full((1, M2_INNER)), _full((M2_INNER, D_MODEL)),
                  _full((1, D_MODEL)), _full((1, D_MODEL))],
        out_specs=rowspec(D_MODEL),
        out_shape=jax.ShapeDtypeStruct((t, D_MODEL), F32),
        scratch_shapes=[pltpu.VMEM((M2_HEADS * M2_HEADDIM, M2_STATE), F32),
                        pltpu.VMEM((8, xbc_w), F32),
                        pltpu.VMEM((q, M2_INNER), F32)],
        compiler_params=_cparams("parallel", "arbitrary"),
    )(z, xbc, dt, h2, conv_w.astype(F32), conv_b.reshape(1, -1), pad_heads(dt_bias),
      pad_heads(-jnp.exp(a_log.astype(F32))), d_cols, norm_g.reshape(1, -1), w_out.astype(BF16),
      g.reshape(1, -1), b.reshape(1, -1))


def _mlstm_proj_kernel(xm_ref, cw_ref, cb_ref, wq_ref, wk_ref, wv_ref, wg_ref, bg_ref,
                       xc_ref, q_ref, k_ref, v_ref, gate_ref, halo_scr):
    tq = xm_ref.shape[0]

    @pl.when(pl.program_id(1) == 0)
    def _():
        halo_scr[...] = jnp.zeros(halo_scr.shape, F32)

    xm = xm_ref[...]
    xc = _silu(_causal_conv(xm, halo_scr[...], cw_ref[...], cb_ref[...]))
    halo_scr[...] = xm[tq - 8:, :]
    xc_ref[...] = xc
    gates = jnp.zeros((tq, bg_ref.shape[1]), F32) + bg_ref[...]
    for hd in range(ML_HEADS):
        cols = slice(hd * ML_HEADDIM, (hd + 1) * ML_HEADDIM)
        qh = _bdot(xc[:, cols], wq_ref[hd])
        kh = _bdot(xc[:, cols], wk_ref[hd]) * ML_HEADDIM ** -0.5
        vh = _bdot(xm[:, cols], wv_ref[hd])
        q_ref[:, cols] = qh.astype(BF16)
        k_ref[:, cols] = kh.astype(BF16)
        v_ref[:, cols] = vh.astype(BF16)
        gates = gates + _bdot(qh, wg_ref[0, cols, :]) + _bdot(kh, wg_ref[1, cols, :]) + _bdot(vh, wg_ref[2, cols, :])
    gate_ref[...] = gates


def _mlstm_cell_kernel(q_ref, k_ref, v_ref, gate_ref, h_ref, c_scr, n_scr, m_scr):
    hd = pl.program_id(1)
    qn = q_ref.shape[0]

    @pl.when(pl.program_id(2) == 0)
    def _():
        c_scr[...] = jnp.zeros(c_scr.shape, F32)
        n_scr[...] = jnp.zeros(n_scr.shape, F32)
        m_scr[...] = jnp.zeros(m_scr.shape, F32)

    q = q_ref[...]
    k = k_ref[...]
    v = v_ref[...]
    gates = gate_ref[...]
    glane = lax.broadcasted_iota(jnp.int32, gates.shape, 1)
    gsub = lax.broadcasted_iota(jnp.int32, (gates.shape[1], qn), 0)
    row = lax.broadcasted_iota(jnp.int32, (qn, qn), 0)
    col = lax.broadcasted_iota(jnp.int32, (qn, qn), 1)
    lower = row >= col
    logf = jnp.minimum(gates, 0.0) - jnp.log(1.0 + jnp.exp(-jnp.abs(gates)))
    cum = _cumsum_rows(logf, lower.astype(F32))
    ii = jnp.sum(jnp.where(glane == hd, gates, 0.0), axis=-1, keepdims=True)
    bcs = jnp.sum(jnp.where(glane == ML_HEADS + hd, cum, 0.0), axis=-1, keepdims=True)
    ii_row = jnp.sum(jnp.where(gsub == hd, gates.T, 0.0), axis=0, keepdims=True)
    bcs_row = jnp.sum(jnp.where(gsub == ML_HEADS + hd, cum.T, 0.0), axis=0, keepdims=True)
    m_prev = m_scr[0:1, 0:1]
    dmat = jnp.where(lower, bcs - bcs_row + ii_row, NEG_INF)
    inter = bcs + m_prev
    m_row = jnp.maximum(jnp.max(dmat, axis=-1, keepdims=True), inter)
    s = _bdot_nt(q, k) * jnp.exp(dmat - m_row)
    w_inter = jnp.exp(inter - m_row)
    num = _bdot(s, v) + w_inter * _bdot(q, c_scr[...])
    den = jnp.sum(s, axis=-1, keepdims=True) + w_inter * jnp.sum(q.astype(F32) * n_scr[0:1, :], axis=-1, keepdims=True)
    h_ref[...] = num / jnp.maximum(jnp.abs(den), jnp.exp(-m_row))
    b_last = bcs[qn - 1:qn, :]
    gdec = b_last - bcs + ii
    m_new = jnp.maximum(b_last + m_prev, jnp.max(gdec, axis=0, keepdims=True))
    wk = jnp.exp(gdec - m_new)
    decay = jnp.exp(b_last + m_prev - m_new)
    kw = k.astype(F32) * wk
    c_scr[...] = decay * c_scr[...] + _bdot_tn(kw, v)
    n_scr[...] = decay * n_scr[...] + jnp.sum(kw, axis=0, keepdims=True)
    m_scr[...] = jnp.zeros(m_scr.shape, F32) + m_new


def _mlstm_out_kernel(hc_ref, xc_ref, og_ref, h_ref, ng_ref, sk_ref, wd_ref, lg_ref, lb_ref, o_ref):
    hc = hc_ref[...]
    parts = []
    for hd in range(ML_HEADS):
        x = hc[:, hd * ML_HEADDIM:(hd + 1) * ML_HEADDIM]
        mu = jnp.mean(x, axis=-1, keepdims=True)
        c = x - mu
        parts.append(c * lax.rsqrt(jnp.mean(c * c, axis=-1, keepdims=True) + LN_EPS))
    hn = jnp.concatenate(parts, axis=1) * ng_ref[...] + sk_ref[...] * xc_ref[...]
    out = hn * _sigmoid(og_ref[...])
    y = jnp.dot(out.astype(BF16), wd_ref[...], preferred_element_type=F32)
    o_ref[...] = _layer_norm_rows(ALPHA * h_ref[...] + y, lg_ref[...], lb_ref[...])


def mlstm_layer(h2, bsz, seq, w_in, conv_w, conv_b, w_q, w_k, w_v, w_gates, b_gates, norm_g, skip, w_down, g, b):
    t = bsz * seq
    wb = w_in.astype(BF16)
    xm, og = _linear(h2, [wb[:, :ML_INNER], wb[:, ML_INNER:]], [F32, F32])
    tq = min(256, seq)
    nt = seq // tq
    rows = lambda w: pl.BlockSpec((tq, w), lambda i, j: (i * nt + j, 0))
    gpad = 128 - 2 * ML_HEADS
    wg = jnp.pad(w_gates.astype(BF16), ((0, 0), (0, 0), (0, gpad)))
    bg = jnp.pad(b_gates.astype(F32).reshape(1, -1), ((0, 0), (0, gpad)))
    xc, q, k, v, gates = pl.pallas_call(
        _mlstm_proj_kernel,
        name="mlstm_proj",
        grid=(bsz, nt),
        in_specs=[rows(ML_INNER), _full((ML_CONV, ML_INNER)), _full((1, ML_INNER)),
                  _full(w_q.shape), _full(w_k.shape), _full(w_v.shape), _full(wg.shape), _full(bg.shape)],
        out_specs=[rows(ML_INNER), rows(ML_INNER), rows(ML_INNER), rows(ML_INNER), rows(128)],
        out_shape=[jax.ShapeDtypeStruct((t, ML_INNER), F32)] + [jax.ShapeDtypeStruct((t, ML_INNER), BF16)] * 3
        + [jax.ShapeDtypeStruct((t, 128), F32)],
        scratch_shapes=[pltpu.VMEM((8, ML_INNER), F32)],
        compiler_params=_cparams("parallel", "arbitrary"),
    )(xm, conv_w.astype(F32), conv_b.reshape(1, -1), w_q.astype(BF16), w_k.astype(BF16), w_v.astype(BF16), wg, bg)
    qn = min(ML_Q, seq)
    nc = seq // qn
    head = pl.BlockSpec((qn, ML_HEADDIM), lambda i, hd, j: (i * nc + j, hd))
    hc = pl.pallas_call(
        _mlstm_cell_kernel,
        name="mlstm_cell",
        grid=(bsz, ML_HEADS, nc),
        in_specs=[head, head, head, pl.BlockSpec((qn, 128), lambda i, hd, j: (i * nc + j, 0))],
        out_specs=head,
        out_shape=jax.ShapeDtypeStruct((t, ML_INNER), F32),
        scratch_shapes=[pltpu.VMEM((ML_HEADDIM, ML_HEADDIM), F32), pltpu.VMEM((8, ML_HEADDIM), F32),
                        pltpu.VMEM((8, 128), F32)],
        compiler_params=_cparams("parallel", "parallel", "arbitrary"),
    )(q, k, v, gates)
    tm = min(512, t)
    row = lambda w: pl.BlockSpec((tm, w), lambda i: (i, 0))
    return pl.pallas_call(
        _mlstm_out_kernel,
        name="mlstm_out",
        grid=(t // tm,),
        in_specs=[row(ML_INNER), row(ML_INNER), row(ML_INNER), row(D_MODEL), _full((1, ML_INNER)),
                  _full((1, ML_INNER)), _full((ML_INNER, D_MODEL)), _full((1, D_MODEL)), _full((1, D_MODEL))],
        out_specs=row(D_MODEL),
        out_shape=jax.ShapeDtypeStruct((t, D_MODEL), F32),
        compiler_params=_cparams("parallel"),
    )(hc, xc, og, h2, norm_g.reshape(1, -1), skip.reshape(1, -1), w_down.astype(BF16),
      g.reshape(1, -1), b.reshape(1, -1))


def _cross_attn_kernel(h_ref, k_ref, v_ref, wq_ref, wo_ref, g_ref, b_ref, o_ref):
    h = h_ref[...]
    q = jnp.dot(h.astype(BF16), wq_ref[...], preferred_element_type=F32)
    parts = []
    for hd in range(XA_HEADS):
        cols = slice(hd * XA_HEADDIM, (hd + 1) * XA_HEADDIM)
        s = _bdot_nt(q[:, cols], k_ref[:, cols]) * XA_HEADDIM ** -0.5
        p = jnp.exp(s - jnp.max(s, axis=-1, keepdims=True))
        p = p / jnp.sum(p, axis=-1, keepdims=True)
        parts.append(_bdot(p, v_ref[:, cols]))
    o = jnp.concatenate(parts, axis=1)
    y = jnp.dot(o.astype(BF16), wo_ref[...], preferred_element_type=F32)
    o_ref[...] = _layer_norm_rows(ALPHA * h + y, g_ref[...], b_ref[...])


def cross_attention_layer(h2, mem2, bsz, seq, w_q, w_kv, w_o, g, b, tq=512):
    t = bsz * seq
    mlen = mem2.shape[0] // bsz
    tq = min(tq, seq)
    nq = seq // tq
    (kv,) = _linear(mem2, [w_kv.astype(BF16)], [BF16], tm=256)
    return pl.pallas_call(
        _cross_attn_kernel,
        name="cross_attention",
        grid=(bsz, nq),
        in_specs=[pl.BlockSpec((tq, D_MODEL), lambda i, j: (i * nq + j, 0)),
                  pl.BlockSpec((mlen, D_MODEL), lambda i, j: (i, 0)),
                  pl.BlockSpec((mlen, D_MODEL), lambda i, j: (i, 1)),
                  _full((D_MODEL, D_MODEL)), _full((D_MODEL, D_MODEL)), _full((1, D_MODEL)), _full((1, D_MODEL))],
        out_specs=pl.BlockSpec((tq, D_MODEL), lambda i, j: (i * nq + j, 0)),
        out_shape=jax.ShapeDtypeStruct((t, D_MODEL), F32),
        compiler_params=_cparams("parallel", "parallel"),
    )(h2, kv, kv, w_q.astype(BF16), w_o.astype(BF16), g.reshape(1, -1), b.reshape(1, -1))


_PK_PIECES = (('a', 0, 0), ('a', 0, 8), ('a', 1, 0), ('a', 2, 0), ('a', 3, 0),
              ('b', 0, 8), ('b', 0, 0), ('b', 1, 0), ('b', 2, 0))
_PK_INVALID = 1 << 20


def _peer_piece_ids():
    ids = []
    seen = set()
    for kind, fixed, off in _PK_PIECES:
        for i in range(8):
            a, b = (fixed, off + i) if kind == 'a' else (off + i, fixed)
            ok = (a + 1) * (b + 1) <= PK_TOPK and (a, b) not in seen
            seen.add((a, b))
            ids.append(a * PK_TOPK + b if ok else _PK_INVALID)
    assert sum(i != _PK_INVALID for i in ids) == sum((a + 1) * (b + 1) <= PK_TOPK
                                                     for a in range(PK_TOPK) for b in range(PK_TOPK))
    return jnp.array(ids, jnp.int32).reshape(-1, 1)


def _peer_select_kernel(h_ref, wq_ref, keys_ref, flat_ref, eidx_ref, gate_ref, q_scr):
    tm = h_ref.shape[0]
    q_scr[...] = jnp.dot(h_ref[...].astype(BF16), wq_ref[...], preferred_element_type=F32).astype(BF16)
    key_id = lax.broadcasted_iota(jnp.int32, (PK_NKEYS, tm), 0)
    rank_id = lax.broadcasted_iota(jnp.int32, (PK_TOPK, tm), 0)
    flat = jnp.broadcast_to(flat_ref[...], (flat_ref.shape[0], tm))
    neg = jnp.float32(-jnp.inf)
    half = PK_QDIM // 2
    zero = jnp.zeros((PK_TOPK, tm), F32)

    def top1(a, s, vals, idxs):
        m = jnp.max(s, axis=0, keepdims=True)
        idx = jnp.min(jnp.where(s == m, key_id, PK_NKEYS), axis=0, keepdims=True)
        sel = rank_id == a
        return jnp.where(key_id == idx, neg, s), jnp.where(sel, m, vals), jnp.where(sel, idx.astype(F32), idxs)

    def candidates(vals0, idxs0, vals1, idxs1):
        rep = lambda x, r: jnp.broadcast_to(x[r:r + 1, :], (8, tm))
        v, c = [], []
        for kind, fixed, off in _PK_PIECES:
            if kind == 'a':
                v.append(rep(vals0, fixed) + vals1[off:off + 8, :])
                c.append(rep(idxs0, fixed) * PK_NKEYS + idxs1[off:off + 8, :])
            else:
                v.append(vals0[off:off + 8, :] + rep(vals1, fixed))
                c.append(idxs0[off:off + 8, :] * PK_NKEYS + rep(idxs1, fixed))
        cand = jnp.where(flat == _PK_INVALID, neg, jnp.concatenate(v, axis=0))
        return cand, jnp.concatenate(c, axis=0)

    def top2(kk, cand, cidx, cv, ce):
        m = jnp.max(cand, axis=0, keepdims=True)
        pos = jnp.min(jnp.where(cand == m, flat, _PK_INVALID), axis=0, keepdims=True)
        hit = flat == pos
        e = jnp.max(jnp.where(hit, cidx, -1.0), axis=0, keepdims=True)
        sel = rank_id == kk
        return jnp.where(hit, neg, cand), jnp.where(sel, m, cv), jnp.where(sel, e, ce)

    tops = None
    for stage in range(PK_HEADS + 1):
        first = stage < PK_HEADS
        second = stage > 0
        init = []
        if first:
            for j in range(2):
                cols = slice(stage * PK_QDIM + j * half, stage * PK_QDIM + (j + 1) * half)
                init += [_bdot_nt(keys_ref[j], q_scr[:, cols]), zero, zero]
        if second:
            cand0, cidx = candidates(*tops)
            init += [cand0, zero, zero]

        def body(a, st, first=first, second=second, cidx=cidx if second else None):
            st = list(st)
            out = []
            if first:
                out += top1(a, *st[0:3]) + top1(a, *st[3:6])
                st = st[6:]
            if second:
                out += top2(a, st[0], cidx, st[1], st[2])
            return tuple(out)

        res = lax.fori_loop(0, PK_TOPK, body, tuple(init))
        if second:
            cv, ce = res[-2], res[-1]
            p = jnp.exp(cv - cv[0:1, :])
            rows = slice((stage - 1) * PK_TOPK, stage * PK_TOPK)
            gate_ref[rows, :] = p / jnp.sum(p, axis=0, keepdims=True)
            eidx_ref[rows, :] = (ce * ROW_WORDS).astype(jnp.int32)
        if first:
            tops = (res[1], res[2], res[4], res[5])


def _peer_select(h2, wq_bf, keys_bf, tm=128):
    t = h2.shape[0]
    tm = min(tm, t)
    flat = _peer_piece_ids()
    return pl.pallas_call(
        _peer_select_kernel,
        name="peer_select",
        grid=(t // tm,),
        in_specs=[pl.BlockSpec((tm, D_MODEL), lambda i: (i, 0)), _full(wq_bf.shape), _full(keys_bf.shape),
                  _full(flat.shape)],
        out_specs=[pl.BlockSpec((PK_SLOTS, tm), lambda i: (0, i)),
                   pl.BlockSpec((PK_SLOTS, tm), lambda i: (0, i))],
        out_shape=[jax.ShapeDtypeStruct((PK_SLOTS, t), jnp.int32),
                   jax.ShapeDtypeStruct((PK_SLOTS, t), F32)],
        scratch_shapes=[pltpu.VMEM((tm, PK_HEADS * PK_QDIM), BF16)],
        compiler_params=_cparams("parallel"),
    )(h2, wq_bf, keys_bf, flat)


def _pack_rows(w):
    e, d = w.shape
    wb = lax.bitcast_convert_type(w.astype(BF16), jnp.uint16).astype(jnp.uint32)
    packed = (wb[:, d // 2:] << 16) | wb[:, :d // 2]
    return lax.bitcast_convert_type(packed, jnp.int32).reshape(e * ROW_WORDS, 128)


def _unpack_lo(w):
    return pltpu.bitcast(w << 16, F32)


def _unpack_hi(w):
    return pltpu.bitcast(w & jnp.int32(-65536), F32)


def _gather_rows(idx_ref, t, tab_ref, tile_ref):
    for r in range(PK_SLOTS):
        i = pl.multiple_of(idx_ref[t, r], ROW_WORDS)
        tile_ref[pl.ds(r * ROW_WORDS, ROW_WORDS), :] = tab_ref[pl.ds(i, ROW_WORDS), :]


def _two_token_pipeline(tb, gather, dense, init):
    gather(0, 0)

    def pair(i, carry):
        t0 = 2 * i
        gather(t0 + 1, 1)
        carry = dense(t0, 0, carry)
        gather(jnp.minimum(t0 + 2, tb - 1), 0)
        return dense(t0 + 1, 1, carry)

    return lax.fori_loop(0, tb // 2, pair, init)


def _peer_score_kernel(idx_ref, x_ref, gate_ref, tab_ref, act_ref, tile0_ref, tile1_ref):
    tb = x_ref.shape[0]
    tiles = (tile0_ref, tile1_ref)
    lane = lax.broadcasted_iota(jnp.int32, (PK_SLOTS, tb), 1)

    def dense(t, which, cols):
        acc = jnp.zeros((PK_SLOTS, 128), F32)
        xrow = x_ref[t]
        for s in range(ROW_WORDS):
            w = tiles[which][pl.ds(s, PK_SLOTS, stride=ROW_WORDS), :]
            acc = acc + _unpack_lo(w) * xrow[s:s + 1, :] + _unpack_hi(w) * xrow[ROW_WORDS + s:ROW_WORDS + s + 1, :]
        col = jnp.sum(acc, axis=-1, keepdims=True)
        return jnp.where(lane == t, col, cols)

    cols = _two_token_pipeline(tb, lambda t, which: _gather_rows(idx_ref, t, tab_ref, tiles[which]), dense,
                               jnp.zeros((PK_SLOTS, tb), F32))
    act_ref[...] = _gelu(cols) * gate_ref[...]


def _peer_score(eidx, x2, gate, tab, tb=128):
    t = x2.shape[0]
    tb = min(tb, t)
    slots = pl.BlockSpec((PK_SLOTS, tb), lambda i: (0, i))
    return pl.pallas_call(
        _peer_score_kernel,
        name="peer_score",
        grid=(t // tb,),
        in_specs=[pl.BlockSpec((tb, PK_SLOTS), lambda i: (i, 0), memory_space=pltpu.SMEM),
                  pl.BlockSpec((tb, 8, 128), lambda i: (i, 0, 0)), slots, _full(tab.shape)],
        out_specs=slots,
        out_shape=jax.ShapeDtypeStruct((PK_SLOTS, t), F32),
        scratch_shapes=[pltpu.VMEM((PK_SLOTS * ROW_WORDS, 128), jnp.int32)] * 2,
        compiler_params=_cparams("parallel"),
    )(eidx, x2.reshape(t, 8, 128), gate, tab)


def _peer_combine_kernel(idx_ref, act_ref, h_ref, g_ref, b_ref, tab_ref, out_ref, y_ref, tile0_ref, tile1_ref):
    tb = h_ref.shape[0]
    tiles = (tile0_ref, tile1_ref)
    lane = lax.broadcasted_iota(jnp.int32, (PK_SLOTS, tb), 1)

    def dense(t, which, carry):
        a = jnp.sum(jnp.where(lane == t, act_ref[...], 0.0), axis=-1, keepdims=True)
        lo, hi = [], []
        for s in range(ROW_WORDS):
            w = tiles[which][pl.ds(s, PK_SLOTS, stride=ROW_WORDS), :]
            lo.append(jnp.sum(a * _unpack_lo(w), axis=0, keepdims=True))
            hi.append(jnp.sum(a * _unpack_hi(w), axis=0, keepdims=True))
        y_ref[t] = jnp.concatenate(lo + hi, axis=0)
        return carry

    _two_token_pipeline(tb, lambda t, which: _gather_rows(idx_ref, t, tab_ref, tiles[which]), dense, 0)
    v = ALPHA * h_ref[...] + y_ref[...]
    mu = jnp.mean(v, axis=(1, 2), keepdims=True)
    c = v - mu
    var = jnp.mean(c * c, axis=(1, 2), keepdims=True)
    out_ref[...] = c * lax.rsqrt(var + LN_EPS) * g_ref[...] + b_ref[...]


def _peer_combine(eidx, act, h2, g, b, tab, tb=128):
    t = h2.shape[0]
    tb = min(tb, t)
    tok = pl.BlockSpec((tb, 8, 128), lambda i: (i, 0, 0))
    return pl.pallas_call(
        _peer_combine_kernel,
        name="peer_combine",
        grid=(t // tb,),
        in_specs=[pl.BlockSpec((tb, PK_SLOTS), lambda i: (i, 0), memory_space=pltpu.SMEM),
                  pl.BlockSpec((PK_SLOTS, tb), lambda i: (0, i)),
                  tok, _full((1, 8, 128)), _full((1, 8, 128)), _full(tab.shape)],
        out_specs=tok,
        out_shape=jax.ShapeDtypeStruct((t, 8, 128), F32),
        scratch_shapes=[pltpu.VMEM((tb, 8, 128), F32)] + [pltpu.VMEM((PK_SLOTS * ROW_WORDS, 128), jnp.int32)] * 2,
        compiler_params=_cparams("parallel"),
    )(eidx, act, h2.reshape(t, 8, 128), g.reshape(1, 8, 128), b.reshape(1, 8, 128), tab).reshape(t, D_MODEL)


def peer_layer(h2, w_query, sub_keys, u, v, g, b):
    eidx_t, gate = _peer_select(h2, w_query.astype(BF16), sub_keys.astype(BF16))
    eidx = eidx_t.T
    act = _peer_score(eidx, h2, gate, _pack_rows(u))
    return _peer_combine(eidx, act, h2, g, b, _pack_rows(v))


def kernel(x, mem, s5_lam_re, s5_lam_im, s5_log_dt, s5_b_re, s5_b_im, s5_c_re, s5_c_im, s5_d, s5_w_glu, s5_b_glu, da_w_qkv, da_lambda, da_subln_g, da_w_o, m2_w_in, m2_conv_w, m2_conv_b, m2_dt_bias, m2_a_log, m2_d, m2_norm_g, m2_w_out, ml_w_in, ml_conv_w, ml_conv_b, ml_w_q, ml_w_k, ml_w_v, ml_w_gates, ml_b_gates, ml_norm_g, ml_skip, ml_w_down, xa_w_q, xa_w_kv, xa_w_o, pk_w_query, pk_sub_keys, pk_u, pk_v, ln_g, ln_b):
    bsz, seq, dm = x.shape
    t = bsz * seq
    mem2 = mem.reshape(-1, dm)
    h = x.reshape(t, dm)
    for i in range(DEPTH):
        kind, j = i % 4, i // 4
        g0, b0 = ln_g[i, 0], ln_b[i, 0]
        if kind == 0:
            h = s5_layer(h.reshape(bsz, seq, dm), s5_lam_re[j], s5_lam_im[j], s5_log_dt[j], s5_b_re[j], s5_b_im[j],
                         s5_c_re[j], s5_c_im[j], s5_d[j], s5_w_glu[j], s5_b_glu[j], g0, b0)
        elif kind == 1:
            h = diff_attention_layer(h, bsz, seq, da_w_qkv[j], da_lambda[j], da_subln_g[j], da_w_o[j], i, g0, b0)
        elif kind == 2:
            h = mamba2_layer(h, bsz, seq, m2_w_in[j], m2_conv_w[j], m2_conv_b[j], m2_dt_bias[j], m2_a_log[j],
                             m2_d[j], m2_norm_g[j], m2_w_out[j], g0, b0)
        else:
            h = mlstm_layer(h, bsz, seq, ml_w_in[j], ml_conv_w[j], ml_conv_b[j], ml_w_q[j], ml_w_k[j], ml_w_v[j],
                            ml_w_gates[j], ml_b_gates[j], ml_norm_g[j], ml_skip[j], ml_w_down[j], g0, b0)
        h = cross_attention_layer(h, mem2, bsz, seq, xa_w_q[i], xa_w_kv[i], xa_w_o[i], ln_g[i, 1], ln_b[i, 1])
        h = peer_layer(h, pk_w_query[i], pk_sub_keys[i], pk_u[i], pk_v[i], ln_g[i, 2], ln_b[i, 2])
    return h.reshape(bsz, seq, dm)
```

```python
import functools
import math

import jax
import jax.numpy as jnp
from jax import lax
from jax.experimental import pallas as pl
from jax.experimental.pallas import tpu as pltpu

F32 = jnp.float32
BF16 = jnp.bfloat16
HIGHEST = lax.Precision.HIGHEST

D_MODEL = 1024
DEPTH = 4
ALPHA = (2 * DEPTH) ** 0.25
LN_EPS = 1e-5
NEG_INF = -1e30
CHUNK = 64

S5_GROUP = 16
S5_STATE = 64
S5_Q = 64

DA_HEAD = 64
DA_HEADS = 8
DA_TILE = 512
DA_ROWS = 512

M2_INNER = 2048
M2_HEADS = 32
M2_HEADDIM = 64
M2_GROUPS = 4
M2_STATE = 128
M2_CONV = 4
M2_Q = 256

ML_INNER = 2048
ML_HEADS = 4
ML_HEADDIM = 512
ML_CONV = 4
ML_Q = 256

XA_HEADS = 4
XA_HEADDIM = 256

PK_HEADS = 8
PK_NKEYS = 128
PK_QDIM = 256
PK_TOPK = 16
PK_SLOTS = PK_HEADS * PK_TOPK
ROW_WORDS = D_MODEL // 2 // 128
PK_GROUP = 4

VMEM_LIMIT = 56 * 1024 * 1024


def _cparams(*sem):
    return pltpu.CompilerParams(dimension_semantics=sem, vmem_limit_bytes=VMEM_LIMIT)


def _gelu(x):
    return 0.5 * x * (1.0 + jnp.tanh(math.sqrt(2.0 / math.pi) * (x + 0.044715 * x * x * x)))


def _sigmoid(x):
    return 1.0 / (1.0 + jnp.exp(-x))


def _silu(x):
    return x * _sigmoid(x)


def _softplus(x):
    return jnp.maximum(x, 0.0) + jnp.log(1.0 + jnp.exp(-jnp.abs(x)))


def _layer_norm_rows(v, g, b):
    mu = jnp.mean(v, axis=-1, keepdims=True)
    c = v - mu
    var = jnp.mean(c * c, axis=-1, keepdims=True)
    return c * lax.rsqrt(var + LN_EPS) * g + b


def _bdot(a, b):
    return jnp.dot(a.astype(BF16), b.astype(BF16), preferred_element_type=F32)


def _bdot_nt(a, b):
    return lax.dot_general(a.astype(BF16), b.astype(BF16), (((1,), (1,)), ((), ())), preferred_element_type=F32)


def _bdot_tn(a, b):
    return lax.dot_general(a.astype(BF16), b.astype(BF16), (((0,), (0,)), ((), ())), preferred_element_type=F32)


def _full(shape):
    return pl.BlockSpec(shape, lambda *_: (0,) * len(shape))


def _causal_conv(x, halo, w, b):
    q = x.shape[0]
    k = w.shape[0]
    ext = jnp.concatenate([halo, x], axis=0)
    out = x * w[k - 1:k, :] + b
    for d in range(1, k):
        out = out + ext[8 - d:8 - d + q, :] * w[k - 1 - d:k - d, :]
    return out


def _cumsum_rows(a, tri):
    return jnp.dot(tri, a, precision=HIGHEST, preferred_element_type=F32)


def _linear_kernel(*refs, n_out):
    a = refs[0][...].astype(BF16)
    for w_ref, o_ref in zip(refs[1:1 + n_out], refs[1 + n_out:]):
        o_ref[...] = jnp.dot(a, w_ref[...], preferred_element_type=F32).astype(o_ref.dtype)


def _linear(a, ws, out_dtypes, tm=512):
    m, k = a.shape
    tm = min(tm, m)
    return pl.pallas_call(
        functools.partial(_linear_kernel, n_out=len(ws)),
        name="linear",
        grid=(m // tm,),
        in_specs=[pl.BlockSpec((tm, k), lambda i: (i, 0))] + [_full(w.shape) for w in ws],
        out_specs=[pl.BlockSpec((tm, w.shape[1]), lambda i: (i, 0)) for w in ws],
        out_shape=[jax.ShapeDtypeStruct((m, w.shape[1]), dt) for w, dt in zip(ws, out_dtypes)],
        compiler_params=_cparams("parallel"),
    )(a, *ws)


def _linear_res_ln_kernel(a_ref, w_ref, h_ref, g_ref, b_ref, o_ref):
    y = jnp.dot(a_ref[...].astype(BF16), w_ref[...], preferred_element_type=F32)
    o_ref[...] = _layer_norm_rows(ALPHA * h_ref[...] + y, g_ref[...], b_ref[...])


def _linear_res_ln(a, w, h, g, b, tm=512):
    m, k = a.shape
    tm = min(tm, m)
    return pl.pallas_call(
        _linear_res_ln_kernel,
        name="linear_res_ln",
        grid=(m // tm,),
        in_specs=[pl.BlockSpec((tm, k), lambda i: (i, 0)), _full(w.shape),
                  pl.BlockSpec((tm, D_MODEL), lambda i: (i, 0)), _full((1, D_MODEL)), _full((1, D_MODEL))],
        out_specs=pl.BlockSpec((tm, D_MODEL), lambda i: (i, 0)),
        out_shape=jax.ShapeDtypeStruct((m, D_MODEL), F32),
        compiler_params=_cparams("parallel"),
    )(a, w, h, g.reshape(1, -1), b.reshape(1, -1))


def _s5_tables(lam_re, lam_im, log_dt, b_re, b_im, c_re, c_im):
    q = S5_Q
    g, p = lam_re.shape
    lam = lax.complex(lam_re.astype(F32), lam_im.astype(F32))
    dt = jnp.exp(log_dt.astype(F32))[:, None]
    lam_bar = jnp.exp(lam * dt)
    b_bar = ((lam_bar - 1.0) / lam)[..., None] * lax.complex(b_re.astype(F32), b_im.astype(F32))
    c_mat = lax.complex(c_re.astype(F32), c_im.astype(F32))
    pw = jnp.cumprod(jnp.concatenate([jnp.ones((1, g, p), lam_bar.dtype),
                                      jnp.broadcast_to(lam_bar, (q, g, p))], axis=0), axis=0)
    cp = c_mat[None] * pw[:q, :, None, :]
    taps = (jnp.einsum('tgop,gpi->gtoi', cp.real, b_bar.real, precision=HIGHEST)
            - jnp.einsum('tgop,gpi->gtoi', cp.imag, b_bar.imag, precision=HIGHEST))
    pos = jnp.arange(q)
    tau = pos[None, :] - pos[:, None]
    toep = jnp.where((tau >= 0)[None, :, :, None, None], taps[:, jnp.maximum(tau, 0)], 0.0)
    toep = toep.transpose(0, 1, 4, 2, 3).reshape(g, q * S5_GROUP, q * S5_GROUP)
    wb = pw[q - 1 - pos][:, :, :, None] * b_bar[None]
    wt = jnp.concatenate([wb.real, wb.imag], axis=2).transpose(1, 0, 3, 2).reshape(g, q * S5_GROUP, 2 * p)
    cv = c_mat[None] * pw[1:q + 1, :, None, :]
    vt = jnp.concatenate([cv.real, -cv.imag], axis=3).transpose(1, 3, 0, 2).reshape(g, 2 * p, q * S5_GROUP)
    aq = pw[q]
    a_re = jnp.concatenate([aq.real, aq.real], axis=1).reshape(g, 1, 2 * p)
    a_im = jnp.concatenate([-aq.imag, aq.imag], axis=1).reshape(g, 1, 2 * p)
    return toep.astype(BF16), wt.astype(BF16), vt.astype(BF16), a_re, a_im


def _s5_kernel(x_ref, tt_ref, wt_ref, vt_ref, ar_ref, ai_ref, y_ref, s_scr, h_scr, *, nb):
    x = x_ref[0]
    y_ref[0] = jnp.dot(x, tt_ref[0], preferred_element_type=F32)
    s_scr[...] = jnp.dot(x, wt_ref[0], preferred_element_type=F32)
    a_re = ar_ref[0]
    a_im = ai_ref[0]
    half = s_scr.shape[1] // 2

    def step(c, h):
        rows = pl.ds(pl.multiple_of(c * nb, nb), nb)
        h_scr[rows, :] = h
        return a_re * h + a_im * pltpu.roll(h, half, axis=1) + s_scr[rows, :]

    lax.fori_loop(0, x.shape[0] // nb, step, jnp.zeros((nb, s_scr.shape[1]), F32))
    y_ref[0] += jnp.dot(h_scr[...].astype(BF16), vt_ref[0], preferred_element_type=F32)


def _s5_scan(x3, lam_re, lam_im, log_dt, b_re, b_im, c_re, c_im):
    bsz, seq, dm = x3.shape
    g = dm // S5_GROUP
    q = S5_Q
    nc = seq // q
    m = nc * bsz
    toep, wt, vt, a_re, a_im = _s5_tables(lam_re, lam_im, log_dt, b_re, b_im, c_re, c_im)
    xg = x3.reshape(bsz, nc, q, g, S5_GROUP).transpose(3, 1, 0, 2, 4).reshape(g, m, q * S5_GROUP).astype(BF16)
    w = q * S5_GROUP
    y = pl.pallas_call(
        functools.partial(_s5_kernel, nb=bsz),
        name="s5_scan",
        grid=(g,),
        in_specs=[pl.BlockSpec((1, m, w), lambda i: (i, 0, 0)),
                  pl.BlockSpec((1, w, w), lambda i: (i, 0, 0)),
                  pl.BlockSpec((1, w, 2 * S5_STATE), lambda i: (i, 0, 0)),
                  pl.BlockSpec((1, 2 * S5_STATE, w), lambda i: (i, 0, 0)),
                  pl.BlockSpec((1, 1, 2 * S5_STATE), lambda i: (i, 0, 0)),
                  pl.BlockSpec((1, 1, 2 * S5_STATE), lambda i: (i, 0, 0))],
        out_specs=pl.BlockSpec((1, m, w), lambda i: (i, 0, 0)),
        out_shape=jax.ShapeDtypeStruct((g, m, w), F32),
        scratch_shapes=[pltpu.VMEM((m, 2 * S5_STATE), F32), pltpu.VMEM((m, 2 * S5_STATE), F32)],
        compiler_params=_cparams("parallel"),
    )(xg, toep, wt, vt, a_re, a_im)
    return y.reshape(g, nc, bsz, q, S5_GROUP).transpose(2, 1, 3, 0, 4).reshape(bsz, seq, dm)


def _s5_out_kernel(y_ref, x_ref, d_ref, w_ref, bias_ref, g_ref, b_ref, o_ref):
    x = x_ref[...]
    y = _gelu(y_ref[...] + d_ref[...] * x)
    z = jnp.dot(y.astype(BF16), w_ref[...], preferred_element_type=F32) + bias_ref[...]
    out = z[:, :D_MODEL] * _sigmoid(z[:, D_MODEL:])
    o_ref[...] = _layer_norm_rows(ALPHA * x + out, g_ref[...], b_ref[...])


def s5_layer(x3, lam_re, lam_im, log_dt, b_re, b_im, c_re, c_im, d_skip, w_glu, b_glu, g, b, tm=512):
    bsz, seq, dm = x3.shape
    t = bsz * seq
    tm = min(tm, t)
    y = _s5_scan(x3, lam_re, lam_im, log_dt, b_re, b_im, c_re, c_im).reshape(t, dm)
    row = pl.BlockSpec((tm, dm), lambda i: (i, 0))
    return pl.pallas_call(
        _s5_out_kernel,
        name="s5_out",
        grid=(t // tm,),
        in_specs=[row, row, _full((1, dm)), _full((dm, 2 * dm)), _full((1, 2 * dm)), _full((1, dm)), _full((1, dm))],
        out_specs=row,
        out_shape=jax.ShapeDtypeStruct((t, dm), F32),
        compiler_params=_cparams("parallel"),
    )(y, x3.reshape(t, dm), d_skip.reshape(1, -1), w_glu.astype(BF16), b_glu.reshape(1, -1),
      g.reshape(1, -1), b.reshape(1, -1))


def _diff_attn_kernel(qi_ref, kj_ref, sc_ref, q_ref, k_ref, v_ref, rel_ref, g_ref, o_ref, m_scr, l_scr, acc_scr):
    hh = pl.program_id(1)
    pair = pl.program_id(2)
    qi = qi_ref[pair]
    kj = kj_ref[pair]
    tq = q_ref.shape[0]
    slope = sc_ref[0, hh]

    @pl.when(kj == 0)
    def _():
        m_scr[...] = jnp.full(m_scr.shape, NEG_INF, F32)
        l_scr[...] = jnp.zeros(l_scr.shape, F32)
        acc_scr[...] = jnp.zeros(acc_scr.shape, F32)

    def accumulate(bias_fn, shift):
        k = k_ref[...]
        v = v_ref[...]
        rb = min(DA_ROWS, tq)
        lane = lax.broadcasted_iota(jnp.int32, (rb, q_ref.shape[1]), 1)
        old = [(m_scr[j], l_scr[j], acc_scr[j]) for j in range(2)]
        new = [([], [], []) for _ in range(2)]
        blocks = [slice(blk * rb, (blk + 1) * rb) for blk in range(tq // rb)]
        scores = {}
        for rows in blocks:
            q = q_ref[rows, :]
            for j in range(2):
                qj = jnp.where((lane // DA_HEAD) == j, q, jnp.zeros_like(q)) * jnp.asarray(DA_HEAD ** -0.5, q.dtype)
                scores[rows.start, j] = lax.dot_general(qj, k, (((1,), (1,)), ((), ())), preferred_element_type=F32)
        width = m_scr.shape[2]
        for rows in blocks:
            for j in range(2):
                s = bias_fn(scores[rows.start, j], rows)
                m_old = old[j][0][rows, :]
                m_new = jnp.maximum(m_old, jnp.broadcast_to(jnp.max(s, axis=-1, keepdims=True), (rb, width)) + shift)
                p = jnp.exp(s - jnp.concatenate([m_new - shift] * (s.shape[1] // width), axis=1))
                corr = jnp.exp(m_old - m_new)
                new[j][0].append(m_new)
                new[j][1].append(corr * old[j][1][rows, :]
                                 + jnp.broadcast_to(jnp.sum(p, axis=-1, keepdims=True), (rb, width)))
                new[j][2].append(corr * old[j][2][rows, :] + jnp.dot(p.astype(BF16), v, preferred_element_type=F32))
        for j in range(2):
            m_scr[j] = jnp.concatenate(new[j][0], axis=0)
            l_scr[j] = jnp.concatenate(new[j][1], axis=0)
            acc_scr[j] = jnp.concatenate(new[j][2], axis=0)

    @pl.when(kj < qi)
    def _():
        accumulate(lambda s, rows: s + rel_ref[rows, :] * (-slope), -slope * ((qi - kj) * tq).astype(F32))

    @pl.when(kj == qi)
    def _():
        def masked(s, rows):
            row = lax.broadcasted_iota(jnp.int32, s.shape, 0) + rows.start
            col = lax.broadcasted_iota(jnp.int32, s.shape, 1)
            allowed = (col // CHUNK) <= (row // CHUNK)
            return jnp.where(allowed, s + jnp.abs(rel_ref[rows, :]) * (-slope), NEG_INF)

        accumulate(masked, 0.0)
        lam_full = sc_ref[1, 0]
        out_scale = sc_ref[1, 1]
        o = acc_scr[0] / l_scr[0] - lam_full * (acc_scr[1] / l_scr[1])
        o = o * lax.rsqrt(jnp.mean(o * o, axis=-1, keepdims=True) + LN_EPS)
        o_ref[...] = (o * g_ref[...] * out_scale).astype(o_ref.dtype)


def _diff_attention(qkv, scalars, subln_g, bsz, seq):
    t = bsz * seq
    tq = min(DA_TILE, seq)
    nq = seq // tq
    n_h = DA_HEADS
    pairs = [(i, j) for i in range(nq) for j in range(i + 1)]
    qi_of = jnp.array([i for i, _ in pairs], jnp.int32)
    kj_of = jnp.array([j for _, j in pairs], jnp.int32)
    pos = jnp.arange(tq, dtype=F32)
    rel = pos[:, None] - pos[None, :]
    grid_spec = pltpu.PrefetchScalarGridSpec(
        num_scalar_prefetch=2,
        grid=(bsz, n_h, len(pairs)),
        in_specs=[pl.BlockSpec(memory_space=pltpu.SMEM),
                  pl.BlockSpec((tq, 2 * DA_HEAD), lambda b, h, p, qi, kj: (b * nq + qi[p], h)),
                  pl.BlockSpec((tq, 2 * DA_HEAD), lambda b, h, p, qi, kj: (b * nq + kj[p], n_h + h)),
                  pl.BlockSpec((tq, 2 * DA_HEAD), lambda b, h, p, qi, kj: (b * nq + kj[p], 2 * n_h + h)),
                  pl.BlockSpec((tq, tq), lambda b, h, p, qi, kj: (0, 0)),
                  pl.BlockSpec((1, 2 * DA_HEAD), lambda b, h, p, qi, kj: (0, 0))],
        out_specs=pl.BlockSpec((tq, 2 * DA_HEAD), lambda b, h, p, qi, kj: (b * nq + qi[p], h)),
        scratch_shapes=[pltpu.VMEM((2, tq, 2 * DA_HEAD), F32), pltpu.VMEM((2, tq, 2 * DA_HEAD), F32),
                        pltpu.VMEM((2, tq, 2 * DA_HEAD), F32)])
    return pl.pallas_call(
        _diff_attn_kernel,
        name="diff_attention",
        grid_spec=grid_spec,
        out_shape=jax.ShapeDtypeStruct((t, D_MODEL), BF16),
        compiler_params=_cparams("parallel", "parallel", "arbitrary"),
    )(qi_of, kj_of, scalars, qkv, qkv, qkv, rel, subln_g.reshape(1, -1))


def diff_attention_layer(h2, bsz, seq, w_qkv, lam, subln_g, w_o, layer_idx, g, b):
    lam_init = 0.8 - 0.6 * math.exp(-0.3 * layer_idx)
    lf = lam.astype(F32)
    lam_full = jnp.exp(jnp.sum(lf[0] * lf[1])) - jnp.exp(jnp.sum(lf[2] * lf[3])) + lam_init
    slopes = 2.0 ** (-8.0 * jnp.arange(1, DA_HEADS + 1, dtype=F32) / DA_HEADS)
    scalars = jnp.stack([slopes, jnp.zeros((DA_HEADS,), F32).at[0].set(lam_full).at[1].set(1.0 - lam_init)])
    (qkv,) = _linear(h2, [w_qkv.astype(BF16)], [BF16])
    o = _diff_attention(qkv, scalars, subln_g, bsz, seq)
    return _linear_res_ln(o, w_o.astype(BF16), h2, g, b)


def _mamba_kernel(z_ref, xbc_ref, dt_ref, h_ref, cw_ref, cb_ref, dtb_ref, a_ref, d_ref, ng_ref, wo_ref,
                  lg_ref, lb_ref, o_ref, state_scr, halo_scr, y_scr):
    q = z_ref.shape[0]
    n = M2_STATE

    @pl.when(pl.program_id(1) == 0)
    def _():
        state_scr[...] = jnp.zeros(state_scr.shape, F32)
        halo_scr[...] = jnp.zeros(halo_scr.shape, F32)

    xbc_raw = xbc_ref[...]
    xbc = _silu(_causal_conv(xbc_raw, halo_scr[...], cw_ref[...], cb_ref[...]))
    halo_scr[...] = xbc_raw[q - 8:, :]
    row = lax.broadcasted_iota(jnp.int32, (q, q), 0)
    col = lax.broadcasted_iota(jnp.int32, (q, q), 1)
    lower = row >= col
    tri = lower.astype(F32)
    dt = _softplus(dt_ref[...] + dtb_ref[...])
    a_cs = _cumsum_rows(dt * a_ref[...], tri)
    a_cs_t = a_cs.T
    a_last = a_cs[q - 1:q, :]
    e_cs = jnp.exp(a_cs)
    e_dec = jnp.exp(a_last - a_cs)
    e_last = jnp.exp(a_last)
    lane = lax.broadcasted_iota(jnp.int32, (q, 2 * M2_HEADDIM), 1)
    first = lane < M2_HEADDIM
    srow = lax.broadcasted_iota(jnp.int32, (2 * M2_HEADDIM, n), 0)
    heads_per_group = M2_HEADS // M2_GROUPS
    for grp in range(M2_GROUPS):
        bm = xbc[:, M2_INNER + grp * n:M2_INNER + (grp + 1) * n]
        cm = xbc[:, M2_INNER + M2_GROUPS * n + grp * n:M2_INNER + M2_GROUPS * n + (grp + 1) * n]
        cb = _bdot_nt(cm, bm)
        for pair in range(heads_per_group // 2):
            h0 = grp * heads_per_group + 2 * pair
            cols = slice(h0 * M2_HEADDIM, (h0 + 2) * M2_HEADDIM)
            xs = xbc[:, cols]
            dtp = jnp.where(first, dt[:, h0:h0 + 1], dt[:, h0 + 1:h0 + 2])
            xdt = xs * dtp
            y = jnp.zeros((q, 2 * M2_HEADDIM), F32)
            for s in range(2):
                hd = h0 + s
                seg = a_cs[:, hd:hd + 1] - a_cs_t[hd:hd + 1, :]
                lmat = jnp.exp(jnp.where(lower, seg, NEG_INF))
                part = jnp.where(first if s == 0 else jnp.logical_not(first), xdt, 0.0)
                y = y + _bdot(cb * lmat, part)
            state = state_scr[pl.ds(h0 * M2_HEADDIM, 2 * M2_HEADDIM), :]
            y_off = _bdot_nt(cm, state)
            y = y + y_off * jnp.where(first, e_cs[:, h0:h0 + 1], e_cs[:, h0 + 1:h0 + 2])
            dec = jnp.where(first, e_dec[:, h0:h0 + 1], e_dec[:, h0 + 1:h0 + 2])
            grow = jnp.where(srow < M2_HEADDIM, e_last[:, h0:h0 + 1], e_last[:, h0 + 1:h0 + 2])
            state_scr[pl.ds(h0 * M2_HEADDIM, 2 * M2_HEADDIM), :] = state * grow + _bdot_tn(xdt * dec, bm)
            y_scr[:, cols] = y + d_ref[:, cols] * xs
    gate = _silu(z_ref[...])
    y = y_scr[...] * gate
    gw = M2_INNER // M2_GROUPS
    parts = []
    for grp in range(M2_GROUPS):
        yg = y[:, grp * gw:(grp + 1) * gw]
        parts.append(yg * lax.rsqrt(jnp.mean(yg * yg, axis=-1, keepdims=True) + LN_EPS))
    yn = jnp.concatenate(parts, axis=1) * ng_ref[...]
    out = jnp.dot(yn.astype(BF16), wo_ref[...], preferred_element_type=F32)
    o_ref[...] = _layer_norm_rows(ALPHA * h_ref[...] + out, lg_ref[...], lb_ref[...])


def mamba2_layer(h2, bsz, seq, w_in, conv_w, conv_b, dt_bias, a_log, d_skip, norm_g, w_out, g, b):
    t = bsz * seq
    q = min(M2_Q, seq)
    nc = seq // q
    xbc_w = M2_INNER + 2 * M2_GROUPS * M2_STATE
    wb = w_in.astype(BF16)
    hpad = 128 - M2_HEADS
    pad_heads = lambda a: jnp.pad(a.astype(F32).reshape(1, -1), ((0, 0), (0, hpad)))
    z, xbc, dt = _linear(h2, [wb[:, :M2_INNER], wb[:, M2_INNER:M2_INNER + xbc_w],
                              jnp.pad(wb[:, M2_INNER + xbc_w:], ((0, 0), (0, hpad)))],
                         [F32, F32, F32], tm=256)
    d_cols = jnp.repeat(d_skip.astype(F32), M2_HEADDIM).reshape(1, -1)
    rowspec = lambda w: pl.BlockSpec((q, w), lambda i, j: (i * nc + j, 0))
    return pl.pallas_call(
        _mamba_kernel,
        name="mamba2",
        grid=(bsz, nc),
        in_specs=[rowspec(M2_INNER), rowspec(xbc_w), rowspec(128), rowspec(D_MODEL),
                  _full((M2_CONV, xbc_w)), _full((1, xbc_w)), _full((1, 128)), _full((1, 128)),
                  _full((1, M2_INNER)), _full((1, M2_INNER)), _full((M2_INNER, D_MODEL)),
                  _full((1, D_MODEL)), _full((1, D_MODEL))],
        out_specs=rowspec(D_MODEL),
        out_shape=jax.ShapeDtypeStruct((t, D_MODEL), F32),
        scratch_shapes=[pltpu.VMEM((M2_HEADS * M2_HEADDIM, M2_STATE), F32),
                        pltpu.VMEM((8, xbc_w), F32),
                        pltpu.VMEM((q, M2_INNER), F32)],
        compiler_params=_cparams("parallel", "arbitrary"),
    )(z, xbc, dt, h2, conv_w.astype(F32), conv_b.reshape(1, -1), pad_heads(dt_bias),
      pad_heads(-jnp.exp(a_log.astype(F32))), d_cols, norm_g.reshape(1, -1), w_out.astype(BF16),
      g.reshape(1, -1), b.reshape(1, -1))


def _mlstm_proj_kernel(xm_ref, cw_ref, cb_ref, wq_ref, wk_ref, wv_ref, wg_ref, bg_ref,
                       xc_ref, q_ref, k_ref, v_ref, gate_ref, halo_scr):
    tq = xm_ref.shape[0]

    @pl.when(pl.program_id(1) == 0)
    def _():
        halo_scr[...] = jnp.zeros(halo_scr.shape, F32)

    xm = xm_ref[...]
    xc = _silu(_causal_conv(xm, halo_scr[...], cw_ref[...], cb_ref[...]))
    halo_scr[...] = xm[tq - 8:, :]
    xc_ref[...] = xc
    gates = jnp.zeros((tq, bg_ref.shape[1]), F32) + bg_ref[...]
    for hd in range(ML_HEADS):
        cols = slice(hd * ML_HEADDIM, (hd + 1) * ML_HEADDIM)
        qh = _bdot(xc[:, cols], wq_ref[hd])
        kh = _bdot(xc[:, cols], wk_ref[hd]) * ML_HEADDIM ** -0.5
        vh = _bdot(xm[:, cols], wv_ref[hd])
        q_ref[:, cols] = qh.astype(BF16)
        k_ref[:, cols] = kh.astype(BF16)
        v_ref[:, cols] = vh.astype(BF16)
        gates = gates + _bdot(qh, wg_ref[0, cols, :]) + _bdot(kh, wg_ref[1, cols, :]) + _bdot(vh, wg_ref[2, cols, :])
    gate_ref[...] = gates


def _mlstm_cell_kernel(q_ref, k_ref, v_ref, gate_ref, h_ref, c_scr, n_scr, m_scr):
    hd = pl.program_id(1)
    qn = q_ref.shape[0]

    @pl.when(pl.program_id(2) == 0)
    def _():
        c_scr[...] = jnp.zeros(c_scr.shape, F32)
        n_scr[...] = jnp.zeros(n_scr.shape, F32)
        m_scr[...] = jnp.zeros(m_scr.shape, F32)

    q = q_ref[...]
    k = k_ref[...]
    v = v_ref[...]
    gates = gate_ref[...]
    glane = lax.broadcasted_iota(jnp.int32, gates.shape, 1)
    gsub = lax.broadcasted_iota(jnp.int32, (gates.shape[1], qn), 0)
    row = lax.broadcasted_iota(jnp.int32, (qn, qn), 0)
    col = lax.broadcasted_iota(jnp.int32, (qn, qn), 1)
    lower = row >= col
    logf = jnp.minimum(gates, 0.0) - jnp.log(1.0 + jnp.exp(-jnp.abs(gates)))
    cum = _cumsum_rows(logf, lower.astype(F32))
    ii = jnp.sum(jnp.where(glane == hd, gates, 0.0), axis=-1, keepdims=True)
    bcs = jnp.sum(jnp.where(glane == ML_HEADS + hd, cum, 0.0), axis=-1, keepdims=True)
    ii_row = jnp.sum(jnp.where(gsub == hd, gates.T, 0.0), axis=0, keepdims=True)
    bcs_row = jnp.sum(jnp.where(gsub == ML_HEADS + hd, cum.T, 0.0), axis=0, keepdims=True)
    m_prev = m_scr[0:1, 0:1]
    dmat = jnp.where(lower, bcs - bcs_row + ii_row, NEG_INF)
    inter = bcs + m_prev
    m_row = jnp.maximum(jnp.max(dmat, axis=-1, keepdims=True), inter)
    s = _bdot_nt(q, k) * jnp.exp(dmat - m_row)
    w_inter = jnp.exp(inter - m_row)
    num = _bdot(s, v) + w_inter * _bdot(q, c_scr[...])
    den = jnp.sum(s, axis=-1, keepdims=True) + w_inter * jnp.sum(q.astype(F32) * n_scr[0:1, :], axis=-1, keepdims=True)
    h_ref[...] = num / jnp.maximum(jnp.abs(den), jnp.exp(-m_row))
    b_last = bcs[qn - 1:qn, :]
    gdec = b_last - bcs + ii
    m_new = jnp.maximum(b_last + m_prev, jnp.max(gdec, axis=0, keepdims=True))
    wk = jnp.exp(gdec - m_new)
    decay = jnp.exp(b_last + m_prev - m_new)
    kw = k.astype(F32) * wk
    c_scr[...] = decay * c_scr[...] + _bdot_tn(kw, v)
    n_scr[...] = decay * n_scr[...] + jnp.sum(kw, axis=0, keepdims=True)
    m_scr[...] = jnp.zeros(m_scr.shape, F32) + m_new


def _mlstm_out_kernel(hc_ref, xc_ref, og_ref, h_ref, ng_ref, sk_ref, wd_ref, lg_ref, lb_ref, o_ref):
    hc = hc_ref[...]
    parts = []
    for hd in range(ML_HEADS):
        x = hc[:, hd * ML_HEADDIM:(hd + 1) * ML_HEADDIM]
        mu = jnp.mean(x, axis=-1, keepdims=True)
        c = x - mu
        parts.append(c * lax.rsqrt(jnp.mean(c * c, axis=-1, keepdims=True) + LN_EPS))
    hn = jnp.concatenate(parts, axis=1) * ng_ref[...] + sk_ref[...] * xc_ref[...]
    out = hn * _sigmoid(og_ref[...])
    y = jnp.dot(out.astype(BF16), wd_ref[...], preferred_element_type=F32)
    o_ref[...] = _layer_norm_rows(ALPHA * h_ref[...] + y, lg_ref[...], lb_ref[...])


def mlstm_layer(h2, bsz, seq, w_in, conv_w, conv_b, w_q, w_k, w_v, w_gates, b_gates, norm_g, skip, w_down, g, b):
    t = bsz * seq
    wb = w_in.astype(BF16)
    xm, og = _linear(h2, [wb[:, :ML_INNER], wb[:, ML_INNER:]], [F32, F32])
    tq = min(256, seq)
    nt = seq // tq
    rows = lambda w: pl.BlockSpec((tq, w), lambda i, j: (i * nt + j, 0))
    gpad = 128 - 2 * ML_HEADS
    wg = jnp.pad(w_gates.astype(BF16), ((0, 0), (0, 0), (0, gpad)))
    bg = jnp.pad(b_gates.astype(F32).reshape(1, -1), ((0, 0), (0, gpad)))
    xc, q, k, v, gates = pl.pallas_call(
        _mlstm_proj_kernel,
        name="mlstm_proj",
        grid=(bsz, nt),
        in_specs=[rows(ML_INNER), _full((ML_CONV, ML_INNER)), _full((1, ML_INNER)),
                  _full(w_q.shape), _full(w_k.shape), _full(w_v.shape), _full(wg.shape), _full(bg.shape)],
        out_specs=[rows(ML_INNER), rows(ML_INNER), rows(ML_INNER), rows(ML_INNER), rows(128)],
        out_shape=[jax.ShapeDtypeStruct((t, ML_INNER), F32)] + [jax.ShapeDtypeStruct((t, ML_INNER), BF16)] * 3
        + [jax.ShapeDtypeStruct((t, 128), F32)],
        scratch_shapes=[pltpu.VMEM((8, ML_INNER), F32)],
        compiler_params=_cparams("parallel", "arbitrary"),
    )(xm, conv_w.astype(F32), conv_b.reshape(1, -1), w_q.astype(BF16), w_k.astype(BF16), w_v.astype(BF16), wg, bg)
    qn = min(ML_Q, seq)
    nc = seq // qn
    head = pl.BlockSpec((qn, ML_HEADDIM), lambda i, hd, j: (i * nc + j, hd))
    hc = pl.pallas_call(
        _mlstm_cell_kernel,
        name="mlstm_cell",
        grid=(bsz, ML_HEADS, nc),
        in_specs=[head, head, head, pl.BlockSpec((qn, 128), lambda i, hd, j: (i * nc + j, 0))],
        out_specs=head,
        out_shape=jax.ShapeDtypeStruct((t, ML_INNER), F32),
        scratch_shapes=[pltpu.VMEM((ML_HEADDIM, ML_HEADDIM), F32), pltpu.VMEM((8, ML_HEADDIM), F32),
                        pltpu.VMEM((8, 128), F32)],
        compiler_params=_cparams("parallel", "parallel", "arbitrary"),
    )(q, k, v, gates)
    tm = min(512, t)
    row = lambda w: pl.BlockSpec((tm, w), lambda i: (i, 0))
    return pl.pallas_call(
        _mlstm_out_kernel,
        name="mlstm_out",
        grid=(t // tm,),
        in_specs=[row(ML_INNER), row(ML_INNER), row(ML_INNER), row(D_MODEL), _full((1, ML_INNER)),
                  _full((1, ML_INNER)), _full((ML_INNER, D_MODEL)), _full((1, D_MODEL)), _full((1, D_MODEL))],
        out_specs=row(D_MODEL),
        out_shape=jax.ShapeDtypeStruct((t, D_MODEL), F32),
        compiler_params=_cparams("parallel"),
    )(hc, xc, og, h2, norm_g.reshape(1, -1), skip.reshape(1, -1), w_down.astype(BF16),
      g.reshape(1, -1), b.reshape(1, -1))


def _cross_attn_kernel(h_ref, k_ref, v_ref, wq_ref, wo_ref, g_ref, b_ref, o_ref):
    h = h_ref[...]
    q = jnp.dot(h.astype(BF16), wq_ref[...], preferred_element_type=F32)
    parts = []
    for hd in range(XA_HEADS):
        cols = slice(hd * XA_HEADDIM, (hd + 1) * XA_HEADDIM)
        s = _bdot_nt(q[:, cols], k_ref[:, cols]) * XA_HEADDIM ** -0.5
        p = jnp.exp(s - jnp.max(s, axis=-1, keepdims=True))
        p = p / jnp.sum(p, axis=-1, keepdims=True)
        parts.append(_bdot(p, v_ref[:, cols]))
    o = jnp.concatenate(parts, axis=1)
    y = jnp.dot(o.astype(BF16), wo_ref[...], preferred_element_type=F32)
    o_ref[...] = _layer_norm_rows(ALPHA * h + y, g_ref[...], b_ref[...])


def cross_attention_layer(h2, mem2, bsz, seq, w_q, w_kv, w_o, g, b, tq=512):
    t = bsz * seq
    mlen = mem2.shape[0] // bsz
    tq = min(tq, seq)
    nq = seq // tq
    (kv,) = _linear(mem2, [w_kv.astype(BF16)], [BF16], tm=256)
    return pl.pallas_call(
        _cross_attn_kernel,
        name="cross_attention",
        grid=(bsz, nq),
        in_specs=[pl.BlockSpec((tq, D_MODEL), lambda i, j: (i * nq + j, 0)),
                  pl.BlockSpec((mlen, D_MODEL), lambda i, j: (i, 0)),
                  pl.BlockSpec((mlen, D_MODEL), lambda i, j: (i, 1)),
                  _full((D_MODEL, D_MODEL)), _full((D_MODEL, D_MODEL)), _full((1, D_MODEL)), _full((1, D_MODEL))],
        out_specs=pl.BlockSpec((tq, D_MODEL), lambda i, j: (i * nq + j, 0)),
        out_shape=jax.ShapeDtypeStruct((t, D_MODEL), F32),
        compiler_params=_cparams("parallel", "parallel"),
    )(h2, kv, kv, w_q.astype(BF16), w_o.astype(BF16), g.reshape(1, -1), b.reshape(1, -1))


_PK_PIECES = (('a', 0, 0), ('a', 0, 8), ('a', 1, 0), ('a', 2, 0), ('a', 3, 0),
              ('b', 0, 8), ('b', 0, 0), ('b', 1, 0), ('b', 2, 0))
_PK_INVALID = 1 << 20


def _peer_piece_ids():
    ids = []
    seen = set()
    for kind, fixed, off in _PK_PIECES:
        for i in range(8):
            a, b = (fixed, off + i) if kind == 'a' else (off + i, fixed)
            ok = (a + 1) * (b + 1) <= PK_TOPK and (a, b) not in seen
            seen.add((a, b))
            ids.append(a * PK_TOPK + b if ok else _PK_INVALID)
    assert sum(i != _PK_INVALID for i in ids) == sum((a + 1) * (b + 1) <= PK_TOPK
                                                     for a in range(PK_TOPK) for b in range(PK_TOPK))
    return jnp.array(ids, jnp.int32).reshape(-1, 1)


def _peer_select_kernel(h_ref, wq_ref, keys_ref, flat_ref, eidx_ref, gate_ref, q_scr, e_scr):
    tm = h_ref.shape[0]
    q_scr[...] = jnp.dot(h_ref[...].astype(BF16), wq_ref[...], preferred_element_type=F32).astype(BF16)
    key_id = lax.broadcasted_iota(jnp.int32, (PK_NKEYS, tm), 0)
    rank_id = lax.broadcasted_iota(jnp.int32, (PK_TOPK, tm), 0)
    flat = jnp.broadcast_to(flat_ref[...], (flat_ref.shape[0], tm))
    neg = jnp.float32(-jnp.inf)
    half = PK_QDIM // 2
    zero = jnp.zeros((PK_TOPK, tm), F32)

    def top1(a, s, vals, idxs):
        m = jnp.max(s, axis=0, keepdims=True)
        idx = jnp.min(jnp.where(s == m, key_id, PK_NKEYS), axis=0, keepdims=True)
        sel = rank_id == a
        return jnp.where(key_id == idx, neg, s), jnp.where(sel, m, vals), jnp.where(sel, idx.astype(F32), idxs)

    def candidates(vals0, idxs0, vals1, idxs1):
        rep = lambda x, r: jnp.broadcast_to(x[r:r + 1, :], (8, tm))
        v, c = [], []
        for kind, fixed, off in _PK_PIECES:
            if kind == 'a':
                v.append(rep(vals0, fixed) + vals1[off:off + 8, :])
                c.append(rep(idxs0, fixed) * PK_NKEYS + idxs1[off:off + 8, :])
            else:
                v.append(vals0[off:off + 8, :] + rep(vals1, fixed))
                c.append(idxs0[off:off + 8, :] * PK_NKEYS + rep(idxs1, fixed))
        cand = jnp.where(flat == _PK_INVALID, neg, jnp.concatenate(v, axis=0))
        return cand, jnp.concatenate(c, axis=0)

    def top2(kk, cand, cidx, cv, ce):
        m = jnp.max(cand, axis=0, keepdims=True)
        pos = jnp.min(jnp.where(cand == m, flat, _PK_INVALID), axis=0, keepdims=True)
        hit = flat == pos
        e = jnp.max(jnp.where(hit, cidx, -1.0), axis=0, keepdims=True)
        sel = rank_id == kk
        return jnp.where(hit, neg, cand), jnp.where(sel, m, cv), jnp.where(sel, e, ce)

    tops = None
    for stage in range(PK_HEADS + 1):
        first = stage < PK_HEADS
        second = stage > 0
        init = []
        if first:
            for j in range(2):
                cols = slice(stage * PK_QDIM + j * half, stage * PK_QDIM + (j + 1) * half)
                init += [_bdot_nt(keys_ref[j], q_scr[:, cols]), zero, zero]
        if second:
            cand0, cidx = candidates(*tops)
            init += [cand0, zero, zero]

        def body(a, st, first=first, second=second, cidx=cidx if second else None):
            st = list(st)
            out = []
            if first:
                out += top1(a, *st[0:3]) + top1(a, *st[3:6])
                st = st[6:]
            if second:
                out += top2(a, st[0], cidx, st[1], st[2])
            return tuple(out)

        res = lax.fori_loop(0, PK_TOPK, body, tuple(init))
        if second:
            cv, ce = res[-2], res[-1]
            p = jnp.exp(cv - cv[0:1, :])
            rows = slice((stage - 1) * PK_TOPK, stage * PK_TOPK)
            gate_ref[rows, :] = p / jnp.sum(p, axis=0, keepdims=True)
            e_scr[rows, :] = ce * ROW_WORDS
        if first:
            tops = (res[1], res[2], res[4], res[5])
    eidx_ref[...] = e_scr[...].T.astype(jnp.int32)


def _peer_select(h2, wq_bf, keys_bf, tm=128):
    t = h2.shape[0]
    tm = min(tm, t)
    flat = _peer_piece_ids()
    return pl.pallas_call(
        _peer_select_kernel,
        name="peer_select",
        grid=(t // tm,),
        in_specs=[pl.BlockSpec((tm, D_MODEL), lambda i: (i, 0)), _full(wq_bf.shape), _full(keys_bf.shape),
                  _full(flat.shape)],
        out_specs=[pl.BlockSpec((tm, PK_SLOTS), lambda i: (i, 0)),
                   pl.BlockSpec((PK_SLOTS, tm), lambda i: (0, i))],
        out_shape=[jax.ShapeDtypeStruct((t, PK_SLOTS), jnp.int32),
                   jax.ShapeDtypeStruct((PK_SLOTS, t), F32)],
        scratch_shapes=[pltpu.VMEM((tm, PK_HEADS * PK_QDIM), BF16), pltpu.VMEM((PK_SLOTS, tm), F32)],
        compiler_params=_cparams("parallel"),
    )(h2, wq_bf, keys_bf, flat)


def _pack_rows(w):
    e, d = w.shape
    wb = lax.bitcast_convert_type(w.astype(BF16), jnp.uint16).astype(jnp.uint32)
    packed = (wb[:, d // 2:] << 16) | wb[:, :d // 2]
    return lax.bitcast_convert_type(packed, jnp.int32).reshape(e * ROW_WORDS, 128)


def _unpack_lo(w):
    return pltpu.bitcast(w << 16, F32)


def _unpack_hi(w):
    return pltpu.bitcast(w & jnp.int32(-65536), F32)


def _gather_group(idx_ref, t0, tb, tab_ref, tiles_ref, first_tile):
    for u in range(PK_GROUP):
        row_ref = idx_ref.at[jnp.minimum(t0 + u, tb - 1)]
        for r in range(PK_SLOTS):
            i = pl.multiple_of(row_ref[r], ROW_WORDS)
            tiles_ref[first_tile + u, pl.ds(r * ROW_WORDS, ROW_WORDS), :] = tab_ref[pl.ds(i, ROW_WORDS), :]


def _group_pipeline(tb, idx_ref, tab_ref, tiles_ref, dense, init):
    _gather_group(idx_ref, 0, tb, tab_ref, tiles_ref, 0)

    def trip(i, carry):
        t0 = 2 * PK_GROUP * i
        for half in range(2):
            base = t0 + half * PK_GROUP
            _gather_group(idx_ref, base + PK_GROUP, tb, tab_ref, tiles_ref, (1 - half) * PK_GROUP)
            for u in range(PK_GROUP):
                carry = dense(base + u, half * PK_GROUP + u, carry)
        return carry

    return lax.fori_loop(0, tb // (2 * PK_GROUP), trip, init)


def _peer_score_kernel(idx_ref, x_ref, gate_ref, tab_ref, act_ref, tiles_ref):
    tb = x_ref.shape[0]
    lane = lax.broadcasted_iota(jnp.int32, (PK_SLOTS, tb), 1)

    def dense(t, tile, cols):
        acc = jnp.zeros((PK_SLOTS, 128), F32)
        xrow = x_ref[t]
        for s in range(ROW_WORDS):
            w = tiles_ref[tile, pl.ds(s, PK_SLOTS, stride=ROW_WORDS), :]
            acc = acc + _unpack_lo(w) * xrow[s:s + 1, :] + _unpack_hi(w) * xrow[ROW_WORDS + s:ROW_WORDS + s + 1, :]
        col = jnp.sum(acc, axis=-1, keepdims=True)
        return jnp.where(lane == t, col, cols)

    cols = _group_pipeline(tb, idx_ref, tab_ref, tiles_ref, dense, jnp.zeros((PK_SLOTS, tb), F32))
    act_ref[...] = _gelu(cols) * gate_ref[...]


def _peer_score(eidx, x2, gate, tab, tb=128):
    t = x2.shape[0]
    tb = min(tb, t)
    slots = pl.BlockSpec((PK_SLOTS, tb), lambda i: (0, i))
    return pl.pallas_call(
        _peer_score_kernel,
        name="peer_score",
        grid=(t // tb,),
        in_specs=[pl.BlockSpec((tb, PK_SLOTS), lambda i: (i, 0), memory_space=pltpu.SMEM),
                  pl.BlockSpec((tb, 8, 128), lambda i: (i, 0, 0)), slots, _full(tab.shape)],
        out_specs=slots,
        out_shape=jax.ShapeDtypeStruct((PK_SLOTS, t), F32),
        scratch_shapes=[pltpu.VMEM((2 * PK_GROUP, PK_SLOTS * ROW_WORDS, 128), jnp.int32)],
        compiler_params=_cparams("parallel"),
    )(eidx, x2.reshape(t, 8, 128), gate, tab)


def _peer_combine_kernel(idx_ref, act_ref, h_ref, g_ref, b_ref, tab_ref, out_ref, y_ref, tiles_ref):
    tb = h_ref.shape[0]
    lane = lax.broadcasted_iota(jnp.int32, (PK_SLOTS, tb), 1)

    def dense(t, tile, carry):
        a = jnp.sum(jnp.where(lane == t, act_ref[...], 0.0), axis=-1, keepdims=True)
        lo, hi = [], []
        for s in range(ROW_WORDS):
            w = tiles_ref[tile, pl.ds(s, PK_SLOTS, stride=ROW_WORDS), :]
            lo.append(jnp.sum(a * _unpack_lo(w), axis=0, keepdims=True))
            hi.append(jnp.sum(a * _unpack_hi(w), axis=0, keepdims=True))
        y_ref[t] = jnp.concatenate(lo + hi, axis=0)
        return carry

    _group_pipeline(tb, idx_ref, tab_ref, tiles_ref, dense, 0)
    v = ALPHA * h_ref[...] + y_ref[...]
    mu = jnp.mean(v, axis=(1, 2), keepdims=True)
    c = v - mu
    var = jnp.mean(c * c, axis=(1, 2), keepdims=True)
    out_ref[...] = c * lax.rsqrt(var + LN_EPS) * g_ref[...] + b_ref[...]


def _peer_combine(eidx, act, h2, g, b, tab, tb=128):
    t = h2.shape[0]
    tb = min(tb, t)
    tok = pl.BlockSpec((tb, 8, 128), lambda i: (i, 0, 0))
    return pl.pallas_call(
        _peer_combine_kernel,
        name="peer_combine",
        grid=(t // tb,),
        in_specs=[pl.BlockSpec((tb, PK_SLOTS), lambda i: (i, 0), memory_space=pltpu.SMEM),
                  pl.BlockSpec((PK_SLOTS, tb), lambda i: (0, i)),
                  tok, _full((1, 8, 128)), _full((1, 8, 128)), _full(tab.shape)],
        out_specs=tok,
        out_shape=jax.ShapeDtypeStruct((t, 8, 128), F32),
        scratch_shapes=[pltpu.VMEM((tb, 8, 128), F32),
                        pltpu.VMEM((2 * PK_GROUP, PK_SLOTS * ROW_WORDS, 128), jnp.int32)],
        compiler_params=_cparams("parallel"),
    )(eidx, act, h2.reshape(t, 8, 128), g.reshape(1, 8, 128), b.reshape(1, 8, 128), tab).reshape(t, D_MODEL)


def peer_layer(h2, w_query, sub_keys, u, v, g, b):
    eidx, gate = _peer_select(h2, w_query.astype(BF16), sub_keys.astype(BF16))
    act = _peer_score(eidx, h2, gate, _pack_rows(u))
    return _peer_combine(eidx, act, h2, g, b, _pack_rows(v))


def kernel(x, mem, s5_lam_re, s5_lam_im, s5_log_dt, s5_b_re, s5_b_im, s5_c_re, s5_c_im, s5_d, s5_w_glu, s5_b_glu, da_w_qkv, da_lambda, da_subln_g, da_w_o, m2_w_in, m2_conv_w, m2_conv_b, m2_dt_bias, m2_a_log, m2_d, m2_norm_g, m2_w_out, ml_w_in, ml_conv_w, ml_conv_b, ml_w_q, ml_w_k, ml_w_v, ml_w_gates, ml_b_gates, ml_norm_g, ml_skip, ml_w_down, xa_w_q, xa_w_kv, xa_w_o, pk_w_query, pk_sub_keys, pk_u, pk_v, ln_g, ln_b):
    bsz, seq, dm = x.shape
    t = bsz * seq
    mem2 = mem.reshape(-1, dm)
    h = x.reshape(t, dm)
    for i in range(DEPTH):
        kind, j = i % 4, i // 4
        g0, b0 = ln_g[i, 0], ln_b[i, 0]
        if kind == 0:
            h = s5_layer(h.reshape(bsz, seq, dm), s5_lam_re[j], s5_lam_im[j], s5_log_dt[j], s5_b_re[j], s5_b_im[j],
                         s5_c_re[j], s5_c_im[j], s5_d[j], s5_w_glu[j], s5_b_glu[j], g0, b0)
        elif kind == 1:
            h = diff_attention_layer(h, bsz, seq, da_w_qkv[j], da_lambda[j], da_subln_g[j], da_w_o[j], i, g0, b0)
        elif kind == 2:
            h = mamba2_layer(h, bsz, seq, m2_w_in[j], m2_conv_w[j], m2_conv_b[j], m2_dt_bias[j], m2_a_log[j],
                             m2_d[j], m2_norm_g[j], m2_w_out[j], g0, b0)
        else:
            h = mlstm_layer(h, bsz, seq, ml_w_in[j], ml_conv_w[j], ml_conv_b[j], ml_w_q[j], ml_w_k[j], ml_w_v[j],
                            ml_w_gates[j], ml_b_gates[j], ml_norm_g[j], ml_skip[j], ml_w_down[j], g0, b0)
        h = cross_attention_layer(h, mem2, bsz, seq, xa_w_q[i], xa_w_kv[i], xa_w_o[i], ln_g[i, 1], ln_b[i, 1])
        h = peer_layer(h, pk_w_query[i], pk_sub_keys[i], pk_u[i], pk_v[i], ln_g[i, 2], ln_b[i, 2])
    return h.reshape(bsz, seq, dm)
```

```python
import functools
import math

import jax
import jax.numpy as jnp
from jax import lax
from jax.experimental import pallas as pl
from jax.experimental.pallas import tpu as pltpu

F32 = jnp.float32
BF16 = jnp.bfloat16
HIGHEST = lax.Precision.HIGHEST

D_MODEL = 1024
DEPTH = 4
ALPHA = (2 * DEPTH) ** 0.25
LN_EPS = 1e-5
NEG_INF = -1e30
CHUNK = 64

S5_GROUP = 16
S5_STATE = 64
S5_Q = 64

DA_HEAD = 64
DA_HEADS = 8
DA_TILE = 1024
DA_ROWS = 1024

M2_INNER = 2048
M2_HEADS = 32
M2_HEADDIM = 64
M2_GROUPS = 4
M2_STATE = 128
M2_CONV = 4
M2_Q = 256

ML_INNER = 2048
ML_HEADS = 4
ML_HEADDIM = 512
ML_CONV = 4
ML_Q = 256

XA_HEADS = 4
XA_HEADDIM = 256

PK_HEADS = 8
PK_NKEYS = 128
PK_QDIM = 256
PK_TOPK = 16
PK_SLOTS = PK_HEADS * PK_TOPK
ROW_WORDS = D_MODEL // 2 // 128
PK_GROUP = 8

VMEM_LIMIT = 56 * 1024 * 1024


def _cparams(*sem):
    return pltpu.CompilerParams(dimension_semantics=sem, vmem_limit_bytes=VMEM_LIMIT)


def _gelu(x):
    return 0.5 * x * (1.0 + jnp.tanh(math.sqrt(2.0 / math.pi) * (x + 0.044715 * x * x * x)))


def _sigmoid(x):
    return 1.0 / (1.0 + jnp.exp(-x))


def _silu(x):
    return x * _sigmoid(x)


def _softplus(x):
    return jnp.maximum(x, 0.0) + jnp.log(1.0 + jnp.exp(-jnp.abs(x)))


def _layer_norm_rows(v, g, b):
    mu = jnp.mean(v, axis=-1, keepdims=True)
    c = v - mu
    var = jnp.mean(c * c, axis=-1, keepdims=True)
    return c * lax.rsqrt(var + LN_EPS) * g + b


def _bdot(a, b):
    return jnp.dot(a.astype(BF16), b.astype(BF16), preferred_element_type=F32)


def _bdot_nt(a, b):
    return lax.dot_general(a.astype(BF16), b.astype(BF16), (((1,), (1,)), ((), ())), preferred_element_type=F32)


def _bdot_tn(a, b):
    return lax.dot_general(a.astype(BF16), b.astype(BF16), (((0,), (0,)), ((), ())), preferred_element_type=F32)


def _full(shape):
    return pl.BlockSpec(shape, lambda *_: (0,) * len(shape))


def _causal_conv(x, halo, w, b):
    q = x.shape[0]
    k = w.shape[0]
    ext = jnp.concatenate([halo, x], axis=0)
    out = x * w[k - 1:k, :] + b
    for d in range(1, k):
        out = out + ext[8 - d:8 - d + q, :] * w[k - 1 - d:k - d, :]
    return out


def _cumsum_rows(a, tri):
    return jnp.dot(tri, a, precision=HIGHEST, preferred_element_type=F32)


def _linear_kernel(*refs, n_out):
    a = refs[0][...].astype(BF16)
    for w_ref, o_ref in zip(refs[1:1 + n_out], refs[1 + n_out:]):
        o_ref[...] = jnp.dot(a, w_ref[...], preferred_element_type=F32).astype(o_ref.dtype)


def _linear(a, ws, out_dtypes, tm=512):
    m, k = a.shape
    tm = min(tm, m)
    return pl.pallas_call(
        functools.partial(_linear_kernel, n_out=len(ws)),
        name="linear",
        grid=(m // tm,),
        in_specs=[pl.BlockSpec((tm, k), lambda i: (i, 0))] + [_full(w.shape) for w in ws],
        out_specs=[pl.BlockSpec((tm, w.shape[1]), lambda i: (i, 0)) for w in ws],
        out_shape=[jax.ShapeDtypeStruct((m, w.shape[1]), dt) for w, dt in zip(ws, out_dtypes)],
        compiler_params=_cparams("parallel"),
    )(a, *ws)


def _linear_res_ln_kernel(a_ref, w_ref, h_ref, g_ref, b_ref, o_ref):
    y = jnp.dot(a_ref[...].astype(BF16), w_ref[...], preferred_element_type=F32)
    o_ref[...] = _layer_norm_rows(ALPHA * h_ref[...] + y, g_ref[...], b_ref[...])


def _linear_res_ln(a, w, h, g, b, tm=512):
    m, k = a.shape
    tm = min(tm, m)
    return pl.pallas_call(
        _linear_res_ln_kernel,
        name="linear_res_ln",
        grid=(m // tm,),
        in_specs=[pl.BlockSpec((tm, k), lambda i: (i, 0)), _full(w.shape),
                  pl.BlockSpec((tm, D_MODEL), lambda i: (i, 0)), _full((1, D_MODEL)), _full((1, D_MODEL))],
        out_specs=pl.BlockSpec((tm, D_MODEL), lambda i: (i, 0)),
        out_shape=jax.ShapeDtypeStruct((m, D_MODEL), F32),
        compiler_params=_cparams("parallel"),
    )(a, w, h, g.reshape(1, -1), b.reshape(1, -1))


def _s5_tables(lam_re, lam_im, log_dt, b_re, b_im, c_re, c_im):
    q = S5_Q
    g, p = lam_re.shape
    lam = lax.complex(lam_re.astype(F32), lam_im.astype(F32))
    dt = jnp.exp(log_dt.astype(F32))[:, None]
    lam_bar = jnp.exp(lam * dt)
    b_bar = ((lam_bar - 1.0) / lam)[..., None] * lax.complex(b_re.astype(F32), b_im.astype(F32))
    c_mat = lax.complex(c_re.astype(F32), c_im.astype(F32))
    pw = jnp.cumprod(jnp.concatenate([jnp.ones((1, g, p), lam_bar.dtype),
                                      jnp.broadcast_to(lam_bar, (q, g, p))], axis=0), axis=0)
    cp = c_mat[None] * pw[:q, :, None, :]
    taps = (jnp.einsum('tgop,gpi->gtoi', cp.real, b_bar.real, precision=HIGHEST)
            - jnp.einsum('tgop,gpi->gtoi', cp.imag, b_bar.imag, precision=HIGHEST))
    pos = jnp.arange(q)
    tau = pos[None, :] - pos[:, None]
    toep = jnp.where((tau >= 0)[None, :, :, None, None], taps[:, jnp.maximum(tau, 0)], 0.0)
    toep = toep.transpose(0, 1, 4, 2, 3).reshape(g, q * S5_GROUP, q * S5_GROUP)
    wb = pw[q - 1 - pos][:, :, :, None] * b_bar[None]
    wt = jnp.concatenate([wb.real, wb.imag], axis=2).transpose(1, 0, 3, 2).reshape(g, q * S5_GROUP, 2 * p)
    cv = c_mat[None] * pw[1:q + 1, :, None, :]
    vt = jnp.concatenate([cv.real, -cv.imag], axis=3).transpose(1, 3, 0, 2).reshape(g, 2 * p, q * S5_GROUP)
    aq = pw[q]
    a_re = jnp.concatenate([aq.real, aq.real], axis=1).reshape(g, 1, 2 * p)
    a_im = jnp.concatenate([-aq.imag, aq.imag], axis=1).reshape(g, 1, 2 * p)
    return toep.astype(BF16), wt.astype(BF16), vt.astype(BF16), a_re, a_im


def _s5_kernel(x_ref, tt_ref, wt_ref, vt_ref, ar_ref, ai_ref, y_ref, s_scr, h_scr, *, nb):
    x = x_ref[0]
    y_ref[0] = jnp.dot(x, tt_ref[0], preferred_element_type=F32)
    s_scr[...] = jnp.dot(x, wt_ref[0], preferred_element_type=F32)
    a_re = ar_ref[0]
    a_im = ai_ref[0]
    half = s_scr.shape[1] // 2

    def step(c, h):
        rows = pl.ds(pl.multiple_of(c * nb, nb), nb)
        h_scr[rows, :] = h
        return a_re * h + a_im * pltpu.roll(h, half, axis=1) + s_scr[rows, :]

    lax.fori_loop(0, x.shape[0] // nb, step, jnp.zeros((nb, s_scr.shape[1]), F32))
    y_ref[0] += jnp.dot(h_scr[...].astype(BF16), vt_ref[0], preferred_element_type=F32)


def _s5_scan(x3, lam_re, lam_im, log_dt, b_re, b_im, c_re, c_im):
    bsz, seq, dm = x3.shape
    g = dm // S5_GROUP
    q = S5_Q
    nc = seq // q
    m = nc * bsz
    toep, wt, vt, a_re, a_im = _s5_tables(lam_re, lam_im, log_dt, b_re, b_im, c_re, c_im)
    xg = x3.reshape(bsz, nc, q, g, S5_GROUP).transpose(3, 1, 0, 2, 4).reshape(g, m, q * S5_GROUP).astype(BF16)
    w = q * S5_GROUP
    y = pl.pallas_call(
        functools.partial(_s5_kernel, nb=bsz),
        name="s5_scan",
        grid=(g,),
        in_specs=[pl.BlockSpec((1, m, w), lambda i: (i, 0, 0)),
                  pl.BlockSpec((1, w, w), lambda i: (i, 0, 0)),
                  pl.BlockSpec((1, w, 2 * S5_STATE), lambda i: (i, 0, 0)),
                  pl.BlockSpec((1, 2 * S5_STATE, w), lambda i: (i, 0, 0)),
                  pl.BlockSpec((1, 1, 2 * S5_STATE), lambda i: (i, 0, 0)),
                  pl.BlockSpec((1, 1, 2 * S5_STATE), lambda i: (i, 0, 0))],
        out_specs=pl.BlockSpec((1, m, w), lambda i: (i, 0, 0)),
        out_shape=jax.ShapeDtypeStruct((g, m, w), F32),
        scratch_shapes=[pltpu.VMEM((m, 2 * S5_STATE), F32), pltpu.VMEM((m, 2 * S5_STATE), F32)],
        compiler_params=_cparams("parallel"),
    )(xg, toep, wt, vt, a_re, a_im)
    return y.reshape(g, nc, bsz, q, S5_GROUP).transpose(2, 1, 3, 0, 4).reshape(bsz, seq, dm)


def _s5_out_kernel(y_ref, x_ref, d_ref, w_ref, bias_ref, g_ref, b_ref, o_ref):
    x = x_ref[...]
    y = _gelu(y_ref[...] + d_ref[...] * x)
    z = jnp.dot(y.astype(BF16), w_ref[...], preferred_element_type=F32) + bias_ref[...]
    out = z[:, :D_MODEL] * _sigmoid(z[:, D_MODEL:])
    o_ref[...] = _layer_norm_rows(ALPHA * x + out, g_ref[...], b_ref[...])


def s5_layer(x3, lam_re, lam_im, log_dt, b_re, b_im, c_re, c_im, d_skip, w_glu, b_glu, g, b, tm=512):
    bsz, seq, dm = x3.shape
    t = bsz * seq
    tm = min(tm, t)
    y = _s5_scan(x3, lam_re, lam_im, log_dt, b_re, b_im, c_re, c_im).reshape(t, dm)
    row = pl.BlockSpec((tm, dm), lambda i: (i, 0))
    return pl.pallas_call(
        _s5_out_kernel,
        name="s5_out",
        grid=(t // tm,),
        in_specs=[row, row, _full((1, dm)), _full((dm, 2 * dm)), _full((1, 2 * dm)), _full((1, dm)), _full((1, dm))],
        out_specs=row,
        out_shape=jax.ShapeDtypeStruct((t, dm), F32),
        compiler_params=_cparams("parallel"),
    )(y, x3.reshape(t, dm), d_skip.reshape(1, -1), w_glu.astype(BF16), b_glu.reshape(1, -1),
      g.reshape(1, -1), b.reshape(1, -1))


def _diff_attn_kernel(qi_ref, kj_ref, sc_ref, q_ref, k_ref, v_ref, rel_ref, g_ref, o_ref, m_scr, l_scr, acc_scr):
    hh = pl.program_id(1)
    pair = pl.program_id(2)
    qi = qi_ref[pair]
    kj = kj_ref[pair]
    tq = q_ref.shape[0]
    slope = sc_ref[0, hh]

    @pl.when(kj == 0)
    def _():
        m_scr[...] = jnp.full(m_scr.shape, NEG_INF, F32)
        l_scr[...] = jnp.zeros(l_scr.shape, F32)
        acc_scr[...] = jnp.zeros(acc_scr.shape, F32)

    def accumulate(bias_fn, shift):
        k = k_ref[...]
        v = v_ref[...]
        rb = min(DA_ROWS, tq)
        lane = lax.broadcasted_iota(jnp.int32, (rb, q_ref.shape[1]), 1)
        old = [(m_scr[j], l_scr[j], acc_scr[j]) for j in range(2)]
        new = [([], [], []) for _ in range(2)]
        blocks = [slice(blk * rb, (blk + 1) * rb) for blk in range(tq // rb)]
        scores = {}
        for rows in blocks:
            q = q_ref[rows, :]
            for j in range(2):
                qj = jnp.where((lane // DA_HEAD) == j, q, jnp.zeros_like(q)) * jnp.asarray(DA_HEAD ** -0.5, q.dtype)
                scores[rows.start, j] = lax.dot_general(qj, k, (((1,), (1,)), ((), ())), preferred_element_type=F32)
        width = m_scr.shape[2]
        for rows in blocks:
            for j in range(2):
                s = bias_fn(scores[rows.start, j], rows)
                m_old = old[j][0][rows, :]
                m_new = jnp.maximum(m_old, jnp.broadcast_to(jnp.max(s, axis=-1, keepdims=True), (rb, width)) + shift)
                p = jnp.exp(s - jnp.concatenate([m_new - shift] * (s.shape[1] // width), axis=1))
                corr = jnp.exp(m_old - m_new)
                new[j][0].append(m_new)
                new[j][1].append(corr * old[j][1][rows, :]
                                 + jnp.broadcast_to(jnp.sum(p, axis=-1, keepdims=True), (rb, width)))
                new[j][2].append(corr * old[j][2][rows, :] + jnp.dot(p.astype(BF16), v, preferred_element_type=F32))
        for j in range(2):
            m_scr[j] = jnp.concatenate(new[j][0], axis=0)
            l_scr[j] = jnp.concatenate(new[j][1], axis=0)
            acc_scr[j] = jnp.concatenate(new[j][2], axis=0)

    @pl.when(kj < qi)
    def _():
        accumulate(lambda s, rows: s + rel_ref[rows, :] * (-slope), -slope * ((qi - kj) * tq).astype(F32))

    @pl.when(kj == qi)
    def _():
        def masked(s, rows):
            row = lax.broadcasted_iota(jnp.int32, s.shape, 0) + rows.start
            col = lax.broadcasted_iota(jnp.int32, s.shape, 1)
            allowed = (col // CHUNK) <= (row // CHUNK)
            return jnp.where(allowed, s + jnp.abs(rel_ref[rows, :]) * (-slope), NEG_INF)

        accumulate(masked, 0.0)
        lam_full = sc_ref[1, 0]
        out_scale = sc_ref[1, 1]
        o = acc_scr[0] / l_scr[0] - lam_full * (acc_scr[1] / l_scr[1])
        o = o * lax.rsqrt(jnp.mean(o * o, axis=-1, keepdims=True) + LN_EPS)
        o_ref[...] = (o * g_ref[...] * out_scale).astype(o_ref.dtype)


def _diff_attention(qkv, scalars, subln_g, bsz, seq):
    t = bsz * seq
    tq = min(DA_TILE, seq)
    nq = seq // tq
    n_h = DA_HEADS
    pairs = [(i, j) for i in range(nq) for j in range(i + 1)]
    qi_of = jnp.array([i for i, _ in pairs], jnp.int32)
    kj_of = jnp.array([j for _, j in pairs], jnp.int32)
    pos = jnp.arange(tq, dtype=F32)
    rel = pos[:, None] - pos[None, :]
    grid_spec = pltpu.PrefetchScalarGridSpec(
        num_scalar_prefetch=2,
        grid=(bsz, n_h, len(pairs)),
        in_specs=[pl.BlockSpec(memory_space=pltpu.SMEM),
                  pl.BlockSpec((tq, 2 * DA_HEAD), lambda b, h, p, qi, kj: (b * nq + qi[p], h)),
                  pl.BlockSpec((tq, 2 * DA_HEAD), lambda b, h, p, qi, kj: (b * nq + kj[p], n_h + h)),
                  pl.BlockSpec((tq, 2 * DA_HEAD), lambda b, h, p, qi, kj: (b * nq + kj[p], 2 * n_h + h)),
                  pl.BlockSpec((tq, tq), lambda b, h, p, qi, kj: (0, 0)),
                  pl.BlockSpec((1, 2 * DA_HEAD), lambda b, h, p, qi, kj: (0, 0))],
        out_specs=pl.BlockSpec((tq, 2 * DA_HEAD), lambda b, h, p, qi, kj: (b * nq + qi[p], h)),
        scratch_shapes=[pltpu.VMEM((2, tq, 2 * DA_HEAD), F32), pltpu.VMEM((2, tq, 2 * DA_HEAD), F32),
                        pltpu.VMEM((2, tq, 2 * DA_HEAD), F32)])
    return pl.pallas_call(
        _diff_attn_kernel,
        name="diff_attention",
        grid_spec=grid_spec,
        out_shape=jax.ShapeDtypeStruct((t, D_MODEL), BF16),
        compiler_params=_cparams("parallel", "parallel", "arbitrary"),
    )(qi_of, kj_of, scalars, qkv, qkv, qkv, rel, subln_g.reshape(1, -1))


def diff_attention_layer(h2, bsz, seq, w_qkv, lam, subln_g, w_o, layer_idx, g, b):
    lam_init = 0.8 - 0.6 * math.exp(-0.3 * layer_idx)
    lf = lam.astype(F32)
    lam_full = jnp.exp(jnp.sum(lf[0] * lf[1])) - jnp.exp(jnp.sum(lf[2] * lf[3])) + lam_init
    slopes = 2.0 ** (-8.0 * jnp.arange(1, DA_HEADS + 1, dtype=F32) / DA_HEADS)
    scalars = jnp.stack([slopes, jnp.zeros((DA_HEADS,), F32).at[0].set(lam_full).at[1].set(1.0 - lam_init)])
    (qkv,) = _linear(h2, [w_qkv.astype(BF16)], [BF16])
    o = _diff_attention(qkv, scalars, subln_g, bsz, seq)
    return _linear_res_ln(o, w_o.astype(BF16), h2, g, b)


def _mamba_kernel(z_ref, xbc_ref, dt_ref, h_ref, cw_ref, cb_ref, dtb_ref, a_ref, d_ref, ng_ref, wo_ref,
                  lg_ref, lb_ref, o_ref, state_scr, halo_scr, y_scr):
    q = z_ref.shape[0]
    n = M2_STATE

    @pl.when(pl.program_id(1) == 0)
    def _():
        state_scr[...] = jnp.zeros(state_scr.shape, F32)
        halo_scr[...] = jnp.zeros(halo_scr.shape, F32)

    xbc_raw = xbc_ref[...]
    xbc = _silu(_causal_conv(xbc_raw, halo_scr[...], cw_ref[...], cb_ref[...]))
    halo_scr[...] = xbc_raw[q - 8:, :]
    row = lax.broadcasted_iota(jnp.int32, (q, q), 0)
    col = lax.broadcasted_iota(jnp.int32, (q, q), 1)
    lower = row >= col
    tri = lower.astype(F32)
    dt = _softplus(dt_ref[...] + dtb_ref[...])
    a_cs = _cumsum_rows(dt * a_ref[...], tri)
    a_cs_t = a_cs.T
    a_last = a_cs[q - 1:q, :]
    e_cs = jnp.exp(a_cs)
    e_dec = jnp.exp(a_last - a_cs)
    e_last = jnp.exp(a_last)
    lane = lax.broadcasted_iota(jnp.int32, (q, 2 * M2_HEADDIM), 1)
    first = lane < M2_HEADDIM
    srow = lax.broadcasted_iota(jnp.int32, (2 * M2_HEADDIM, n), 0)
    heads_per_group = M2_HEADS // M2_GROUPS
    for grp in range(M2_GROUPS):
        bm = xbc[:, M2_INNER + grp * n:M2_INNER + (grp + 1) * n]
        cm = xbc[:, M2_INNER + M2_GROUPS * n + grp * n:M2_INNER + M2_GROUPS * n + (grp + 1) * n]
        cb = _bdot_nt(cm, bm)
        for pair in range(heads_per_group // 2):
            h0 = grp * heads_per_group + 2 * pair
            cols = slice(h0 * M2_HEADDIM, (h0 + 2) * M2_HEADDIM)
            xs = xbc[:, cols]
            dtp = jnp.where(first, dt[:, h0:h0 + 1], dt[:, h0 + 1:h0 + 2])
            xdt = xs * dtp
            y = jnp.zeros((q, 2 * M2_HEADDIM), F32)
            for s in range(2):
                hd = h0 + s
                seg = a_cs[:, hd:hd + 1] - a_cs_t[hd:hd + 1, :]
                lmat = jnp.exp(jnp.where(lower, seg, NEG_INF))
                part = jnp.where(first if s == 0 else jnp.logical_not(first), xdt, 0.0)
                y = y + _bdot(cb * lmat, part)
            state = state_scr[pl.ds(h0 * M2_HEADDIM, 2 * M2_HEADDIM), :]
            y_off = _bdot_nt(cm, state)
            y = y + y_off * jnp.where(first, e_cs[:, h0:h0 + 1], e_cs[:, h0 + 1:h0 + 2])
            dec = jnp.where(first, e_dec[:, h0:h0 + 1], e_dec[:, h0 + 1:h0 + 2])
            grow = jnp.where(srow < M2_HEADDIM, e_last[:, h0:h0 + 1], e_last[:, h0 + 1:h0 + 2])
            state_scr[pl.ds(h0 * M2_HEADDIM, 2 * M2_HEADDIM), :] = state * grow + _bdot_tn(xdt * dec, bm)
            y_scr[:, cols] = y + d_ref[:, cols] * xs
    gate = _silu(z_ref[...])
    y = y_scr[...] * gate
    gw = M2_INNER // M2_GROUPS
    parts = []
    for grp in range(M2_GROUPS):
        yg = y[:, grp * gw:(grp + 1) * gw]
        parts.append(yg * lax.rsqrt(jnp.mean(yg * yg, axis=-1, keepdims=True) + LN_EPS))
    yn = jnp.concatenate(parts, axis=1) * ng_ref[...]
    out = jnp.dot(yn.astype(BF16), wo_ref[...], preferred_element_type=F32)
    o_ref[...] = _layer_norm_rows(ALPHA * h_ref[...] + out, lg_ref[...], lb_ref[...])


def mamba2_layer(h2, bsz, seq, w_in, conv_w, conv_b, dt_bias, a_log, d_skip, norm_g, w_out, g, b):
    t = bsz * seq
    q = min(M2_Q, seq)
    nc = seq // q
    xbc_w = M2_INNER + 2 * M2_GROUPS * M2_STATE
    wb = w_in.astype(BF16)
    hpad = 128 - M2_HEADS
    pad_heads = lambda a: jnp.pad(a.astype(F32).reshape(1, -1), ((0, 0), (0, hpad)))
    z, xbc, dt = _linear(h2, [wb[:, :M2_INNER], wb[:, M2_INNER:M2_INNER + xbc_w],
                              jnp.pad(wb[:, M2_INNER + xbc_w:], ((0, 0), (0, hpad)))],
                         [F32, F32, F32], tm=256)
    d_cols = jnp.repeat(d_skip.astype(F32), M2_HEADDIM).reshape(1, -1)
    rowspec = lambda w: pl.BlockSpec((q, w), lambda i, j: (i * nc + j, 0))
    return pl.pallas_call(
        _mamba_kernel,
        name="mamba2",
        grid=(bsz, nc),
        in_specs=[rowspec(M2_INNER), rowspec(xbc_w), rowspec(128), rowspec(D_MODEL),
                  _full((M2_CONV, xbc_w)), _full((1, xbc_w)), _full((1, 128)), _full((1, 128)),
                  _full((1, M2_INNER)), _full((1, M2_INNER)), _full((M2_INNER, D_MODEL)),
                  _full((1, D_MODEL)), _full((1, D_MODEL))],
        out_specs=rowspec(D_MODEL),
        out_shape=jax.ShapeDtypeStruct((t, D_MODEL), F32),
        scratch_shapes=[pltpu.VMEM((M2_HEADS * M2_HEADDIM, M2_STATE), F32),
                        pltpu.VMEM((8, xbc_w), F32),
                        pltpu.VMEM((q, M2_INNER), F32)],
        compiler_params=_cparams("parallel", "arbitrary"),
    )(z, xbc, dt, h2, conv_w.astype(F32), conv_b.reshape(1, -1), pad_heads(dt_bias),
      pad_heads(-jnp.exp(a_log.astype(F32))), d_cols, norm_g.reshape(1, -1), w_out.astype(BF16),
      g.reshape(1, -1), b.reshape(1, -1))


def _mlstm_proj_kernel(xm_ref, cw_ref, cb_ref, wq_ref, wk_ref, wv_ref, wg_ref, bg_ref,
                       xc_ref, q_ref, k_ref, v_ref, gate_ref, halo_scr):
    tq = xm_ref.shape[0]

    @pl.when(pl.program_id(1) == 0)
    def _():
        halo_scr[...] = jnp.zeros(halo_scr.shape, F32)

    xm = xm_ref[...]
    xc = _silu(_causal_conv(xm, halo_scr[...], cw_ref[...], cb_ref[...]))
    halo_scr[...] = xm[tq - 8:, :]
    xc_ref[...] = xc
    gates = jnp.zeros((tq, bg_ref.shape[1]), F32) + bg_ref[...]
    for hd in range(ML_HEADS):
        cols = slice(hd * ML_HEADDIM, (hd + 1) * ML_HEADDIM)
        qh = _bdot(xc[:, cols], wq_ref[hd])
        kh = _bdot(xc[:, cols], wk_ref[hd]) * ML_HEADDIM ** -0.5
        vh = _bdot(xm[:, cols], wv_ref[hd])
        q_ref[:, cols] = qh.astype(BF16)
        k_ref[:, cols] = kh.astype(BF16)
        v_ref[:, cols] = vh.astype(BF16)
        gates = gates + _bdot(qh, wg_ref[0, cols, :]) + _bdot(kh, wg_ref[1, cols, :]) + _bdot(vh, wg_ref[2, cols, :])
    gate_ref[...] = gates


def _mlstm_cell_kernel(q_ref, k_ref, v_ref, gate_ref, h_ref, c_scr, n_scr, m_scr):
    hd = pl.program_id(1)
    qn = q_ref.shape[0]

    @pl.when(pl.program_id(2) == 0)
    def _():
        c_scr[...] = jnp.zeros(c_scr.shape, F32)
        n_scr[...] = jnp.zeros(n_scr.shape, F32)
        m_scr[...] = jnp.zeros(m_scr.shape, F32)

    q = q_ref[...]
    k = k_ref[...]
    v = v_ref[...]
    gates = gate_ref[...]
    glane = lax.broadcasted_iota(jnp.int32, gates.shape, 1)
    gsub = lax.broadcasted_iota(jnp.int32, (gates.shape[1], qn), 0)
    row = lax.broadcasted_iota(jnp.int32, (qn, qn), 0)
    col = lax.broadcasted_iota(jnp.int32, (qn, qn), 1)
    lower = row >= col
    logf = jnp.minimum(gates, 0.0) - jnp.log(1.0 + jnp.exp(-jnp.abs(gates)))
    cum = _cumsum_rows(logf, lower.astype(F32))
    ii = jnp.sum(jnp.where(glane == hd, gates, 0.0), axis=-1, keepdims=True)
    bcs = jnp.sum(jnp.where(glane == ML_HEADS + hd, cum, 0.0), axis=-1, keepdims=True)
    ii_row = jnp.sum(jnp.where(gsub == hd, gates.T, 0.0), axis=0, keepdims=True)
    bcs_row = jnp.sum(jnp.where(gsub == ML_HEADS + hd, cum.T, 0.0), axis=0, keepdims=True)
    m_prev = m_scr[0:1, 0:1]
    dmat = jnp.where(lower, bcs - bcs_row + ii_row, NEG_INF)
    inter = bcs + m_prev
    m_row = jnp.maximum(jnp.max(dmat, axis=-1, keepdims=True), inter)
    s = _bdot_nt(q, k) * jnp.exp(dmat - m_row)
    w_inter = jnp.exp(inter - m_row)
    num = _bdot(s, v) + w_inter * _bdot(q, c_scr[...])
    den = jnp.sum(s, axis=-1, keepdims=True) + w_inter * jnp.sum(q.astype(F32) * n_scr[0:1, :], axis=-1, keepdims=True)
    h_ref[...] = num / jnp.maximum(jnp.abs(den), jnp.exp(-m_row))
    b_last = bcs[qn - 1:qn, :]
    gdec = b_last - bcs + ii
    m_new = jnp.maximum(b_last + m_prev, jnp.max(gdec, axis=0, keepdims=True))
    wk = jnp.exp(gdec - m_new)
    decay = jnp.exp(b_last + m_prev - m_new)
    kw = k.astype(F32) * wk
    c_scr[...] = decay * c_scr[...] + _bdot_tn(kw, v)
    n_scr[...] = decay * n_scr[...] + jnp.sum(kw, axis=0, keepdims=True)
    m_scr[...] = jnp.zeros(m_scr.shape, F32) + m_new


def _mlstm_out_kernel(hc_ref, xc_ref, og_ref, h_ref, ng_ref, sk_ref, wd_ref, lg_ref, lb_ref, o_ref):
    hc = hc_ref[...]
    parts = []
    for hd in range(ML_HEADS):
        x = hc[:, hd * ML_HEADDIM:(hd + 1) * ML_HEADDIM]
        mu = jnp.mean(x, axis=-1, keepdims=True)
        c = x - mu
        parts.append(c * lax.rsqrt(jnp.mean(c * c, axis=-1, keepdims=True) + LN_EPS))
    hn = jnp.concatenate(parts, axis=1) * ng_ref[...] + sk_ref[...] * xc_ref[...]
    out = hn * _sigmoid(og_ref[...])
    y = jnp.dot(out.astype(BF16), wd_ref[...], preferred_element_type=F32)
    o_ref[...] = _layer_norm_rows(ALPHA * h_ref[...] + y, lg_ref[...], lb_ref[...])


def mlstm_layer(h2, bsz, seq, w_in, conv_w, conv_b, w_q, w_k, w_v, w_gates, b_gates, norm_g, skip, w_down, g, b):
    t = bsz * seq
    wb = w_in.astype(BF16)
    xm, og = _linear(h2, [wb[:, :ML_INNER], wb[:, ML_INNER:]], [F32, F32])
    tq = min(256, seq)
    nt = seq // tq
    rows = lambda w: pl.BlockSpec((tq, w), lambda i, j: (i * nt + j, 0))
    gpad = 128 - 2 * ML_HEADS
    wg = jnp.pad(w_gates.astype(BF16), ((0, 0), (0, 0), (0, gpad)))
    bg = jnp.pad(b_gates.astype(F32).reshape(1, -1), ((0, 0), (0, gpad)))
    xc, q, k, v, gates = pl.pallas_call(
        _mlstm_proj_kernel,
        name="mlstm_proj",
        grid=(bsz, nt),
        in_specs=[rows(ML_INNER), _full((ML_CONV, ML_INNER)), _full((1, ML_INNER)),
                  _full(w_q.shape), _full(w_k.shape), _full(w_v.shape), _full(wg.shape), _full(bg.shape)],
        out_specs=[rows(ML_INNER), rows(ML_INNER), rows(ML_INNER), rows(ML_INNER), rows(128)],
        out_shape=[jax.ShapeDtypeStruct((t, ML_INNER), F32)] + [jax.ShapeDtypeStruct((t, ML_INNER), BF16)] * 3
        + [jax.ShapeDtypeStruct((t, 128), F32)],
        scratch_shapes=[pltpu.VMEM((8, ML_INNER), F32)],
        compiler_params=_cparams("parallel", "arbitrary"),
    )(xm, conv_w.astype(F32), conv_b.reshape(1, -1), w_q.astype(BF16), w_k.astype(BF16), w_v.astype(BF16), wg, bg)
    qn = min(ML_Q, seq)
    nc = seq // qn
    head = pl.BlockSpec((qn, ML_HEADDIM), lambda i, hd, j: (i * nc + j, hd))
    hc = pl.pallas_call(
        _mlstm_cell_kernel,
        name="mlstm_cell",
        grid=(bsz, ML_HEADS, nc),
        in_specs=[head, head, head, pl.BlockSpec((qn, 128), lambda i, hd, j: (i * nc + j, 0))],
        out_specs=head,
        out_shape=jax.ShapeDtypeStruct((t, ML_INNER), F32),
        scratch_shapes=[pltpu.VMEM((ML_HEADDIM, ML_HEADDIM), F32), pltpu.VMEM((8, ML_HEADDIM), F32),
                        pltpu.VMEM((8, 128), F32)],
        compiler_params=_cparams("parallel", "parallel", "arbitrary"),
    )(q, k, v, gates)
    tm = min(512, t)
    row = lambda w: pl.BlockSpec((tm, w), lambda i: (i, 0))
    return pl.pallas_call(
        _mlstm_out_kernel,
        name="mlstm_out",
        grid=(t // tm,),
        in_specs=[row(ML_INNER), row(ML_INNER), row(ML_INNER), row(D_MODEL), _full((1, ML_INNER)),
                  _full((1, ML_INNER)), _full((ML_INNER, D_MODEL)), _full((1, D_MODEL)), _full((1, D_MODEL))],
        out_specs=row(D_MODEL),
        out_shape=jax.ShapeDtypeStruct((t, D_MODEL), F32),
        compiler_params=_cparams("parallel"),
    )(hc, xc, og, h2, norm_g.reshape(1, -1), skip.reshape(1, -1), w_down.astype(BF16),
      g.reshape(1, -1), b.reshape(1, -1))


def _cross_attn_kernel(h_ref, k_ref, v_ref, wq_ref, wo_ref, g_ref, b_ref, o_ref):
    h = h_ref[...]
    q = jnp.dot(h.astype(BF16), wq_ref[...], preferred_element_type=F32)
    parts = []
    for hd in range(XA_HEADS):
        cols = slice(hd * XA_HEADDIM, (hd + 1) * XA_HEADDIM)
        s = _bdot_nt(q[:, cols], k_ref[:, cols]) * XA_HEADDIM ** -0.5
        p = jnp.exp(s - jnp.max(s, axis=-1, keepdims=True))
        p = p / jnp.sum(p, axis=-1, keepdims=True)
        parts.append(_bdot(p, v_ref[:, cols]))
    o = jnp.concatenate(parts, axis=1)
    y = jnp.dot(o.astype(BF16), wo_ref[...], preferred_element_type=F32)
    o_ref[...] = _layer_norm_rows(ALPHA * h + y, g_ref[...], b_ref[...])


def cross_attention_layer(h2, mem2, bsz, seq, w_q, w_kv, w_o, g, b, tq=512):
    t = bsz * seq
    mlen = mem2.shape[0] // bsz
    tq = min(tq, seq)
    nq = seq // tq
    (kv,) = _linear(mem2, [w_kv.astype(BF16)], [BF16], tm=256)
    return pl.pallas_call(
        _cross_attn_kernel,
        name="cross_attention",
        grid=(bsz, nq),
        in_specs=[pl.BlockSpec((tq, D_MODEL), lambda i, j: (i * nq + j, 0)),
                  pl.BlockSpec((mlen, D_MODEL), lambda i, j: (i, 0)),
                  pl.BlockSpec((mlen, D_MODEL), lambda i, j: (i, 1)),
                  _full((D_MODEL, D_MODEL)), _full((D_MODEL, D_MODEL)), _full((1, D_MODEL)), _full((1, D_MODEL))],
        out_specs=pl.BlockSpec((tq, D_MODEL), lambda i, j: (i * nq + j, 0)),
        out_shape=jax.ShapeDtypeStruct((t, D_MODEL), F32),
        compiler_params=_cparams("parallel", "parallel"),
    )(h2, kv, kv, w_q.astype(BF16), w_o.astype(BF16), g.reshape(1, -1), b.reshape(1, -1))


_PK_PIECES = (('a', 0, 0), ('a', 0, 8), ('a', 1, 0), ('a', 2, 0), ('a', 3, 0),
              ('b', 0, 8), ('b', 0, 0), ('b', 1, 0), ('b', 2, 0))
_PK_INVALID = 1 << 20


def _peer_piece_ids():
    ids = []
    seen = set()
    for kind, fixed, off in _PK_PIECES:
        for i in range(8):
            a, b = (fixed, off + i) if kind == 'a' else (off + i, fixed)
            ok = (a + 1) * (b + 1) <= PK_TOPK and (a, b) not in seen
            seen.add((a, b))
            ids.append(a * PK_TOPK + b if ok else _PK_INVALID)
    assert sum(i != _PK_INVALID for i in ids) == sum((a + 1) * (b + 1) <= PK_TOPK
                                                     for a in range(PK_TOPK) for b in range(PK_TOPK))
    return jnp.array(ids, jnp.int32).reshape(-1, 1)


def _peer_select_kernel(h_ref, wq_ref, keys_ref, flat_ref, eidx_ref, gate_ref, q_scr, e_scr):
    tm = h_ref.shape[0]
    q_scr[...] = jnp.dot(h_ref[...].astype(BF16), wq_ref[...], preferred_element_type=F32).astype(BF16)
    key_id = lax.broadcasted_iota(jnp.int32, (PK_NKEYS, tm), 0)
    rank_id = lax.broadcasted_iota(jnp.int32, (PK_TOPK, tm), 0)
    flat = jnp.broadcast_to(flat_ref[...], (flat_ref.shape[0], tm))
    neg = jnp.float32(-jnp.inf)
    half = PK_QDIM // 2
    zero = jnp.zeros((PK_TOPK, tm), F32)

    def top1(a, s, vals, idxs):
        m = jnp.max(s, axis=0, keepdims=True)
        idx = jnp.min(jnp.where(s == m, key_id, PK_NKEYS), axis=0, keepdims=True)
        sel = rank_id == a
        return jnp.where(key_id == idx, neg, s), jnp.where(sel, m, vals), jnp.where(sel, idx.astype(F32), idxs)

    def candidates(vals0, idxs0, vals1, idxs1):
        rep = lambda x, r: jnp.broadcast_to(x[r:r + 1, :], (8, tm))
        v, c = [], []
        for kind, fixed, off in _PK_PIECES:
            if kind == 'a':
                v.append(rep(vals0, fixed) + vals1[off:off + 8, :])
                c.append(rep(idxs0, fixed) * PK_NKEYS + idxs1[off:off + 8, :])
            else:
                v.append(vals0[off:off + 8, :] + rep(vals1, fixed))
                c.append(idxs0[off:off + 8, :] * PK_NKEYS + rep(idxs1, fixed))
        cand = jnp.where(flat == _PK_INVALID, neg, jnp.concatenate(v, axis=0))
        return cand, jnp.concatenate(c, axis=0)

    def top2(kk, cand, cidx, cv, ce):
        m = jnp.max(cand, axis=0, keepdims=True)
        pos = jnp.min(jnp.where(cand == m, flat, _PK_INVALID), axis=0, keepdims=True)
        hit = flat == pos
        e = jnp.max(jnp.where(hit, cidx, -1.0), axis=0, keepdims=True)
        sel = rank_id == kk
        return jnp.where(hit, neg, cand), jnp.where(sel, m, cv), jnp.where(sel, e, ce)

    tops = None
    for stage in range(PK_HEADS + 1):
        first = stage < PK_HEADS
        second = stage > 0
        init = []
        if first:
            for j in range(2):
                cols = slice(stage * PK_QDIM + j * half, stage * PK_QDIM + (j + 1) * half)
                init += [_bdot_nt(keys_ref[j], q_scr[:, cols]), zero, zero]
        if second:
            cand0, cidx = candidates(*tops)
            init += [cand0, zero, zero]

        def body(a, st, first=first, second=second, cidx=cidx if second else None):
            st = list(st)
            out = []
            if first:
                out += top1(a, *st[0:3]) + top1(a, *st[3:6])
                st = st[6:]
            if second:
                out += top2(a, st[0], cidx, st[1], st[2])
            return tuple(out)

        res = lax.fori_loop(0, PK_TOPK // 2, lambda i, st, body=body: body(2 * i + 1, body(2 * i, st)), tuple(init))
        if second:
            cv, ce = res[-2], res[-1]
            p = jnp.exp(cv - cv[0:1, :])
            rows = slice((stage - 1) * PK_TOPK, stage * PK_TOPK)
            gate_ref[rows, :] = p / jnp.sum(p, axis=0, keepdims=True)
            e_scr[rows, :] = ce * ROW_WORDS
        if first:
            tops = (res[1], res[2], res[4], res[5])
    eidx_ref[...] = e_scr[...].T.astype(jnp.int32)


def _peer_select(h2, wq_bf, keys_bf, tm=128):
    t = h2.shape[0]
    tm = min(tm, t)
    flat = _peer_piece_ids()
    return pl.pallas_call(
        _peer_select_kernel,
        name="peer_select",
        grid=(t // tm,),
        in_specs=[pl.BlockSpec((tm, D_MODEL), lambda i: (i, 0)), _full(wq_bf.shape), _full(keys_bf.shape),
                  _full(flat.shape)],
        out_specs=[pl.BlockSpec((tm, PK_SLOTS), lambda i: (i, 0)),
                   pl.BlockSpec((PK_SLOTS, tm), lambda i: (0, i))],
        out_shape=[jax.ShapeDtypeStruct((t, PK_SLOTS), jnp.int32),
                   jax.ShapeDtypeStruct((PK_SLOTS, t), F32)],
        scratch_shapes=[pltpu.VMEM((tm, PK_HEADS * PK_QDIM), BF16), pltpu.VMEM((PK_SLOTS, tm), F32)],
        compiler_params=_cparams("parallel"),
    )(h2, wq_bf, keys_bf, flat)


def _pack_rows(w):
    e, d = w.shape
    wb = lax.bitcast_convert_type(w.astype(BF16), jnp.uint16).astype(jnp.uint32)
    packed = (wb[:, d // 2:] << 16) | wb[:, :d // 2]
    return lax.bitcast_convert_type(packed, jnp.int32).reshape(e * ROW_WORDS, 128)


def _unpack_lo(w):
    return pltpu.bitcast(w << 16, F32)


def _unpack_hi(w):
    return pltpu.bitcast(w & jnp.int32(-65536), F32)


def _gather_group(idx_ref, t0, tb, tab_ref, tiles_ref, first_tile):
    for u in range(PK_GROUP):
        row_ref = idx_ref.at[jnp.minimum(t0 + u, tb - 1)]
        for r in range(PK_SLOTS):
            i = pl.multiple_of(row_ref[r], ROW_WORDS)
            tiles_ref[first_tile + u, pl.ds(r * ROW_WORDS, ROW_WORDS), :] = tab_ref[pl.ds(i, ROW_WORDS), :]


def _group_pipeline(tb, idx_ref, tab_ref, tiles_ref, dense_group):
    _gather_group(idx_ref, 0, tb, tab_ref, tiles_ref, 0)

    def trip(i, carry):
        t0 = 2 * PK_GROUP * i
        for half in range(2):
            base = pl.multiple_of(t0 + half * PK_GROUP, PK_GROUP)
            _gather_group(idx_ref, base + PK_GROUP, tb, tab_ref, tiles_ref, (1 - half) * PK_GROUP)
            dense_group(base, half * PK_GROUP)
        return carry

    lax.fori_loop(0, tb // (2 * PK_GROUP), trip, 0)


def _lane_block(base):
    return pl.multiple_of((base // 128) * 128, 128)


def _peer_score_kernel(idx_ref, x_ref, gate_ref, tab_ref, act_ref, tiles_ref):
    tb = x_ref.shape[0]
    lane = lax.broadcasted_iota(jnp.int32, (PK_SLOTS, 128), 1)
    half = D_MODEL // 2

    def dense_group(base, first_tile):
        xg = x_ref[pl.ds(base, PK_GROUP), :]
        blk = _lane_block(base)
        cols = act_ref[:, pl.ds(blk, 128)]
        for u in range(PK_GROUP):
            acc = jnp.zeros((PK_SLOTS, 128), F32)
            for s in range(ROW_WORDS):
                w = tiles_ref[first_tile + u, pl.ds(s, PK_SLOTS, stride=ROW_WORDS), :]
                acc = (acc + _unpack_lo(w) * xg[u:u + 1, s * 128:(s + 1) * 128]
                       + _unpack_hi(w) * xg[u:u + 1, half + s * 128:half + (s + 1) * 128])
            cols = jnp.where(lane == base + u - blk, jnp.sum(acc, axis=-1, keepdims=True), cols)
        act_ref[:, pl.ds(blk, 128)] = cols

    act_ref[...] = jnp.zeros(act_ref.shape, F32)
    _group_pipeline(tb, idx_ref, tab_ref, tiles_ref, dense_group)
    act_ref[...] = _gelu(act_ref[...]) * gate_ref[...]


def _peer_score(eidx, x2, gate, tab, tb=256):
    t = x2.shape[0]
    tb = min(tb, t)
    slots = pl.BlockSpec((PK_SLOTS, tb), lambda i: (0, i))
    return pl.pallas_call(
        _peer_score_kernel,
        name="peer_score",
        grid=(t // tb,),
        in_specs=[pl.BlockSpec((tb, PK_SLOTS), lambda i: (i, 0), memory_space=pltpu.SMEM),
                  pl.BlockSpec((tb, D_MODEL), lambda i: (i, 0)), slots, _full(tab.shape)],
        out_specs=slots,
        out_shape=jax.ShapeDtypeStruct((PK_SLOTS, t), F32),
        scratch_shapes=[pltpu.VMEM((2 * PK_GROUP, PK_SLOTS * ROW_WORDS, 128), jnp.int32)],
        compiler_params=_cparams("parallel"),
    )(eidx, x2, gate, tab)


def _peer_combine_kernel(idx_ref, act_ref, h_ref, g_ref, b_ref, tab_ref, out_ref, y_ref, tiles_ref):
    tb = h_ref.shape[0]
    lane = lax.broadcasted_iota(jnp.int32, (PK_SLOTS, 128), 1)

    def dense_group(base, first_tile):
        blk = _lane_block(base)
        act = act_ref[:, pl.ds(blk, 128)]
        rows = []
        for u in range(PK_GROUP):
            a = jnp.sum(jnp.where(lane == base + u - blk, act, 0.0), axis=-1, keepdims=True)
            lo, hi = [], []
            for s in range(ROW_WORDS):
                w = tiles_ref[first_tile + u, pl.ds(s, PK_SLOTS, stride=ROW_WORDS), :]
                lo.append(jnp.sum(a * _unpack_lo(w), axis=0, keepdims=True))
                hi.append(jnp.sum(a * _unpack_hi(w), axis=0, keepdims=True))
            rows.append(jnp.concatenate(lo + hi, axis=1))
        y_ref[pl.ds(base, PK_GROUP), :] = jnp.concatenate(rows, axis=0)

    _group_pipeline(tb, idx_ref, tab_ref, tiles_ref, dense_group)
    out_ref[...] = _layer_norm_rows(ALPHA * h_ref[...] + y_ref[...], g_ref[...], b_ref[...])


def _peer_combine(eidx, act, h2, g, b, tab, tb=256):
    t = h2.shape[0]
    tb = min(tb, t)
    tok = pl.BlockSpec((tb, D_MODEL), lambda i: (i, 0))
    return pl.pallas_call(
        _peer_combine_kernel,
        name="peer_combine",
        grid=(t // tb,),
        in_specs=[pl.BlockSpec((tb, PK_SLOTS), lambda i: (i, 0), memory_space=pltpu.SMEM),
                  pl.BlockSpec((PK_SLOTS, tb), lambda i: (0, i)),
                  tok, _full((1, D_MODEL)), _full((1, D_MODEL)), _full(tab.shape)],
        out_specs=tok,
        out_shape=jax.ShapeDtypeStruct((t, D_MODEL), F32),
        scratch_shapes=[pltpu.VMEM((tb, D_MODEL), F32),
                        pltpu.VMEM((2 * PK_GROUP, PK_SLOTS * ROW_WORDS, 128), jnp.int32)],
        compiler_params=_cparams("parallel"),
    )(eidx, act, h2, g.reshape(1, -1), b.reshape(1, -1), tab)


def peer_layer(h2, w_query, sub_keys, u, v, g, b):
    eidx, gate = _peer_select(h2, w_query.astype(BF16), sub_keys.astype(BF16))
    act = _peer_score(eidx, h2, gate, _pack_rows(u))
    return _peer_combine(eidx, act, h2, g, b, _pack_rows(v))


def kernel(x, mem, s5_lam_re, s5_lam_im, s5_log_dt, s5_b_re, s5_b_im, s5_c_re, s5_c_im, s5_d, s5_w_glu, s5_b_glu, da_w_qkv, da_lambda, da_subln_g, da_w_o, m2_w_in, m2_conv_w, m2_conv_b, m2_dt_bias, m2_a_log, m2_d, m2_norm_g, m2_w_out, ml_w_in, ml_conv_w, ml_conv_b, ml_w_q, ml_w_k, ml_w_v, ml_w_gates, ml_b_gates, ml_norm_g, ml_skip, ml_w_down, xa_w_q, xa_w_kv, xa_w_o, pk_w_query, pk_sub_keys, pk_u, pk_v, ln_g, ln_b):
    bsz, seq, dm = x.shape
    t = bsz * seq
    mem2 = mem.reshape(-1, dm)
    h = x.reshape(t, dm)
    for i in range(DEPTH):
        kind, j = i % 4, i // 4
        g0, b0 = ln_g[i, 0], ln_b[i, 0]
        if kind == 0:
            h = s5_layer(h.reshape(bsz, seq, dm), s5_lam_re[j], s5_lam_im[j], s5_log_dt[j], s5_b_re[j], s5_b_im[j],
                         s5_c_re[j], s5_c_im[j], s5_d[j], s5_w_glu[j], s5_b_glu[j], g0, b0)
        elif kind == 1:
            h = diff_attention_layer(h, bsz, seq, da_w_qkv[j], da_lambda[j], da_subln_g[j], da_w_o[j], i, g0, b0)
        elif kind == 2:
            h = mamba2_layer(h, bsz, seq, m2_w_in[j], m2_conv_w[j], m2_conv_b[j], m2_dt_bias[j], m2_a_log[j],
                             m2_d[j], m2_norm_g[j], m2_w_out[j], g0, b0)
        else:
            h = mlstm_layer(h, bsz, seq, ml_w_in[j], ml_conv_w[j], ml_conv_b[j], ml_w_q[j], ml_w_k[j], ml_w_v[j],
                            ml_w_gates[j], ml_b_gates[j], ml_norm_g[j], ml_skip[j], ml_w_down[j], g0, b0)
        h = cross_attention_layer(h, mem2, bsz, seq, xa_w_q[i], xa_w_kv[i], xa_w_o[i], ln_g[i, 1], ln_b[i, 1])
        h = peer_layer(h, pk_w_query[i], pk_sub_keys[i], pk_u[i], pk_v[i], ln_g[i, 2], ln_b[i, 2])
    return h.reshape(bsz, seq, dm)
```

```python
import functools
import math

import jax
import jax.numpy as jnp
from jax import lax
from jax.experimental import pallas as pl
from jax.experimental.pallas import tpu as pltpu

F32 = jnp.float32
BF16 = jnp.bfloat16
HIGHEST = lax.Precision.HIGHEST

D_MODEL = 1024
DEPTH = 4
ALPHA = (2 * DEPTH) ** 0.25
LN_EPS = 1e-5
NEG_INF = -1e30
CHUNK = 64

S5_GROUP = 16
S5_STATE = 64
S5_Q = 64

DA_HEAD = 64
DA_HEADS = 8
DA_TILE = 1024
DA_ROWS = 1024

M2_INNER = 2048
M2_HEADS = 32
M2_HEADDIM = 64
M2_GROUPS = 4
M2_STATE = 128
M2_CONV = 4
M2_Q = 256

ML_INNER = 2048
ML_HEADS = 4
ML_HEADDIM = 512
ML_CONV = 4
ML_Q = 256

XA_HEADS = 4
XA_HEADDIM = 256

PK_HEADS = 8
PK_NKEYS = 128
PK_QDIM = 256
PK_TOPK = 16
PK_SLOTS = PK_HEADS * PK_TOPK
ROW_WORDS = D_MODEL // 2 // 128
PK_GROUP = 8

VMEM_LIMIT = 56 * 1024 * 1024


def _cparams(*sem):
    return pltpu.CompilerParams(dimension_semantics=sem, vmem_limit_bytes=VMEM_LIMIT)


def _gelu(x):
    return 0.5 * x * (1.0 + jnp.tanh(math.sqrt(2.0 / math.pi) * (x + 0.044715 * x * x * x)))


def _sigmoid(x):
    return 1.0 / (1.0 + jnp.exp(-x))


def _silu(x):
    return x * _sigmoid(x)


def _softplus(x):
    return jnp.maximum(x, 0.0) + jnp.log(1.0 + jnp.exp(-jnp.abs(x)))


def _layer_norm_rows(v, g, b):
    mu = jnp.mean(v, axis=-1, keepdims=True)
    c = v - mu
    var = jnp.mean(c * c, axis=-1, keepdims=True)
    return c * lax.rsqrt(var + LN_EPS) * g + b


def _bdot(a, b):
    return jnp.dot(a.astype(BF16), b.astype(BF16), preferred_element_type=F32)


def _bdot_nt(a, b):
    return lax.dot_general(a.astype(BF16), b.astype(BF16), (((1,), (1,)), ((), ())), preferred_element_type=F32)


def _bdot_tn(a, b):
    return lax.dot_general(a.astype(BF16), b.astype(BF16), (((0,), (0,)), ((), ())), preferred_element_type=F32)


def _full(shape):
    return pl.BlockSpec(shape, lambda *_: (0,) * len(shape))


def _causal_conv(x, halo, w, b):
    q = x.shape[0]
    k = w.shape[0]
    ext = jnp.concatenate([halo, x], axis=0)
    out = x * w[k - 1:k, :] + b
    for d in range(1, k):
        out = out + ext[8 - d:8 - d + q, :] * w[k - 1 - d:k - d, :]
    return out


def _cumsum_rows(a, tri):
    return jnp.dot(tri, a, precision=HIGHEST, preferred_element_type=F32)


def _linear_kernel(*refs, n_out):
    a = refs[0][...].astype(BF16)
    for w_ref, o_ref in zip(refs[1:1 + n_out], refs[1 + n_out:]):
        o_ref[...] = jnp.dot(a, w_ref[...], preferred_element_type=F32).astype(o_ref.dtype)


def _linear(a, ws, out_dtypes, tm=512):
    m, k = a.shape
    tm = min(tm, m)
    return pl.pallas_call(
        functools.partial(_linear_kernel, n_out=len(ws)),
        name="linear",
        grid=(m // tm,),
        in_specs=[pl.BlockSpec((tm, k), lambda i: (i, 0))] + [_full(w.shape) for w in ws],
        out_specs=[pl.BlockSpec((tm, w.shape[1]), lambda i: (i, 0)) for w in ws],
        out_shape=[jax.ShapeDtypeStruct((m, w.shape[1]), dt) for w, dt in zip(ws, out_dtypes)],
        compiler_params=_cparams("parallel"),
    )(a, *ws)


def _linear_res_ln_kernel(a_ref, w_ref, h_ref, g_ref, b_ref, o_ref):
    y = jnp.dot(a_ref[...].astype(BF16), w_ref[...], preferred_element_type=F32)
    o_ref[...] = _layer_norm_rows(ALPHA * h_ref[...] + y, g_ref[...], b_ref[...])


def _linear_res_ln(a, w, h, g, b, tm=512):
    m, k = a.shape
    tm = min(tm, m)
    return pl.pallas_call(
        _linear_res_ln_kernel,
        name="linear_res_ln",
        grid=(m // tm,),
        in_specs=[pl.BlockSpec((tm, k), lambda i: (i, 0)), _full(w.shape),
                  pl.BlockSpec((tm, D_MODEL), lambda i: (i, 0)), _full((1, D_MODEL)), _full((1, D_MODEL))],
        out_specs=pl.BlockSpec((tm, D_MODEL), lambda i: (i, 0)),
        out_shape=jax.ShapeDtypeStruct((m, D_MODEL), F32),
        compiler_params=_cparams("parallel"),
    )(a, w, h, g.reshape(1, -1), b.reshape(1, -1))


def _s5_tables(lam_re, lam_im, log_dt, b_re, b_im, c_re, c_im):
    q = S5_Q
    g, p = lam_re.shape
    lam = lax.complex(lam_re.astype(F32), lam_im.astype(F32))
    dt = jnp.exp(log_dt.astype(F32))[:, None]
    lam_bar = jnp.exp(lam * dt)
    b_bar = ((lam_bar - 1.0) / lam)[..., None] * lax.complex(b_re.astype(F32), b_im.astype(F32))
    c_mat = lax.complex(c_re.astype(F32), c_im.astype(F32))
    pw = jnp.cumprod(jnp.concatenate([jnp.ones((1, g, p), lam_bar.dtype),
                                      jnp.broadcast_to(lam_bar, (q, g, p))], axis=0), axis=0)
    cp = c_mat[None] * pw[:q, :, None, :]
    taps = (jnp.einsum('tgop,gpi->gtoi', cp.real, b_bar.real, precision=HIGHEST)
            - jnp.einsum('tgop,gpi->gtoi', cp.imag, b_bar.imag, precision=HIGHEST))
    pos = jnp.arange(q)
    tau = pos[None, :] - pos[:, None]
    toep = jnp.where((tau >= 0)[None, :, :, None, None], taps[:, jnp.maximum(tau, 0)], 0.0)
    toep = toep.transpose(0, 1, 4, 2, 3).reshape(g, q * S5_GROUP, q * S5_GROUP)
    wb = pw[q - 1 - pos][:, :, :, None] * b_bar[None]
    wt = jnp.concatenate([wb.real, wb.imag], axis=2).transpose(1, 0, 3, 2).reshape(g, q * S5_GROUP, 2 * p)
    cv = c_mat[None] * pw[1:q + 1, :, None, :]
    vt = jnp.concatenate([cv.real, -cv.imag], axis=3).transpose(1, 3, 0, 2).reshape(g, 2 * p, q * S5_GROUP)
    aq = pw[q]
    a_re = jnp.concatenate([aq.real, aq.real], axis=1).reshape(g, 1, 2 * p)
    a_im = jnp.concatenate([-aq.imag, aq.imag], axis=1).reshape(g, 1, 2 * p)
    return toep.astype(BF16), wt.astype(BF16), vt.astype(BF16), a_re, a_im


def _s5_kernel(x_ref, tt_ref, wt_ref, vt_ref, ar_ref, ai_ref, y_ref, s_scr, h_scr, *, nb):
    x = x_ref[0]
    y_ref[0] = jnp.dot(x, tt_ref[0], preferred_element_type=F32)
    s_scr[...] = jnp.dot(x, wt_ref[0], preferred_element_type=F32)
    a_re = ar_ref[0]
    a_im = ai_ref[0]
    half = s_scr.shape[1] // 2

    def step(c, h):
        rows = pl.ds(pl.multiple_of(c * nb, nb), nb)
        h_scr[rows, :] = h
        return a_re * h + a_im * pltpu.roll(h, half, axis=1) + s_scr[rows, :]

    lax.fori_loop(0, x.shape[0] // nb, step, jnp.zeros((nb, s_scr.shape[1]), F32))
    y_ref[0] += jnp.dot(h_scr[...].astype(BF16), vt_ref[0], preferred_element_type=F32)


def _s5_scan(x3, lam_re, lam_im, log_dt, b_re, b_im, c_re, c_im):
    bsz, seq, dm = x3.shape
    g = dm // S5_GROUP
    q = S5_Q
    nc = seq // q
    m = nc * bsz
    toep, wt, vt, a_re, a_im = _s5_tables(lam_re, lam_im, log_dt, b_re, b_im, c_re, c_im)
    xg = x3.reshape(bsz, nc, q, g, S5_GROUP).transpose(3, 1, 0, 2, 4).reshape(g, m, q * S5_GROUP).astype(BF16)
    w = q * S5_GROUP
    y = pl.pallas_call(
        functools.partial(_s5_kernel, nb=bsz),
        name="s5_scan",
        grid=(g,),
        in_specs=[pl.BlockSpec((1, m, w), lambda i: (i, 0, 0)),
                  pl.BlockSpec((1, w, w), lambda i: (i, 0, 0)),
                  pl.BlockSpec((1, w, 2 * S5_STATE), lambda i: (i, 0, 0)),
                  pl.BlockSpec((1, 2 * S5_STATE, w), lambda i: (i, 0, 0)),
                  pl.BlockSpec((1, 1, 2 * S5_STATE), lambda i: (i, 0, 0)),
                  pl.BlockSpec((1, 1, 2 * S5_STATE), lambda i: (i, 0, 0))],
        out_specs=pl.BlockSpec((1, m, w), lambda i: (i, 0, 0)),
        out_shape=jax.ShapeDtypeStruct((g, m, w), F32),
        scratch_shapes=[pltpu.VMEM((m, 2 * S5_STATE), F32), pltpu.VMEM((m, 2 * S5_STATE), F32)],
        compiler_params=_cparams("parallel"),
    )(xg, toep, wt, vt, a_re, a_im)
    return y.reshape(g, nc, bsz, q, S5_GROUP).transpose(2, 1, 3, 0, 4).reshape(bsz, seq, dm)


def _s5_out_kernel(y_ref, x_ref, d_ref, w_ref, bias_ref, g_ref, b_ref, o_ref):
    x = x_ref[...]
    y = _gelu(y_ref[...] + d_ref[...] * x)
    z = jnp.dot(y.astype(BF16), w_ref[...], preferred_element_type=F32) + bias_ref[...]
    out = z[:, :D_MODEL] * _sigmoid(z[:, D_MODEL:])
    o_ref[...] = _layer_norm_rows(ALPHA * x + out, g_ref[...], b_ref[...])


def s5_layer(x3, lam_re, lam_im, log_dt, b_re, b_im, c_re, c_im, d_skip, w_glu, b_glu, g, b, tm=512):
    bsz, seq, dm = x3.shape
    t = bsz * seq
    tm = min(tm, t)
    y = _s5_scan(x3, lam_re, lam_im, log_dt, b_re, b_im, c_re, c_im).reshape(t, dm)
    row = pl.BlockSpec((tm, dm), lambda i: (i, 0))
    return pl.pallas_call(
        _s5_out_kernel,
        name="s5_out",
        grid=(t // tm,),
        in_specs=[row, row, _full((1, dm)), _full((dm, 2 * dm)), _full((1, 2 * dm)), _full((1, dm)), _full((1, dm))],
        out_specs=row,
        out_shape=jax.ShapeDtypeStruct((t, dm), F32),
        compiler_params=_cparams("parallel"),
    )(y, x3.reshape(t, dm), d_skip.reshape(1, -1), w_glu.astype(BF16), b_glu.reshape(1, -1),
      g.reshape(1, -1), b.reshape(1, -1))


def _diff_attn_kernel(qi_ref, kj_ref, sc_ref, q_ref, k_ref, v_ref, rel_ref, g_ref, o_ref, m_scr, l_scr, acc_scr):
    hh = pl.program_id(1)
    pair = pl.program_id(2)
    qi = qi_ref[pair]
    kj = kj_ref[pair]
    tq = q_ref.shape[0]
    slope = sc_ref[0, hh]

    @pl.when(kj == 0)
    def _():
        m_scr[...] = jnp.full(m_scr.shape, NEG_INF, F32)
        l_scr[...] = jnp.zeros(l_scr.shape, F32)
        acc_scr[...] = jnp.zeros(acc_scr.shape, F32)

    def accumulate(bias_fn, shift):
        k = k_ref[...]
        v = v_ref[...]
        rb = min(DA_ROWS, tq)
        lane = lax.broadcasted_iota(jnp.int32, (rb, q_ref.shape[1]), 1)
        old = [(m_scr[j], l_scr[j], acc_scr[j]) for j in range(2)]
        new = [([], [], []) for _ in range(2)]
        blocks = [slice(blk * rb, (blk + 1) * rb) for blk in range(tq // rb)]
        scores = {}
        for rows in blocks:
            q = q_ref[rows, :]
            for j in range(2):
                qj = jnp.where((lane // DA_HEAD) == j, q, jnp.zeros_like(q)) * jnp.asarray(DA_HEAD ** -0.5, q.dtype)
                scores[rows.start, j] = lax.dot_general(qj, k, (((1,), (1,)), ((), ())), preferred_element_type=F32)
        width = m_scr.shape[2]
        for rows in blocks:
            for j in range(2):
                s = bias_fn(scores[rows.start, j], rows)
                m_old = old[j][0][rows, :]
                m_new = jnp.maximum(m_old, jnp.broadcast_to(jnp.max(s, axis=-1, keepdims=True), (rb, width)) + shift)
                p = jnp.exp(s - jnp.concatenate([m_new - shift] * (s.shape[1] // width), axis=1))
                corr = jnp.exp(m_old - m_new)
                new[j][0].append(m_new)
                new[j][1].append(corr * old[j][1][rows, :]
                                 + jnp.broadcast_to(jnp.sum(p, axis=-1, keepdims=True), (rb, width)))
                new[j][2].append(corr * old[j][2][rows, :] + jnp.dot(p.astype(BF16), v, preferred_element_type=F32))
        for j in range(2):
            m_scr[j] = jnp.concatenate(new[j][0], axis=0)
            l_scr[j] = jnp.concatenate(new[j][1], axis=0)
            acc_scr[j] = jnp.concatenate(new[j][2], axis=0)

    @pl.when(kj < qi)
    def _():
        accumulate(lambda s, rows: s + rel_ref[rows, :] * (-slope), -slope * ((qi - kj) * tq).astype(F32))

    @pl.when(kj == qi)
    def _():
        def masked(s, rows):
            row = lax.broadcasted_iota(jnp.int32, s.shape, 0) + rows.start
            col = lax.broadcasted_iota(jnp.int32, s.shape, 1)
            allowed = (col // CHUNK) <= (row // CHUNK)
            return jnp.where(allowed, s + jnp.abs(rel_ref[rows, :]) * (-slope), NEG_INF)

        accumulate(masked, 0.0)
        lam_full = sc_ref[1, 0]
        out_scale = sc_ref[1, 1]
        o = acc_scr[0] / l_scr[0] - lam_full * (acc_scr[1] / l_scr[1])
        o = o * lax.rsqrt(jnp.mean(o * o, axis=-1, keepdims=True) + LN_EPS)
        o_ref[...] = (o * g_ref[...] * out_scale).astype(o_ref.dtype)


def _diff_attention(qkv, scalars, subln_g, bsz, seq):
    t = bsz * seq
    tq = min(DA_TILE, seq)
    nq = seq // tq
    n_h = DA_HEADS
    pairs = [(i, j) for i in range(nq) for j in range(i + 1)]
    qi_of = jnp.array([i for i, _ in pairs], jnp.int32)
    kj_of = jnp.array([j for _, j in pairs], jnp.int32)
    pos = jnp.arange(tq, dtype=F32)
    rel = pos[:, None] - pos[None, :]
    grid_spec = pltpu.PrefetchScalarGridSpec(
        num_scalar_prefetch=2,
        grid=(bsz, n_h, len(pairs)),
        in_specs=[pl.BlockSpec(memory_space=pltpu.SMEM),
                  pl.BlockSpec((tq, 2 * DA_HEAD), lambda b, h, p, qi, kj: (b * nq + qi[p], h)),
                  pl.BlockSpec((tq, 2 * DA_HEAD), lambda b, h, p, qi, kj: (b * nq + kj[p], n_h + h)),
                  pl.BlockSpec((tq, 2 * DA_HEAD), lambda b, h, p, qi, kj: (b * nq + kj[p], 2 * n_h + h)),
                  pl.BlockSpec((tq, tq), lambda b, h, p, qi, kj: (0, 0)),
                  pl.BlockSpec((1, 2 * DA_HEAD), lambda b, h, p, qi, kj: (0, 0))],
        out_specs=pl.BlockSpec((tq, 2 * DA_HEAD), lambda b, h, p, qi, kj: (b * nq + qi[p], h)),
        scratch_shapes=[pltpu.VMEM((2, tq, 2 * DA_HEAD), F32), pltpu.VMEM((2, tq, 2 * DA_HEAD), F32),
                        pltpu.VMEM((2, tq, 2 * DA_HEAD), F32)])
    return pl.pallas_call(
        _diff_attn_kernel,
        name="diff_attention",
        grid_spec=grid_spec,
        out_shape=jax.ShapeDtypeStruct((t, D_MODEL), BF16),
        compiler_params=_cparams("parallel", "parallel", "arbitrary"),
    )(qi_of, kj_of, scalars, qkv, qkv, qkv, rel, subln_g.reshape(1, -1))


def diff_attention_layer(h2, bsz, seq, w_qkv, lam, subln_g, w_o, layer_idx, g, b):
    lam_init = 0.8 - 0.6 * math.exp(-0.3 * layer_idx)
    lf = lam.astype(F32)
    lam_full = jnp.exp(jnp.sum(lf[0] * lf[1])) - jnp.exp(jnp.sum(lf[2] * lf[3])) + lam_init
    slopes = 2.0 ** (-8.0 * jnp.arange(1, DA_HEADS + 1, dtype=F32) / DA_HEADS)
    scalars = jnp.stack([slopes, jnp.zeros((DA_HEADS,), F32).at[0].set(lam_full).at[1].set(1.0 - lam_init)])
    (qkv,) = _linear(h2, [w_qkv.astype(BF16)], [BF16])
    o = _diff_attention(qkv, scalars, subln_g, bsz, seq)
    return _linear_res_ln(o, w_o.astype(BF16), h2, g, b)


def _mamba_kernel(z_ref, xbc_ref, dt_ref, h_ref, cw_ref, cb_ref, dtb_ref, a_ref, d_ref, ng_ref, wo_ref,
                  lg_ref, lb_ref, o_ref, state_scr, halo_scr, y_scr):
    q = z_ref.shape[0]
    n = M2_STATE

    @pl.when(pl.program_id(1) == 0)
    def _():
        state_scr[...] = jnp.zeros(state_scr.shape, F32)
        halo_scr[...] = jnp.zeros(halo_scr.shape, F32)

    xbc_raw = xbc_ref[...]
    xbc = _silu(_causal_conv(xbc_raw, halo_scr[...], cw_ref[...], cb_ref[...]))
    halo_scr[...] = xbc_raw[q - 8:, :]
    row = lax.broadcasted_iota(jnp.int32, (q, q), 0)
    col = lax.broadcasted_iota(jnp.int32, (q, q), 1)
    lower = row >= col
    tri = lower.astype(F32)
    dt = _softplus(dt_ref[...] + dtb_ref[...])
    a_cs = _cumsum_rows(dt * a_ref[...], tri)
    a_cs_t = a_cs.T
    a_last = a_cs[q - 1:q, :]
    e_cs = jnp.exp(a_cs)
    e_dec = jnp.exp(a_last - a_cs)
    e_last = jnp.exp(a_last)
    lane = lax.broadcasted_iota(jnp.int32, (q, 2 * M2_HEADDIM), 1)
    first = lane < M2_HEADDIM
    srow = lax.broadcasted_iota(jnp.int32, (2 * M2_HEADDIM, n), 0)
    heads_per_group = M2_HEADS // M2_GROUPS
    for grp in range(M2_GROUPS):
        bm = xbc[:, M2_INNER + grp * n:M2_INNER + (grp + 1) * n]
        cm = xbc[:, M2_INNER + M2_GROUPS * n + grp * n:M2_INNER + M2_GROUPS * n + (grp + 1) * n]
        cb = _bdot_nt(cm, bm)
        for pair in range(heads_per_group // 2):
            h0 = grp * heads_per_group + 2 * pair
            cols = slice(h0 * M2_HEADDIM, (h0 + 2) * M2_HEADDIM)
            xs = xbc[:, cols]
            dtp = jnp.where(first, dt[:, h0:h0 + 1], dt[:, h0 + 1:h0 + 2])
            xdt = xs * dtp
            y = jnp.zeros((q, 2 * M2_HEADDIM), F32)
            for s in range(2):
                hd = h0 + s
                seg = a_cs[:, hd:hd + 1] - a_cs_t[hd:hd + 1, :]
                lmat = jnp.exp(jnp.where(lower, seg, NEG_INF))
                part = jnp.where(first if s == 0 else jnp.logical_not(first), xdt, 0.0)
                y = y + _bdot(cb * lmat, part)
            state = state_scr[pl.ds(h0 * M2_HEADDIM, 2 * M2_HEADDIM), :]
            y_off = _bdot_nt(cm, state)
            y = y + y_off * jnp.where(first, e_cs[:, h0:h0 + 1], e_cs[:, h0 + 1:h0 + 2])
            dec = jnp.where(first, e_dec[:, h0:h0 + 1], e_dec[:, h0 + 1:h0 + 2])
            grow = jnp.where(srow < M2_HEADDIM, e_last[:, h0:h0 + 1], e_last[:, h0 + 1:h0 + 2])
            state_scr[pl.ds(h0 * M2_HEADDIM, 2 * M2_HEADDIM), :] = state * grow + _bdot_tn(xdt * dec, bm)
            y_scr[:, cols] = y + d_ref[:, cols] * xs
    gate = _silu(z_ref[...])
    y = y_scr[...] * gate
    gw = M2_INNER // M2_GROUPS
    parts = []
    for grp in range(M2_GROUPS):
        yg = y[:, grp * gw:(grp + 1) * gw]
        parts.append(yg * lax.rsqrt(jnp.mean(yg * yg, axis=-1, keepdims=True) + LN_EPS))
    yn = jnp.concatenate(parts, axis=1) * ng_ref[...]
    out = jnp.dot(yn.astype(BF16), wo_ref[...], preferred_element_type=F32)
    o_ref[...] = _layer_norm_rows(ALPHA * h_ref[...] + out, lg_ref[...], lb_ref[...])


def mamba2_layer(h2, bsz, seq, w_in, conv_w, conv_b, dt_bias, a_log, d_skip, norm_g, w_out, g, b):
    t = bsz * seq
    q = min(M2_Q, seq)
    nc = seq // q
    xbc_w = M2_INNER + 2 * M2_GROUPS * M2_STATE
    wb = w_in.astype(BF16)
    hpad = 128 - M2_HEADS
    pad_heads = lambda a: jnp.pad(a.astype(F32).reshape(1, -1), ((0, 0), (0, hpad)))
    z, xbc, dt = _linear(h2, [wb[:, :M2_INNER], wb[:, M2_INNER:M2_INNER + xbc_w],
                              jnp.pad(wb[:, M2_INNER + xbc_w:], ((0, 0), (0, hpad)))],
                         [F32, F32, F32], tm=256)
    d_cols = jnp.repeat(d_skip.astype(F32), M2_HEADDIM).reshape(1, -1)
    rowspec = lambda w: pl.BlockSpec((q, w), lambda i, j: (i * nc + j, 0))
    return pl.pallas_call(
        _mamba_kernel,
        name="mamba2",
        grid=(bsz, nc),
        in_specs=[rowspec(M2_INNER), rowspec(xbc_w), rowspec(128), rowspec(D_MODEL),
                  _full((M2_CONV, xbc_w)), _full((1, xbc_w)), _full((1, 128)), _full((1, 128)),
                  _full((1, M2_INNER)), _full((1, M2_INNER)), _full((M2_INNER, D_MODEL)),
                  _full((1, D_MODEL)), _full((1, D_MODEL))],
        out_specs=rowspec(D_MODEL),
        out_shape=jax.ShapeDtypeStruct((t, D_MODEL), F32),
        scratch_shapes=[pltpu.VMEM((M2_HEADS * M2_HEADDIM, M2_STATE), F32),
                        pltpu.VMEM((8, xbc_w), F32),
                        pltpu.VMEM((q, M2_INNER), F32)],
        compiler_params=_cparams("parallel", "arbitrary"),
    )(z, xbc, dt, h2, conv_w.astype(F32), conv_b.reshape(1, -1), pad_heads(dt_bias),
      pad_heads(-jnp.exp(a_log.astype(F32))), d_cols, norm_g.reshape(1, -1), w_out.astype(BF16),
      g.reshape(1, -1), b.reshape(1, -1))


def _mlstm_proj_kernel(xm_ref, cw_ref, cb_ref, wq_ref, wk_ref, wv_ref, wg_ref, bg_ref,
                       xc_ref, q_ref, k_ref, v_ref, gate_ref, halo_scr):
    tq = xm_ref.shape[0]

    @pl.when(pl.program_id(1) == 0)
    def _():
        halo_scr[...] = jnp.zeros(halo_scr.shape, F32)

    xm = xm_ref[...]
    xc = _silu(_causal_conv(xm, halo_scr[...], cw_ref[...], cb_ref[...]))
    halo_scr[...] = xm[tq - 8:, :]
    xc_ref[...] = xc
    gates = jnp.zeros((tq, bg_ref.shape[1]), F32) + bg_ref[...]
    for hd in range(ML_HEADS):
        cols = slice(hd * ML_HEADDIM, (hd + 1) * ML_HEADDIM)
        qh = _bdot(xc[:, cols], wq_ref[hd])
        kh = _bdot(xc[:, cols], wk_ref[hd]) * ML_HEADDIM ** -0.5
        vh = _bdot(xm[:, cols], wv_ref[hd])
        q_ref[:, cols] = qh.astype(BF16)
        k_ref[:, cols] = kh.astype(BF16)
        v_ref[:, cols] = vh.astype(BF16)
        gates = gates + _bdot(qh, wg_ref[0, cols, :]) + _bdot(kh, wg_ref[1, cols, :]) + _bdot(vh, wg_ref[2, cols, :])
    gate_ref[...] = gates


def _mlstm_cell_kernel(q_ref, k_ref, v_ref, gate_ref, h_ref, c_scr, n_scr, m_scr):
    hd = pl.program_id(1)
    qn = q_ref.shape[0]

    @pl.when(pl.program_id(2) == 0)
    def _():
        c_scr[...] = jnp.zeros(c_scr.shape, F32)
        n_scr[...] = jnp.zeros(n_scr.shape, F32)
        m_scr[...] = jnp.zeros(m_scr.shape, F32)

    q = q_ref[...]
    k = k_ref[...]
    v = v_ref[...]
    gates = gate_ref[...]
    glane = lax.broadcasted_iota(jnp.int32, gates.shape, 1)
    gsub = lax.broadcasted_iota(jnp.int32, (gates.shape[1], qn), 0)
    row = lax.broadcasted_iota(jnp.int32, (qn, qn), 0)
    col = lax.broadcasted_iota(jnp.int32, (qn, qn), 1)
    lower = row >= col
    logf = jnp.minimum(gates, 0.0) - jnp.log(1.0 + jnp.exp(-jnp.abs(gates)))
    cum = _cumsum_rows(logf, lower.astype(F32))
    ii = jnp.sum(jnp.where(glane == hd, gates, 0.0), axis=-1, keepdims=True)
    bcs = jnp.sum(jnp.where(glane == ML_HEADS + hd, cum, 0.0), axis=-1, keepdims=True)
    ii_row = jnp.sum(jnp.where(gsub == hd, gates.T, 0.0), axis=0, keepdims=True)
    bcs_row = jnp.sum(jnp.where(gsub == ML_HEADS + hd, cum.T, 0.0), axis=0, keepdims=True)
    m_prev = m_scr[0:1, 0:1]
    dmat = jnp.where(lower, bcs - bcs_row + ii_row, NEG_INF)
    inter = bcs + m_prev
    m_row = jnp.maximum(jnp.max(dmat, axis=-1, keepdims=True), inter)
    s = _bdot_nt(q, k) * jnp.exp(dmat - m_row)
    w_inter = jnp.exp(inter - m_row)
    num = _bdot(s, v) + w_inter * _bdot(q, c_scr[...])
    den = jnp.sum(s, axis=-1, keepdims=True) + w_inter * jnp.sum(q.astype(F32) * n_scr[0:1, :], axis=-1, keepdims=True)
    h_ref[...] = num / jnp.maximum(jnp.abs(den), jnp.exp(-m_row))
    b_last = bcs[qn - 1:qn, :]
    gdec = b_last - bcs + ii
    m_new = jnp.maximum(b_last + m_prev, jnp.max(gdec, axis=0, keepdims=True))
    wk = jnp.exp(gdec - m_new)
    decay = jnp.exp(b_last + m_prev - m_new)
    kw = k.astype(F32) * wk
    c_scr[...] = decay * c_scr[...] + _bdot_tn(kw, v)
    n_scr[...] = decay * n_scr[...] + jnp.sum(kw, axis=0, keepdims=True)
    m_scr[...] = jnp.zeros(m_scr.shape, F32) + m_new


def _mlstm_out_kernel(hc_ref, xc_ref, og_ref, h_ref, ng_ref, sk_ref, wd_ref, lg_ref, lb_ref, o_ref):
    hc = hc_ref[...]
    parts = []
    for hd in range(ML_HEADS):
        x = hc[:, hd * ML_HEADDIM:(hd + 1) * ML_HEADDIM]
        mu = jnp.mean(x, axis=-1, keepdims=True)
        c = x - mu
        parts.append(c * lax.rsqrt(jnp.mean(c * c, axis=-1, keepdims=True) + LN_EPS))
    hn = jnp.concatenate(parts, axis=1) * ng_ref[...] + sk_ref[...] * xc_ref[...]
    out = hn * _sigmoid(og_ref[...])
    y = jnp.dot(out.astype(BF16), wd_ref[...], preferred_element_type=F32)
    o_ref[...] = _layer_norm_rows(ALPHA * h_ref[...] + y, lg_ref[...], lb_ref[...])


def mlstm_layer(h2, bsz, seq, w_in, conv_w, conv_b, w_q, w_k, w_v, w_gates, b_gates, norm_g, skip, w_down, g, b):
    t = bsz * seq
    wb = w_in.astype(BF16)
    xm, og = _linear(h2, [wb[:, :ML_INNER], wb[:, ML_INNER:]], [F32, F32])
    tq = min(256, seq)
    nt = seq // tq
    rows = lambda w: pl.BlockSpec((tq, w), lambda i, j: (i * nt + j, 0))
    gpad = 128 - 2 * ML_HEADS
    wg = jnp.pad(w_gates.astype(BF16), ((0, 0), (0, 0), (0, gpad)))
    bg = jnp.pad(b_gates.astype(F32).reshape(1, -1), ((0, 0), (0, gpad)))
    xc, q, k, v, gates = pl.pallas_call(
        _mlstm_proj_kernel,
        name="mlstm_proj",
        grid=(bsz, nt),
        in_specs=[rows(ML_INNER), _full((ML_CONV, ML_INNER)), _full((1, ML_INNER)),
                  _full(w_q.shape), _full(w_k.shape), _full(w_v.shape), _full(wg.shape), _full(bg.shape)],
        out_specs=[rows(ML_INNER), rows(ML_INNER), rows(ML_INNER), rows(ML_INNER), rows(128)],
        out_shape=[jax.ShapeDtypeStruct((t, ML_INNER), F32)] + [jax.ShapeDtypeStruct((t, ML_INNER), BF16)] * 3
        + [jax.ShapeDtypeStruct((t, 128), F32)],
        scratch_shapes=[pltpu.VMEM((8, ML_INNER), F32)],
        compiler_params=_cparams("parallel", "arbitrary"),
    )(xm, conv_w.astype(F32), conv_b.reshape(1, -1), w_q.astype(BF16), w_k.astype(BF16), w_v.astype(BF16), wg, bg)
    qn = min(ML_Q, seq)
    nc = seq // qn
    head = pl.BlockSpec((qn, ML_HEADDIM), lambda i, hd, j: (i * nc + j, hd))
    hc = pl.pallas_call(
        _mlstm_cell_kernel,
        name="mlstm_cell",
        grid=(bsz, ML_HEADS, nc),
        in_specs=[head, head, head, pl.BlockSpec((qn, 128), lambda i, hd, j: (i * nc + j, 0))],
        out_specs=head,
        out_shape=jax.ShapeDtypeStruct((t, ML_INNER), F32),
        scratch_shapes=[pltpu.VMEM((ML_HEADDIM, ML_HEADDIM), F32), pltpu.VMEM((8, ML_HEADDIM), F32),
                        pltpu.VMEM((8, 128), F32)],
        compiler_params=_cparams("parallel", "parallel", "arbitrary"),
    )(q, k, v, gates)
    tm = min(512, t)
    row = lambda w: pl.BlockSpec((tm, w), lambda i: (i, 0))
    return pl.pallas_call(
        _mlstm_out_kernel,
        name="mlstm_out",
        grid=(t // tm,),
        in_specs=[row(ML_INNER), row(ML_INNER), row(ML_INNER), row(D_MODEL), _full((1, ML_INNER)),
                  _full((1, ML_INNER)), _full((ML_INNER, D_MODEL)), _full((1, D_MODEL)), _full((1, D_MODEL))],
        out_specs=row(D_MODEL),
        out_shape=jax.ShapeDtypeStruct((t, D_MODEL), F32),
        compiler_params=_cparams("parallel"),
    )(hc, xc, og, h2, norm_g.reshape(1, -1), skip.reshape(1, -1), w_down.astype(BF16),
      g.reshape(1, -1), b.reshape(1, -1))


def _cross_attn_kernel(h_ref, k_ref, v_ref, wq_ref, wo_ref, g_ref, b_ref, o_ref):
    h = h_ref[...]
    q = jnp.dot(h.astype(BF16), wq_ref[...], preferred_element_type=F32)
    parts = []
    for hd in range(XA_HEADS):
        cols = slice(hd * XA_HEADDIM, (hd + 1) * XA_HEADDIM)
        s = _bdot_nt(q[:, cols], k_ref[:, cols]) * XA_HEADDIM ** -0.5
        p = jnp.exp(s - jnp.max(s, axis=-1, keepdims=True))
        p = p / jnp.sum(p, axis=-1, keepdims=True)
        parts.append(_bdot(p, v_ref[:, cols]))
    o = jnp.concatenate(parts, axis=1)
    y = jnp.dot(o.astype(BF16), wo_ref[...], preferred_element_type=F32)
    o_ref[...] = _layer_norm_rows(ALPHA * h + y, g_ref[...], b_ref[...])


def cross_attention_layer(h2, mem2, bsz, seq, w_q, w_kv, w_o, g, b, tq=512):
    t = bsz * seq
    mlen = mem2.shape[0] // bsz
    tq = min(tq, seq)
    nq = seq // tq
    (kv,) = _linear(mem2, [w_kv.astype(BF16)], [BF16], tm=256)
    return pl.pallas_call(
        _cross_attn_kernel,
        name="cross_attention",
        grid=(bsz, nq),
        in_specs=[pl.BlockSpec((tq, D_MODEL), lambda i, j: (i * nq + j, 0)),
                  pl.BlockSpec((mlen, D_MODEL), lambda i, j: (i, 0)),
                  pl.BlockSpec((mlen, D_MODEL), lambda i, j: (i, 1)),
                  _full((D_MODEL, D_MODEL)), _full((D_MODEL, D_MODEL)), _full((1, D_MODEL)), _full((1, D_MODEL))],
        out_specs=pl.BlockSpec((tq, D_MODEL), lambda i, j: (i * nq + j, 0)),
        out_shape=jax.ShapeDtypeStruct((t, D_MODEL), F32),
        compiler_params=_cparams("parallel", "parallel"),
    )(h2, kv, kv, w_q.astype(BF16), w_o.astype(BF16), g.reshape(1, -1), b.reshape(1, -1))


_PK_PIECES = (('a', 0, 0), ('a', 0, 8), ('a', 1, 0), ('a', 2, 0), ('a', 3, 0),
              ('b', 0, 8), ('b', 0, 0), ('b', 1, 0), ('b', 2, 0))
_PK_INVALID = 1 << 20


def _peer_piece_ids():
    ids = []
    seen = set()
    for kind, fixed, off in _PK_PIECES:
        for i in range(8):
            a, b = (fixed, off + i) if kind == 'a' else (off + i, fixed)
            ok = (a + 1) * (b + 1) <= PK_TOPK and (a, b) not in seen
            seen.add((a, b))
            ids.append(a * PK_TOPK + b if ok else _PK_INVALID)
    assert sum(i != _PK_INVALID for i in ids) == sum((a + 1) * (b + 1) <= PK_TOPK
                                                     for a in range(PK_TOPK) for b in range(PK_TOPK))
    return jnp.array(ids, jnp.int32).reshape(-1, 1)


def _peer_select_kernel(h_ref, wq_ref, keys_ref, flat_ref, eidx_ref, gate_ref, q_scr, e_scr):
    tm = h_ref.shape[0]
    q_scr[...] = jnp.dot(h_ref[...].astype(BF16), wq_ref[...], preferred_element_type=F32).astype(BF16)
    key_id = lax.broadcasted_iota(jnp.int32, (PK_NKEYS, tm), 0)
    rank_id = lax.broadcasted_iota(jnp.int32, (PK_TOPK, tm), 0)
    flat = jnp.broadcast_to(flat_ref[...], (flat_ref.shape[0], tm))
    neg = jnp.float32(-jnp.inf)
    half = PK_QDIM // 2
    zero = jnp.zeros((PK_TOPK, tm), F32)

    n_col = PK_NKEYS // 8

    def sort_columns(s):
        vals = [s[8 * v:8 * v + 8, :] for v in range(n_col)]
        keys = [key_id[8 * v:8 * v + 8, :] for v in range(n_col)]
        for rnd in range(n_col):
            for i in range(rnd % 2, n_col - 1, 2):
                swap = vals[i + 1] > vals[i]
                vals[i], vals[i + 1] = jnp.where(swap, vals[i + 1], vals[i]), jnp.where(swap, vals[i], vals[i + 1])
                keys[i], keys[i + 1] = jnp.where(swap, keys[i + 1], keys[i]), jnp.where(swap, keys[i], keys[i + 1])
        return vals + keys

    def top1(a, st):
        vals, keys, top_v, top_i = list(st[:n_col]), list(st[n_col:2 * n_col]), st[2 * n_col], st[2 * n_col + 1]
        m = jnp.max(vals[0], axis=0, keepdims=True)
        idx = jnp.min(jnp.where(vals[0] == m, keys[0], PK_NKEYS), axis=0, keepdims=True)
        win = keys[0] == idx
        new_vals = [jnp.where(win, vals[k + 1], vals[k]) for k in range(n_col - 1)] + [jnp.where(win, neg, vals[-1])]
        new_keys = [jnp.where(win, keys[k + 1], keys[k]) for k in range(n_col - 1)] + [keys[-1]]
        sel = rank_id == a
        return new_vals + new_keys + [jnp.where(sel, m, top_v), jnp.where(sel, idx.astype(F32), top_i)]

    def candidates(vals0, idxs0, vals1, idxs1):
        rep = lambda x, r: jnp.broadcast_to(x[r:r + 1, :], (8, tm))
        v, c = [], []
        for kind, fixed, off in _PK_PIECES:
            if kind == 'a':
                v.append(rep(vals0, fixed) + vals1[off:off + 8, :])
                c.append(rep(idxs0, fixed) * PK_NKEYS + idxs1[off:off + 8, :])
            else:
                v.append(vals0[off:off + 8, :] + rep(vals1, fixed))
                c.append(idxs0[off:off + 8, :] * PK_NKEYS + rep(idxs1, fixed))
        cand = jnp.where(flat == _PK_INVALID, neg, jnp.concatenate(v, axis=0))
        return cand, jnp.concatenate(c, axis=0)

    def top2(kk, cand, cidx, cv, ce):
        m = jnp.max(cand, axis=0, keepdims=True)
        pos = jnp.min(jnp.where(cand == m, flat, _PK_INVALID), axis=0, keepdims=True)
        hit = flat == pos
        e = jnp.max(jnp.where(hit, cidx, -1.0), axis=0, keepdims=True)
        sel = rank_id == kk
        return jnp.where(hit, neg, cand), jnp.where(sel, m, cv), jnp.where(sel, e, ce)

    tops = None
    for stage in range(PK_HEADS + 1):
        first = stage < PK_HEADS
        second = stage > 0
        init = []
        n_st = 2 * n_col + 2
        if first:
            for j in range(2):
                cols = slice(stage * PK_QDIM + j * half, stage * PK_QDIM + (j + 1) * half)
                init += sort_columns(_bdot_nt(keys_ref[j], q_scr[:, cols])) + [zero, zero]
        if second:
            cand0, cidx = candidates(*tops)
            init += [cand0, zero, zero]

        def body(a, st, first=first, second=second, cidx=cidx if second else None):
            st = list(st)
            out = []
            if first:
                out += top1(a, st[0:n_st]) + top1(a, st[n_st:2 * n_st])
                st = st[2 * n_st:]
            if second:
                out += top2(a, st[0], cidx, st[1], st[2])
            return tuple(out)

        res = lax.fori_loop(0, PK_TOPK // 2, lambda i, st, body=body: body(2 * i + 1, body(2 * i, st)), tuple(init))
        if second:
            cv, ce = res[-2], res[-1]
            p = jnp.exp(cv - cv[0:1, :])
            rows = slice((stage - 1) * PK_TOPK, stage * PK_TOPK)
            gate_ref[rows, :] = p / jnp.sum(p, axis=0, keepdims=True)
            e_scr[rows, :] = ce * ROW_WORDS
        if first:
            tops = (res[n_st - 2], res[n_st - 1], res[2 * n_st - 2], res[2 * n_st - 1])
    eidx_ref[...] = e_scr[...].T.astype(jnp.int32)


def _peer_select(h2, wq_bf, keys_bf, tm=128):
    t = h2.shape[0]
    tm = min(tm, t)
    flat = _peer_piece_ids()
    return pl.pallas_call(
        _peer_select_kernel,
        name="peer_select",
        grid=(t // tm,),
        in_specs=[pl.BlockSpec((tm, D_MODEL), lambda i: (i, 0)), _full(wq_bf.shape), _full(keys_bf.shape),
                  _full(flat.shape)],
        out_specs=[pl.BlockSpec((tm, PK_SLOTS), lambda i: (i, 0)),
                   pl.BlockSpec((PK_SLOTS, tm), lambda i: (0, i))],
        out_shape=[jax.ShapeDtypeStruct((t, PK_SLOTS), jnp.int32),
                   jax.ShapeDtypeStruct((PK_SLOTS, t), F32)],
        scratch_shapes=[pltpu.VMEM((tm, PK_HEADS * PK_QDIM), BF16), pltpu.VMEM((PK_SLOTS, tm), F32)],
        compiler_params=_cparams("parallel"),
    )(h2, wq_bf, keys_bf, flat)


def _pack_rows(w):
    e, d = w.shape
    wb = lax.bitcast_convert_type(w.astype(BF16), jnp.uint16).astype(jnp.uint32)
    packed = (wb[:, d // 2:] << 16) | wb[:, :d // 2]
    return lax.bitcast_convert_type(packed, jnp.int32).reshape(e * ROW_WORDS, 128)


def _unpack_lo(w):
    return pltpu.bitcast(w << 16, F32)


def _unpack_hi(w):
    return pltpu.bitcast(w & jnp.int32(-65536), F32)


def _gather_group(idx_ref, t0, tb, tab_ref, tiles_ref, first_tile):
    for u in range(PK_GROUP):
        row_ref = idx_ref.at[jnp.minimum(t0 + u, tb - 1)]
        for r in range(PK_SLOTS):
            i = pl.multiple_of(row_ref[r], ROW_WORDS)
            tiles_ref[first_tile + u, pl.ds(r * ROW_WORDS, ROW_WORDS), :] = tab_ref[pl.ds(i, ROW_WORDS), :]


def _group_pipeline(tb, idx_ref, tab_ref, tiles_ref, dense_group):
    _gather_group(idx_ref, 0, tb, tab_ref, tiles_ref, 0)

    def trip(i, carry):
        t0 = 2 * PK_GROUP * i
        for half in range(2):
            base = pl.multiple_of(t0 + half * PK_GROUP, PK_GROUP)
            _gather_group(idx_ref, base + PK_GROUP, tb, tab_ref, tiles_ref, (1 - half) * PK_GROUP)
            dense_group(base, half * PK_GROUP)
        return carry

    lax.fori_loop(0, tb // (2 * PK_GROUP), trip, 0)


def _lane_block(base):
    return pl.multiple_of((base // 128) * 128, 128)


def _peer_score_kernel(idx_ref, x_ref, gate_ref, tab_ref, act_ref, tiles_ref):
    tb = x_ref.shape[0]
    lane = lax.broadcasted_iota(jnp.int32, (PK_SLOTS, 128), 1)
    half = D_MODEL // 2

    def dense_group(base, first_tile):
        xg = x_ref[pl.ds(base, PK_GROUP), :]
        blk = _lane_block(base)
        cols = act_ref[:, pl.ds(blk, 128)]
        for u in range(PK_GROUP):
            acc = jnp.zeros((PK_SLOTS, 128), F32)
            for s in range(ROW_WORDS):
                w = tiles_ref[first_tile + u, pl.ds(s, PK_SLOTS, stride=ROW_WORDS), :]
                acc = (acc + _unpack_lo(w) * xg[u:u + 1, s * 128:(s + 1) * 128]
                       + _unpack_hi(w) * xg[u:u + 1, half + s * 128:half + (s + 1) * 128])
            cols = jnp.where(lane == base + u - blk, jnp.sum(acc, axis=-1, keepdims=True), cols)
        act_ref[:, pl.ds(blk, 128)] = cols

    act_ref[...] = jnp.zeros(act_ref.shape, F32)
    _group_pipeline(tb, idx_ref, tab_ref, tiles_ref, dense_group)
    act_ref[...] = _gelu(act_ref[...]) * gate_ref[...]


def _peer_score(eidx, x2, gate, tab, tb=256):
    t = x2.shape[0]
    tb = min(tb, t)
    slots = pl.BlockSpec((PK_SLOTS, tb), lambda i: (0, i))
    return pl.pallas_call(
        _peer_score_kernel,
        name="peer_score",
        grid=(t // tb,),
        in_specs=[pl.BlockSpec((tb, PK_SLOTS), lambda i: (i, 0), memory_space=pltpu.SMEM),
                  pl.BlockSpec((tb, D_MODEL), lambda i: (i, 0)), slots, _full(tab.shape)],
        out_specs=slots,
        out_shape=jax.ShapeDtypeStruct((PK_SLOTS, t), F32),
        scratch_shapes=[pltpu.VMEM((2 * PK_GROUP, PK_SLOTS * ROW_WORDS, 128), jnp.int32)],
        compiler_params=_cparams("parallel"),
    )(eidx, x2, gate, tab)


def _peer_combine_kernel(idx_ref, act_ref, h_ref, g_ref, b_ref, tab_ref, out_ref, y_ref, tiles_ref):
    tb = h_ref.shape[0]
    lane = lax.broadcasted_iota(jnp.int32, (PK_SLOTS, 128), 1)

    def dense_group(base, first_tile):
        blk = _lane_block(base)
        act = act_ref[:, pl.ds(blk, 128)]
        rows = []
        for u in range(PK_GROUP):
            a = jnp.sum(jnp.where(lane == base + u - blk, act, 0.0), axis=-1, keepdims=True)
            lo, hi = [], []
            for s in range(ROW_WORDS):
                w = tiles_ref[first_tile + u, pl.ds(s, PK_SLOTS, stride=ROW_WORDS), :]
                lo.append(jnp.sum(a * _unpack_lo(w), axis=0, keepdims=True))
                hi.append(jnp.sum(a * _unpack_hi(w), axis=0, keepdims=True))
            rows.append(jnp.concatenate(lo + hi, axis=1))
        y_ref[pl.ds(base, PK_GROUP), :] = jnp.concatenate(rows, axis=0)

    _group_pipeline(tb, idx_ref, tab_ref, tiles_ref, dense_group)
    out_ref[...] = _layer_norm_rows(ALPHA * h_ref[...] + y_ref[...], g_ref[...], b_ref[...])


def _peer_combine(eidx, act, h2, g, b, tab, tb=256):
    t = h2.shape[0]
    tb = min(tb, t)
    tok = pl.BlockSpec((tb, D_MODEL), lambda i: (i, 0))
    return pl.pallas_call(
        _peer_combine_kernel,
        name="peer_combine",
        grid=(t // tb,),
        in_specs=[pl.BlockSpec((tb, PK_SLOTS), lambda i: (i, 0), memory_space=pltpu.SMEM),
                  pl.BlockSpec((PK_SLOTS, tb), lambda i: (0, i)),
                  tok, _full((1, D_MODEL)), _full((1, D_MODEL)), _full(tab.shape)],
        out_specs=tok,
        out_shape=jax.ShapeDtypeStruct((t, D_MODEL), F32),
        scratch_shapes=[pltpu.VMEM((tb, D_MODEL), F32),
                        pltpu.VMEM((2 * PK_GROUP, PK_SLOTS * ROW_WORDS, 128), jnp.int32)],
        compiler_params=_cparams("parallel"),
    )(eidx, act, h2, g.reshape(1, -1), b.reshape(1, -1), tab)


def peer_layer(h2, w_query, sub_keys, u, v, g, b):
    eidx, gate = _peer_select(h2, w_query.astype(BF16), sub_keys.astype(BF16))
    act = _peer_score(eidx, h2, gate, _pack_rows(u))
    return _peer_combine(eidx, act, h2, g, b, _pack_rows(v))


def kernel(x, mem, s5_lam_re, s5_lam_im, s5_log_dt, s5_b_re, s5_b_im, s5_c_re, s5_c_im, s5_d, s5_w_glu, s5_b_glu, da_w_qkv, da_lambda, da_subln_g, da_w_o, m2_w_in, m2_conv_w, m2_conv_b, m2_dt_bias, m2_a_log, m2_d, m2_norm_g, m2_w_out, ml_w_in, ml_conv_w, ml_conv_b, ml_w_q, ml_w_k, ml_w_v, ml_w_gates, ml_b_gates, ml_norm_g, ml_skip, ml_w_down, xa_w_q, xa_w_kv, xa_w_o, pk_w_query, pk_sub_keys, pk_u, pk_v, ln_g, ln_b):
    bsz, seq, dm = x.shape
    t = bsz * seq
    mem2 = mem.reshape(-1, dm)
    h = x.reshape(t, dm)
    for i in range(DEPTH):
        kind, j = i % 4, i // 4
        g0, b0 = ln_g[i, 0], ln_b[i, 0]
        if kind == 0:
            h = s5_layer(h.reshape(bsz, seq, dm), s5_lam_re[j], s5_lam_im[j], s5_log_dt[j], s5_b_re[j], s5_b_im[j],
                         s5_c_re[j], s5_c_im[j], s5_d[j], s5_w_glu[j], s5_b_glu[j], g0, b0)
        elif kind == 1:
            h = diff_attention_layer(h, bsz, seq, da_w_qkv[j], da_lambda[j], da_subln_g[j], da_w_o[j], i, g0, b0)
        elif kind == 2:
            h = mamba2_layer(h, bsz, seq, m2_w_in[j], m2_conv_w[j], m2_conv_b[j], m2_dt_bias[j], m2_a_log[j],
                             m2_d[j], m2_norm_g[j], m2_w_out[j], g0, b0)
        else:
            h = mlstm_layer(h, bsz, seq, ml_w_in[j], ml_conv_w[j], ml_conv_b[j], ml_w_q[j], ml_w_k[j], ml_w_v[j],
                            ml_w_gates[j], ml_b_gates[j], ml_norm_g[j], ml_skip[j], ml_w_down[j], g0, b0)
        h = cross_attention_layer(h, mem2, bsz, seq, xa_w_q[i], xa_w_kv[i], xa_w_o[i], ln_g[i, 1], ln_b[i, 1])
        h = peer_layer(h, pk_w_query[i], pk_sub_keys[i], pk_u[i], pk_v[i], ln_g[i, 2], ln_b[i, 2])
    return h.reshape(bsz, seq, dm)
```

```python
import functools
import math

import jax
import jax.numpy as jnp
from jax import lax
from jax.experimental import pallas as pl
from jax.experimental.pallas import tpu as pltpu

F32 = jnp.float32
BF16 = jnp.bfloat16
HIGHEST = lax.Precision.HIGHEST

D_MODEL = 1024
DEPTH = 4
ALPHA = (2 * DEPTH) ** 0.25
LN_EPS = 1e-5
NEG_INF = -1e30
CHUNK = 64

S5_GROUP = 16
S5_STATE = 64
S5_Q = 64

DA_HEAD = 64
DA_HEADS = 8
DA_TILE = 1024
DA_ROWS = 1024

M2_INNER = 2048
M2_HEADS = 32
M2_HEADDIM = 64
M2_GROUPS = 4
M2_STATE = 128
M2_CONV = 4
M2_Q = 256

ML_INNER = 2048
ML_HEADS = 4
ML_HEADDIM = 512
ML_CONV = 4
ML_Q = 256

XA_HEADS = 4
XA_HEADDIM = 256

PK_HEADS = 8
PK_NKEYS = 128
PK_QDIM = 256
PK_TOPK = 16
PK_SLOTS = PK_HEADS * PK_TOPK
ROW_WORDS = D_MODEL // 2 // 128
PK_STACK_DEPTH = 4
PK_GROUP = 8

VMEM_LIMIT = 56 * 1024 * 1024


def _cparams(*sem):
    return pltpu.CompilerParams(dimension_semantics=sem, vmem_limit_bytes=VMEM_LIMIT)


def _gelu(x):
    return 0.5 * x * (1.0 + jnp.tanh(math.sqrt(2.0 / math.pi) * (x + 0.044715 * x * x * x)))


def _sigmoid(x):
    return 1.0 / (1.0 + jnp.exp(-x))


def _silu(x):
    return x * _sigmoid(x)


def _softplus(x):
    return jnp.maximum(x, 0.0) + jnp.log(1.0 + jnp.exp(-jnp.abs(x)))


def _layer_norm_rows(v, g, b):
    mu = jnp.mean(v, axis=-1, keepdims=True)
    c = v - mu
    var = jnp.mean(c * c, axis=-1, keepdims=True)
    return c * lax.rsqrt(var + LN_EPS) * g + b


def _bdot(a, b):
    return jnp.dot(a.astype(BF16), b.astype(BF16), preferred_element_type=F32)


def _bdot_nt(a, b):
    return lax.dot_general(a.astype(BF16), b.astype(BF16), (((1,), (1,)), ((), ())), preferred_element_type=F32)


def _bdot_tn(a, b):
    return lax.dot_general(a.astype(BF16), b.astype(BF16), (((0,), (0,)), ((), ())), preferred_element_type=F32)


def _full(shape):
    return pl.BlockSpec(shape, lambda *_: (0,) * len(shape))


def _causal_conv(x, halo, w, b):
    q = x.shape[0]
    k = w.shape[0]
    ext = jnp.concatenate([halo, x], axis=0)
    out = x * w[k - 1:k, :] + b
    for d in range(1, k):
        out = out + ext[8 - d:8 - d + q, :] * w[k - 1 - d:k - d, :]
    return out


def _cumsum_rows(a, tri):
    return jnp.dot(tri, a, precision=HIGHEST, preferred_element_type=F32)


def _linear_kernel(*refs, n_out):
    a = refs[0][...].astype(BF16)
    for w_ref, o_ref in zip(refs[1:1 + n_out], refs[1 + n_out:]):
        o_ref[...] = jnp.dot(a, w_ref[...], preferred_element_type=F32).astype(o_ref.dtype)


def _linear(a, ws, out_dtypes, tm=512):
    m, k = a.shape
    tm = min(tm, m)
    return pl.pallas_call(
        functools.partial(_linear_kernel, n_out=len(ws)),
        name="linear",
        grid=(m // tm,),
        in_specs=[pl.BlockSpec((tm, k), lambda i: (i, 0))] + [_full(w.shape) for w in ws],
        out_specs=[pl.BlockSpec((tm, w.shape[1]), lambda i: (i, 0)) for w in ws],
        out_shape=[jax.ShapeDtypeStruct((m, w.shape[1]), dt) for w, dt in zip(ws, out_dtypes)],
        compiler_params=_cparams("parallel"),
    )(a, *ws)


def _linear_res_ln_kernel(a_ref, w_ref, h_ref, g_ref, b_ref, o_ref):
    y = jnp.dot(a_ref[...].astype(BF16), w_ref[...], preferred_element_type=F32)
    o_ref[...] = _layer_norm_rows(ALPHA * h_ref[...] + y, g_ref[...], b_ref[...])


def _linear_res_ln(a, w, h, g, b, tm=512):
    m, k = a.shape
    tm = min(tm, m)
    return pl.pallas_call(
        _linear_res_ln_kernel,
        name="linear_res_ln",
        grid=(m // tm,),
        in_specs=[pl.BlockSpec((tm, k), lambda i: (i, 0)), _full(w.shape),
                  pl.BlockSpec((tm, D_MODEL), lambda i: (i, 0)), _full((1, D_MODEL)), _full((1, D_MODEL))],
        out_specs=pl.BlockSpec((tm, D_MODEL), lambda i: (i, 0)),
        out_shape=jax.ShapeDtypeStruct((m, D_MODEL), F32),
        compiler_params=_cparams("parallel"),
    )(a, w, h, g.reshape(1, -1), b.reshape(1, -1))


def _s5_tables(lam_re, lam_im, log_dt, b_re, b_im, c_re, c_im):
    q = S5_Q
    g, p = lam_re.shape
    lam = lax.complex(lam_re.astype(F32), lam_im.astype(F32))
    dt = jnp.exp(log_dt.astype(F32))[:, None]
    lam_bar = jnp.exp(lam * dt)
    b_bar = ((lam_bar - 1.0) / lam)[..., None] * lax.complex(b_re.astype(F32), b_im.astype(F32))
    c_mat = lax.complex(c_re.astype(F32), c_im.astype(F32))
    pw = jnp.cumprod(jnp.concatenate([jnp.ones((1, g, p), lam_bar.dtype),
                                      jnp.broadcast_to(lam_bar, (q, g, p))], axis=0), axis=0)
    cp = c_mat[None] * pw[:q, :, None, :]
    taps = (jnp.einsum('tgop,gpi->gtoi', cp.real, b_bar.real, precision=HIGHEST)
            - jnp.einsum('tgop,gpi->gtoi', cp.imag, b_bar.imag, precision=HIGHEST))
    pos = jnp.arange(q)
    tau = pos[None, :] - pos[:, None]
    toep = jnp.where((tau >= 0)[None, :, :, None, None], taps[:, jnp.maximum(tau, 0)], 0.0)
    toep = toep.transpose(0, 1, 4, 2, 3).reshape(g, q * S5_GROUP, q * S5_GROUP)
    wb = pw[q - 1 - pos][:, :, :, None] * b_bar[None]
    wt = jnp.concatenate([wb.real, wb.imag], axis=2).transpose(1, 0, 3, 2).reshape(g, q * S5_GROUP, 2 * p)
    cv = c_mat[None] * pw[1:q + 1, :, None, :]
    vt = jnp.concatenate([cv.real, -cv.imag], axis=3).transpose(1, 3, 0, 2).reshape(g, 2 * p, q * S5_GROUP)
    aq = pw[q]
    a_re = jnp.concatenate([aq.real, aq.real], axis=1).reshape(g, 1, 2 * p)
    a_im = jnp.concatenate([-aq.imag, aq.imag], axis=1).reshape(g, 1, 2 * p)
    return toep.astype(BF16), wt.astype(BF16), vt.astype(BF16), a_re, a_im


def _s5_kernel(x_ref, tt_ref, wt_ref, vt_ref, ar_ref, ai_ref, y_ref, s_scr, h_scr, *, nb):
    x = x_ref[0]
    y_ref[0] = jnp.dot(x, tt_ref[0], preferred_element_type=F32)
    s_scr[...] = jnp.dot(x, wt_ref[0], preferred_element_type=F32)
    a_re = ar_ref[0]
    a_im = ai_ref[0]
    half = s_scr.shape[1] // 2

    def step(c, h):
        rows = pl.ds(pl.multiple_of(c * nb, nb), nb)
        h_scr[rows, :] = h
        return a_re * h + a_im * pltpu.roll(h, half, axis=1) + s_scr[rows, :]

    lax.fori_loop(0, x.shape[0] // nb, step, jnp.zeros((nb, s_scr.shape[1]), F32))
    y_ref[0] += jnp.dot(h_scr[...].astype(BF16), vt_ref[0], preferred_element_type=F32)


def _s5_scan(x3, lam_re, lam_im, log_dt, b_re, b_im, c_re, c_im):
    bsz, seq, dm = x3.shape
    g = dm // S5_GROUP
    q = S5_Q
    nc = seq // q
    m = nc * bsz
    toep, wt, vt, a_re, a_im = _s5_tables(lam_re, lam_im, log_dt, b_re, b_im, c_re, c_im)
    xg = x3.reshape(bsz, nc, q, g, S5_GROUP).transpose(3, 1, 0, 2, 4).reshape(g, m, q * S5_GROUP).astype(BF16)
    w = q * S5_GROUP
    y = pl.pallas_call(
        functools.partial(_s5_kernel, nb=bsz),
        name="s5_scan",
        grid=(g,),
        in_specs=[pl.BlockSpec((1, m, w), lambda i: (i, 0, 0)),
                  pl.BlockSpec((1, w, w), lambda i: (i, 0, 0)),
                  pl.BlockSpec((1, w, 2 * S5_STATE), lambda i: (i, 0, 0)),
                  pl.BlockSpec((1, 2 * S5_STATE, w), lambda i: (i, 0, 0)),
                  pl.BlockSpec((1, 1, 2 * S5_STATE), lambda i: (i, 0, 0)),
                  pl.BlockSpec((1, 1, 2 * S5_STATE), lambda i: (i, 0, 0))],
        out_specs=pl.BlockSpec((1, m, w), lambda i: (i, 0, 0)),
        out_shape=jax.ShapeDtypeStruct((g, m, w), F32),
        scratch_shapes=[pltpu.VMEM((m, 2 * S5_STATE), F32), pltpu.VMEM((m, 2 * S5_STATE), F32)],
        compiler_params=_cparams("parallel"),
    )(xg, toep, wt, vt, a_re, a_im)
    return y.reshape(g, nc, bsz, q, S5_GROUP).transpose(2, 1, 3, 0, 4).reshape(bsz, seq, dm)


def _s5_out_kernel(y_ref, x_ref, d_ref, w_ref, bias_ref, g_ref, b_ref, o_ref):
    x = x_ref[...]
    y = _gelu(y_ref[...] + d_ref[...] * x)
    z = jnp.dot(y.astype(BF16), w_ref[...], preferred_element_type=F32) + bias_ref[...]
    out = z[:, :D_MODEL] * _sigmoid(z[:, D_MODEL:])
    o_ref[...] = _layer_norm_rows(ALPHA * x + out, g_ref[...], b_ref[...])


def s5_layer(x3, lam_re, lam_im, log_dt, b_re, b_im, c_re, c_im, d_skip, w_glu, b_glu, g, b, tm=512):
    bsz, seq, dm = x3.shape
    t = bsz * seq
    tm = min(tm, t)
    y = _s5_scan(x3, lam_re, lam_im, log_dt, b_re, b_im, c_re, c_im).reshape(t, dm)
    row = pl.BlockSpec((tm, dm), lambda i: (i, 0))
    return pl.pallas_call(
        _s5_out_kernel,
        name="s5_out",
        grid=(t // tm,),
        in_specs=[row, row, _full((1, dm)), _full((dm, 2 * dm)), _full((1, 2 * dm)), _full((1, dm)), _full((1, dm))],
        out_specs=row,
        out_shape=jax.ShapeDtypeStruct((t, dm), F32),
        compiler_params=_cparams("parallel"),
    )(y, x3.reshape(t, dm), d_skip.reshape(1, -1), w_glu.astype(BF16), b_glu.reshape(1, -1),
      g.reshape(1, -1), b.reshape(1, -1))


def _diff_attn_kernel(qi_ref, kj_ref, sc_ref, q_ref, k_ref, v_ref, rel_ref, g_ref, o_ref, m_scr, l_scr, acc_scr):
    hh = pl.program_id(1)
    pair = pl.program_id(2)
    qi = qi_ref[pair]
    kj = kj_ref[pair]
    tq = q_ref.shape[0]
    slope = sc_ref[0, hh]

    @pl.when(kj == 0)
    def _():
        m_scr[...] = jnp.full(m_scr.shape, NEG_INF, F32)
        l_scr[...] = jnp.zeros(l_scr.shape, F32)
        acc_scr[...] = jnp.zeros(acc_scr.shape, F32)

    def accumulate(bias_fn, shift):
        k = k_ref[...]
        v = v_ref[...]
        rb = min(DA_ROWS, tq)
        lane = lax.broadcasted_iota(jnp.int32, (rb, q_ref.shape[1]), 1)
        old = [(m_scr[j], l_scr[j], acc_scr[j]) for j in range(2)]
        new = [([], [], []) for _ in range(2)]
        blocks = [slice(blk * rb, (blk + 1) * rb) for blk in range(tq // rb)]
        scores = {}
        for rows in blocks:
            q = q_ref[rows, :]
            for j in range(2):
                qj = jnp.where((lane // DA_HEAD) == j, q, jnp.zeros_like(q)) * jnp.asarray(DA_HEAD ** -0.5, q.dtype)
                scores[rows.start, j] = lax.dot_general(qj, k, (((1,), (1,)), ((), ())), preferred_element_type=F32)
        width = m_scr.shape[2]
        for rows in blocks:
            for j in range(2):
                s = bias_fn(scores[rows.start, j], rows)
                m_old = old[j][0][rows, :]
                m_new = jnp.maximum(m_old, jnp.broadcast_to(jnp.max(s, axis=-1, keepdims=True), (rb, width)) + shift)
                p = jnp.exp(s - jnp.concatenate([m_new - shift] * (s.shape[1] // width), axis=1))
                corr = jnp.exp(m_old - m_new)
                new[j][0].append(m_new)
                new[j][1].append(corr * old[j][1][rows, :]
                                 + jnp.broadcast_to(jnp.sum(p, axis=-1, keepdims=True), (rb, width)))
                new[j][2].append(corr * old[j][2][rows, :] + jnp.dot(p.astype(BF16), v, preferred_element_type=F32))
        for j in range(2):
            m_scr[j] = jnp.concatenate(new[j][0], axis=0)
            l_scr[j] = jnp.concatenate(new[j][1], axis=0)
            acc_scr[j] = jnp.concatenate(new[j][2], axis=0)

    @pl.when(kj < qi)
    def _():
        accumulate(lambda s, rows: s + rel_ref[rows, :] * (-slope), -slope * ((qi - kj) * tq).astype(F32))

    @pl.when(kj == qi)
    def _():
        def masked(s, rows):
            row = lax.broadcasted_iota(jnp.int32, s.shape, 0) + rows.start
            col = lax.broadcasted_iota(jnp.int32, s.shape, 1)
            allowed = (col // CHUNK) <= (row // CHUNK)
            return jnp.where(allowed, s + jnp.abs(rel_ref[rows, :]) * (-slope), NEG_INF)

        accumulate(masked, 0.0)
        lam_full = sc_ref[1, 0]
        out_scale = sc_ref[1, 1]
        o = acc_scr[0] / l_scr[0] - lam_full * (acc_scr[1] / l_scr[1])
        o = o * lax.rsqrt(jnp.mean(o * o, axis=-1, keepdims=True) + LN_EPS)
        o_ref[...] = (o * g_ref[...] * out_scale).astype(o_ref.dtype)


def _diff_attention(qkv, scalars, subln_g, bsz, seq):
    t = bsz * seq
    tq = min(DA_TILE, seq)
    nq = seq // tq
    n_h = DA_HEADS
    pairs = [(i, j) for i in range(nq) for j in range(i + 1)]
    qi_of = jnp.array([i for i, _ in pairs], jnp.int32)
    kj_of = jnp.array([j for _, j in pairs], jnp.int32)
    pos = jnp.arange(tq, dtype=F32)
    rel = pos[:, None] - pos[None, :]
    grid_spec = pltpu.PrefetchScalarGridSpec(
        num_scalar_prefetch=2,
        grid=(bsz, n_h, len(pairs)),
        in_specs=[pl.BlockSpec(memory_space=pltpu.SMEM),
                  pl.BlockSpec((tq, 2 * DA_HEAD), lambda b, h, p, qi, kj: (b * nq + qi[p], h)),
                  pl.BlockSpec((tq, 2 * DA_HEAD), lambda b, h, p, qi, kj: (b * nq + kj[p], n_h + h)),
                  pl.BlockSpec((tq, 2 * DA_HEAD), lambda b, h, p, qi, kj: (b * nq + kj[p], 2 * n_h + h)),
                  pl.BlockSpec((tq, tq), lambda b, h, p, qi, kj: (0, 0)),
                  pl.BlockSpec((1, 2 * DA_HEAD), lambda b, h, p, qi, kj: (0, 0))],
        out_specs=pl.BlockSpec((tq, 2 * DA_HEAD), lambda b, h, p, qi, kj: (b * nq + qi[p], h)),
        scratch_shapes=[pltpu.VMEM((2, tq, 2 * DA_HEAD), F32), pltpu.VMEM((2, tq, 2 * DA_HEAD), F32),
                        pltpu.VMEM((2, tq, 2 * DA_HEAD), F32)])
    return pl.pallas_call(
        _diff_attn_kernel,
        name="diff_attention",
        grid_spec=grid_spec,
        out_shape=jax.ShapeDtypeStruct((t, D_MODEL), BF16),
        compiler_params=_cparams("parallel", "parallel", "arbitrary"),
    )(qi_of, kj_of, scalars, qkv, qkv, qkv, rel, subln_g.reshape(1, -1))


def diff_attention_layer(h2, bsz, seq, w_qkv, lam, subln_g, w_o, layer_idx, g, b):
    lam_init = 0.8 - 0.6 * math.exp(-0.3 * layer_idx)
    lf = lam.astype(F32)
    lam_full = jnp.exp(jnp.sum(lf[0] * lf[1])) - jnp.exp(jnp.sum(lf[2] * lf[3])) + lam_init
    slopes = 2.0 ** (-8.0 * jnp.arange(1, DA_HEADS + 1, dtype=F32) / DA_HEADS)
    scalars = jnp.stack([slopes, jnp.zeros((DA_HEADS,), F32).at[0].set(lam_full).at[1].set(1.0 - lam_init)])
    (qkv,) = _linear(h2, [w_qkv.astype(BF16)], [BF16])
    o = _diff_attention(qkv, scalars, subln_g, bsz, seq)
    return _linear_res_ln(o, w_o.astype(BF16), h2, g, b)


def _mamba_kernel(z_ref, xbc_ref, dt_ref, h_ref, cw_ref, cb_ref, dtb_ref, a_ref, d_ref, ng_ref, wo_ref,
                  lg_ref, lb_ref, o_ref, state_scr, halo_scr, y_scr):
    q = z_ref.shape[0]
    n = M2_STATE

    @pl.when(pl.program_id(1) == 0)
    def _():
        state_scr[...] = jnp.zeros(state_scr.shape, F32)
        halo_scr[...] = jnp.zeros(halo_scr.shape, F32)

    xbc_raw = xbc_ref[...]
    xbc = _silu(_causal_conv(xbc_raw, halo_scr[...], cw_ref[...], cb_ref[...]))
    halo_scr[...] = xbc_raw[q - 8:, :]
    row = lax.broadcasted_iota(jnp.int32, (q, q), 0)
    col = lax.broadcasted_iota(jnp.int32, (q, q), 1)
    lower = row >= col
    tri = lower.astype(F32)
    dt = _softplus(dt_ref[...] + dtb_ref[...])
    a_cs = _cumsum_rows(dt * a_ref[...], tri)
    a_cs_t = a_cs.T
    a_last = a_cs[q - 1:q, :]
    e_cs = jnp.exp(a_cs)
    e_dec = jnp.exp(a_last - a_cs)
    e_last = jnp.exp(a_last)
    lane = lax.broadcasted_iota(jnp.int32, (q, 2 * M2_HEADDIM), 1)
    first = lane < M2_HEADDIM
    srow = lax.broadcasted_iota(jnp.int32, (2 * M2_HEADDIM, n), 0)
    heads_per_group = M2_HEADS // M2_GROUPS
    for grp in range(M2_GROUPS):
        bm = xbc[:, M2_INNER + grp * n:M2_INNER + (grp + 1) * n]
        cm = xbc[:, M2_INNER + M2_GROUPS * n + grp * n:M2_INNER + M2_GROUPS * n + (grp + 1) * n]
        cb = _bdot_nt(cm, bm)
        for pair in range(heads_per_group // 2):
            h0 = grp * heads_per_group + 2 * pair
            cols = slice(h0 * M2_HEADDIM, (h0 + 2) * M2_HEADDIM)
            xs = xbc[:, cols]
            dtp = jnp.where(first, dt[:, h0:h0 + 1], dt[:, h0 + 1:h0 + 2])
            xdt = xs * dtp
            y = jnp.zeros((q, 2 * M2_HEADDIM), F32)
            for s in range(2):
                hd = h0 + s
                seg = a_cs[:, hd:hd + 1] - a_cs_t[hd:hd + 1, :]
                lmat = jnp.exp(jnp.where(lower, seg, NEG_INF))
                part = jnp.where(first if s == 0 else jnp.logical_not(first), xdt, 0.0)
                y = y + _bdot(cb * lmat, part)
            state = state_scr[pl.ds(h0 * M2_HEADDIM, 2 * M2_HEADDIM), :]
            y_off = _bdot_nt(cm, state)
            y = y + y_off * jnp.where(first, e_cs[:, h0:h0 + 1], e_cs[:, h0 + 1:h0 + 2])
            dec = jnp.where(first, e_dec[:, h0:h0 + 1], e_dec[:, h0 + 1:h0 + 2])
            grow = jnp.where(srow < M2_HEADDIM, e_last[:, h0:h0 + 1], e_last[:, h0 + 1:h0 + 2])
            state_scr[pl.ds(h0 * M2_HEADDIM, 2 * M2_HEADDIM), :] = state * grow + _bdot_tn(xdt * dec, bm)
            y_scr[:, cols] = y + d_ref[:, cols] * xs
    gate = _silu(z_ref[...])
    y = y_scr[...] * gate
    gw = M2_INNER // M2_GROUPS
    parts = []
    for grp in range(M2_GROUPS):
        yg = y[:, grp * gw:(grp + 1) * gw]
        parts.append(yg * lax.rsqrt(jnp.mean(yg * yg, axis=-1, keepdims=True) + LN_EPS))
    yn = jnp.concatenate(parts, axis=1) * ng_ref[...]
    out = jnp.dot(yn.astype(BF16), wo_ref[...], preferred_element_type=F32)
    o_ref[...] = _layer_norm_rows(ALPHA * h_ref[...] + out, lg_ref[...], lb_ref[...])


def mamba2_layer(h2, bsz, seq, w_in, conv_w, conv_b, dt_bias, a_log, d_skip, norm_g, w_out, g, b):
    t = bsz * seq
    q = min(M2_Q, seq)
    nc = seq // q
    xbc_w = M2_INNER + 2 * M2_GROUPS * M2_STATE
    wb = w_in.astype(BF16)
    hpad = 128 - M2_HEADS
    pad_heads = lambda a: jnp.pad(a.astype(F32).reshape(1, -1), ((0, 0), (0, hpad)))
    z, xbc, dt = _linear(h2, [wb[:, :M2_INNER], wb[:, M2_INNER:M2_INNER + xbc_w],
                              jnp.pad(wb[:, M2_INNER + xbc_w:], ((0, 0), (0, hpad)))],
                         [F32, F32, F32], tm=256)
    d_cols = jnp.repeat(d_skip.astype(F32), M2_HEADDIM).reshape(1, -1)
    rowspec = lambda w: pl.BlockSpec((q, w), lambda i, j: (i * nc + j, 0))
    return pl.pallas_call(
        _mamba_kernel,
        name="mamba2",
        grid=(bsz, nc),
        in_specs=[rowspec(M2_INNER), rowspec(xbc_w), rowspec(128), rowspec(D_MODEL),
                  _full((M2_CONV, xbc_w)), _full((1, xbc_w)), _full((1, 128)), _full((1, 128)),
                  _full((1, M2_INNER)), _full((1, M2_INNER)), _full((M2_INNER, D_MODEL)),
                  _full((1, D_MODEL)), _full((1, D_MODEL))],
        out_specs=rowspec(D_MODEL),
        out_shape=jax.ShapeDtypeStruct((t, D_MODEL), F32),
        scratch_shapes=[pltpu.VMEM((M2_HEADS * M2_HEADDIM, M2_STATE), F32),
                        pltpu.VMEM((8, xbc_w), F32),
                        pltpu.VMEM((q, M2_INNER), F32)],
        compiler_params=_cparams("parallel", "arbitrary"),
    )(z, xbc, dt, h2, conv_w.astype(F32), conv_b.reshape(1, -1), pad_heads(dt_bias),
      pad_heads(-jnp.exp(a_log.astype(F32))), d_cols, norm_g.reshape(1, -1), w_out.astype(BF16),
      g.reshape(1, -1), b.reshape(1, -1))


def _mlstm_proj_kernel(xm_ref, cw_ref, cb_ref, wq_ref, wk_ref, wv_ref, wg_ref, bg_ref,
                       xc_ref, q_ref, k_ref, v_ref, gate_ref, halo_scr):
    tq = xm_ref.shape[0]

    @pl.when(pl.program_id(1) == 0)
    def _():
        halo_scr[...] = jnp.zeros(halo_scr.shape, F32)

    xm = xm_ref[...]
    xc = _silu(_causal_conv(xm, halo_scr[...], cw_ref[...], cb_ref[...]))
    halo_scr[...] = xm[tq - 8:, :]
    xc_ref[...] = xc
    gates = jnp.zeros((tq, bg_ref.shape[1]), F32) + bg_ref[...]
    for hd in range(ML_HEADS):
        cols = slice(hd * ML_HEADDIM, (hd + 1) * ML_HEADDIM)
        qh = _bdot(xc[:, cols], wq_ref[hd])
        kh = _bdot(xc[:, cols], wk_ref[hd]) * ML_HEADDIM ** -0.5
        vh = _bdot(xm[:, cols], wv_ref[hd])
        q_ref[:, cols] = qh.astype(BF16)
        k_ref[:, cols] = kh.astype(BF16)
        v_ref[:, cols] = vh.astype(BF16)
        gates = gates + _bdot(qh, wg_ref[0, cols, :]) + _bdot(kh, wg_ref[1, cols, :]) + _bdot(vh, wg_ref[2, cols, :])
    gate_ref[...] = gates


def _mlstm_cell_kernel(q_ref, k_ref, v_ref, gate_ref, h_ref, c_scr, n_scr, m_scr):
    hd = pl.program_id(1)
    qn = q_ref.shape[0]

    @pl.when(pl.program_id(2) == 0)
    def _():
        c_scr[...] = jnp.zeros(c_scr.shape, F32)
        n_scr[...] = jnp.zeros(n_scr.shape, F32)
        m_scr[...] = jnp.zeros(m_scr.shape, F32)

    q = q_ref[...]
    k = k_ref[...]
    v = v_ref[...]
    gates = gate_ref[...]
    glane = lax.broadcasted_iota(jnp.int32, gates.shape, 1)
    gsub = lax.broadcasted_iota(jnp.int32, (gates.shape[1], qn), 0)
    row = lax.broadcasted_iota(jnp.int32, (qn, qn), 0)
    col = lax.broadcasted_iota(jnp.int32, (qn, qn), 1)
    lower = row >= col
    logf = jnp.minimum(gates, 0.0) - jnp.log(1.0 + jnp.exp(-jnp.abs(gates)))
    cum = _cumsum_rows(logf, lower.astype(F32))
    ii = jnp.sum(jnp.where(glane == hd, gates, 0.0), axis=-1, keepdims=True)
    bcs = jnp.sum(jnp.where(glane == ML_HEADS + hd, cum, 0.0), axis=-1, keepdims=True)
    ii_row = jnp.sum(jnp.where(gsub == hd, gates.T, 0.0), axis=0, keepdims=True)
    bcs_row = jnp.sum(jnp.where(gsub == ML_HEADS + hd, cum.T, 0.0), axis=0, keepdims=True)
    m_prev = m_scr[0:1, 0:1]
    dmat = jnp.where(lower, bcs - bcs_row + ii_row, NEG_INF)
    inter = bcs + m_prev
    m_row = jnp.maximum(jnp.max(dmat, axis=-1, keepdims=True), inter)
    s = _bdot_nt(q, k) * jnp.exp(dmat - m_row)
    w_inter = jnp.exp(inter - m_row)
    num = _bdot(s, v) + w_inter * _bdot(q, c_scr[...])
    den = jnp.sum(s, axis=-1, keepdims=True) + w_inter * jnp.sum(q.astype(F32) * n_scr[0:1, :], axis=-1, keepdims=True)
    h_ref[...] = num / jnp.maximum(jnp.abs(den), jnp.exp(-m_row))
    b_last = bcs[qn - 1:qn, :]
    gdec = b_last - bcs + ii
    m_new = jnp.maximum(b_last + m_prev, jnp.max(gdec, axis=0, keepdims=True))
    wk = jnp.exp(gdec - m_new)
    decay = jnp.exp(b_last + m_prev - m_new)
    kw = k.astype(F32) * wk
    c_scr[...] = decay * c_scr[...] + _bdot_tn(kw, v)
    n_scr[...] = decay * n_scr[...] + jnp.sum(kw, axis=0, keepdims=True)
    m_scr[...] = jnp.zeros(m_scr.shape, F32) + m_new


def _mlstm_out_kernel(hc_ref, xc_ref, og_ref, h_ref, ng_ref, sk_ref, wd_ref, lg_ref, lb_ref, o_ref):
    hc = hc_ref[...]
    parts = []
    for hd in range(ML_HEADS):
        x = hc[:, hd * ML_HEADDIM:(hd + 1) * ML_HEADDIM]
        mu = jnp.mean(x, axis=-1, keepdims=True)
        c = x - mu
        parts.append(c * lax.rsqrt(jnp.mean(c * c, axis=-1, keepdims=True) + LN_EPS))
    hn = jnp.concatenate(parts, axis=1) * ng_ref[...] + sk_ref[...] * xc_ref[...]
    out = hn * _sigmoid(og_ref[...])
    y = jnp.dot(out.astype(BF16), wd_ref[...], preferred_element_type=F32)
    o_ref[...] = _layer_norm_rows(ALPHA * h_ref[...] + y, lg_ref[...], lb_ref[...])


def mlstm_layer(h2, bsz, seq, w_in, conv_w, conv_b, w_q, w_k, w_v, w_gates, b_gates, norm_g, skip, w_down, g, b):
    t = bsz * seq
    wb = w_in.astype(BF16)
    xm, og = _linear(h2, [wb[:, :ML_INNER], wb[:, ML_INNER:]], [F32, F32])
    tq = min(256, seq)
    nt = seq // tq
    rows = lambda w: pl.BlockSpec((tq, w), lambda i, j: (i * nt + j, 0))
    gpad = 128 - 2 * ML_HEADS
    wg = jnp.pad(w_gates.astype(BF16), ((0, 0), (0, 0), (0, gpad)))
    bg = jnp.pad(b_gates.astype(F32).reshape(1, -1), ((0, 0), (0, gpad)))
    xc, q, k, v, gates = pl.pallas_call(
        _mlstm_proj_kernel,
        name="mlstm_proj",
        grid=(bsz, nt),
        in_specs=[rows(ML_INNER), _full((ML_CONV, ML_INNER)), _full((1, ML_INNER)),
                  _full(w_q.shape), _full(w_k.shape), _full(w_v.shape), _full(wg.shape), _full(bg.shape)],
        out_specs=[rows(ML_INNER), rows(ML_INNER), rows(ML_INNER), rows(ML_INNER), rows(128)],
        out_shape=[jax.ShapeDtypeStruct((t, ML_INNER), F32)] + [jax.ShapeDtypeStruct((t, ML_INNER), BF16)] * 3
        + [jax.ShapeDtypeStruct((t, 128), F32)],
        scratch_shapes=[pltpu.VMEM((8, ML_INNER), F32)],
        compiler_params=_cparams("parallel", "arbitrary"),
    )(xm, conv_w.astype(F32), conv_b.reshape(1, -1), w_q.astype(BF16), w_k.astype(BF16), w_v.astype(BF16), wg, bg)
    qn = min(ML_Q, seq)
    nc = seq // qn
    head = pl.BlockSpec((qn, ML_HEADDIM), lambda i, hd, j: (i * nc + j, hd))
    hc = pl.pallas_call(
        _mlstm_cell_kernel,
        name="mlstm_cell",
        grid=(bsz, ML_HEADS, nc),
        in_specs=[head, head, head, pl.BlockSpec((qn, 128), lambda i, hd, j: (i * nc + j, 0))],
        out_specs=head,
        out_shape=jax.ShapeDtypeStruct((t, ML_INNER), F32),
        scratch_shapes=[pltpu.VMEM((ML_HEADDIM, ML_HEADDIM), F32), pltpu.VMEM((8, ML_HEADDIM), F32),
                        pltpu.VMEM((8, 128), F32)],
        compiler_params=_cparams("parallel", "parallel", "arbitrary"),
    )(q, k, v, gates)
    tm = min(512, t)
    row = lambda w: pl.BlockSpec((tm, w), lambda i: (i, 0))
    return pl.pallas_call(
        _mlstm_out_kernel,
        name="mlstm_out",
        grid=(t // tm,),
        in_specs=[row(ML_INNER), row(ML_INNER), row(ML_INNER), row(D_MODEL), _full((1, ML_INNER)),
                  _full((1, ML_INNER)), _full((ML_INNER, D_MODEL)), _full((1, D_MODEL)), _full((1, D_MODEL))],
        out_specs=row(D_MODEL),
        out_shape=jax.ShapeDtypeStruct((t, D_MODEL), F32),
        compiler_params=_cparams("parallel"),
    )(hc, xc, og, h2, norm_g.reshape(1, -1), skip.reshape(1, -1), w_down.astype(BF16),
      g.reshape(1, -1), b.reshape(1, -1))


def _cross_attn_kernel(h_ref, k_ref, v_ref, wq_ref, wo_ref, g_ref, b_ref, o_ref):
    h = h_ref[...]
    q = jnp.dot(h.astype(BF16), wq_ref[...], preferred_element_type=F32)
    parts = []
    for hd in range(XA_HEADS):
        cols = slice(hd * XA_HEADDIM, (hd + 1) * XA_HEADDIM)
        s = _bdot_nt(q[:, cols], k_ref[:, cols]) * XA_HEADDIM ** -0.5
        p = jnp.exp(s - jnp.max(s, axis=-1, keepdims=True))
        p = p / jnp.sum(p, axis=-1, keepdims=True)
        parts.append(_bdot(p, v_ref[:, cols]))
    o = jnp.concatenate(parts, axis=1)
    y = jnp.dot(o.astype(BF16), wo_ref[...], preferred_element_type=F32)
    o_ref[...] = _layer_norm_rows(ALPHA * h + y, g_ref[...], b_ref[...])


def cross_attention_layer(h2, mem2, bsz, seq, w_q, w_kv, w_o, g, b, tq=512):
    t = bsz * seq
    mlen = mem2.shape[0] // bsz
    tq = min(tq, seq)
    nq = seq // tq
    (kv,) = _linear(mem2, [w_kv.astype(BF16)], [BF16], tm=256)
    return pl.pallas_call(
        _cross_attn_kernel,
        name="cross_attention",
        grid=(bsz, nq),
        in_specs=[pl.BlockSpec((tq, D_MODEL), lambda i, j: (i * nq + j, 0)),
                  pl.BlockSpec((mlen, D_MODEL), lambda i, j: (i, 0)),
                  pl.BlockSpec((mlen, D_MODEL), lambda i, j: (i, 1)),
                  _full((D_MODEL, D_MODEL)), _full((D_MODEL, D_MODEL)), _full((1, D_MODEL)), _full((1, D_MODEL))],
        out_specs=pl.BlockSpec((tq, D_MODEL), lambda i, j: (i * nq + j, 0)),
        out_shape=jax.ShapeDtypeStruct((t, D_MODEL), F32),
        compiler_params=_cparams("parallel", "parallel"),
    )(h2, kv, kv, w_q.astype(BF16), w_o.astype(BF16), g.reshape(1, -1), b.reshape(1, -1))


_PK_PIECES = (('a', 0, 0), ('a', 0, 8), ('a', 1, 0), ('a', 2, 0), ('a', 3, 0),
              ('b', 0, 8), ('b', 0, 0), ('b', 1, 0), ('b', 2, 0))
_PK_INVALID = 1 << 20


def _peer_piece_ids():
    ids = []
    seen = set()
    for kind, fixed, off in _PK_PIECES:
        for i in range(8):
            a, b = (fixed, off + i) if kind == 'a' else (off + i, fixed)
            ok = (a + 1) * (b + 1) <= PK_TOPK and (a, b) not in seen
            seen.add((a, b))
            ids.append(a * PK_TOPK + b if ok else _PK_INVALID)
    assert sum(i != _PK_INVALID for i in ids) == sum((a + 1) * (b + 1) <= PK_TOPK
                                                     for a in range(PK_TOPK) for b in range(PK_TOPK))
    return jnp.array(ids, jnp.int32).reshape(-1, 1)


def _peer_select_kernel(h_ref, wq_ref, keys_ref, flat_ref, eidx_ref, gate_ref, q_scr, e_scr):
    tm = h_ref.shape[0]
    q_scr[...] = jnp.dot(h_ref[...].astype(BF16), wq_ref[...], preferred_element_type=F32).astype(BF16)
    key_id = lax.broadcasted_iota(jnp.int32, (PK_NKEYS, tm), 0)
    rank_id = lax.broadcasted_iota(jnp.int32, (PK_TOPK, tm), 0)
    flat = jnp.broadcast_to(flat_ref[...], (flat_ref.shape[0], tm))
    neg = jnp.float32(-jnp.inf)
    half = PK_QDIM // 2
    zero = jnp.zeros((PK_TOPK, tm), F32)

    n_col = PK_NKEYS // 8
    depth = PK_STACK_DEPTH

    def sort_columns(s):
        vals = [s[8 * v:8 * v + 8, :] for v in range(n_col)]
        keys = [key_id[8 * v:8 * v + 8, :] for v in range(n_col)]
        for base in range(0, n_col, depth):
            for rnd in range(depth):
                for i in range(base + rnd % 2, base + depth - 1, 2):
                    swap = vals[i + 1] > vals[i]
                    vals[i], vals[i + 1] = jnp.where(swap, vals[i + 1], vals[i]), jnp.where(swap, vals[i], vals[i + 1])
                    keys[i], keys[i + 1] = jnp.where(swap, keys[i + 1], keys[i]), jnp.where(swap, keys[i], keys[i + 1])
        return vals + keys

    def top1(a, st):
        vals, keys, top_v, top_i = list(st[:n_col]), list(st[n_col:2 * n_col]), st[2 * n_col], st[2 * n_col + 1]
        heads = range(0, n_col, depth)
        best = functools.reduce(jnp.maximum, [vals[h] for h in heads])
        m = jnp.max(best, axis=0, keepdims=True)
        first_key = functools.reduce(jnp.minimum, [jnp.where(vals[h] == m, keys[h], PK_NKEYS) for h in heads])
        idx = jnp.min(first_key, axis=0, keepdims=True)
        for h in heads:
            win = keys[h] == idx
            for k in range(h, h + depth - 1):
                vals[k] = jnp.where(win, vals[k + 1], vals[k])
                keys[k] = jnp.where(win, keys[k + 1], keys[k])
            vals[h + depth - 1] = jnp.where(win, neg, vals[h + depth - 1])
        sel = rank_id == a
        return vals + keys + [jnp.where(sel, m, top_v), jnp.where(sel, idx.astype(F32), top_i)]

    def candidates(vals0, idxs0, vals1, idxs1):
        rep = lambda x, r: jnp.broadcast_to(x[r:r + 1, :], (8, tm))
        v, c = [], []
        for kind, fixed, off in _PK_PIECES:
            if kind == 'a':
                v.append(rep(vals0, fixed) + vals1[off:off + 8, :])
                c.append(rep(idxs0, fixed) * PK_NKEYS + idxs1[off:off + 8, :])
            else:
                v.append(vals0[off:off + 8, :] + rep(vals1, fixed))
                c.append(idxs0[off:off + 8, :] * PK_NKEYS + rep(idxs1, fixed))
        cand = jnp.where(flat == _PK_INVALID, neg, jnp.concatenate(v, axis=0))
        return cand, jnp.concatenate(c, axis=0)

    def top2(kk, cand, cidx, cv, ce):
        m = jnp.max(cand, axis=0, keepdims=True)
        pos = jnp.min(jnp.where(cand == m, flat, _PK_INVALID), axis=0, keepdims=True)
        hit = flat == pos
        e = jnp.max(jnp.where(hit, cidx, -1.0), axis=0, keepdims=True)
        sel = rank_id == kk
        return jnp.where(hit, neg, cand), jnp.where(sel, m, cv), jnp.where(sel, e, ce)

    tops = None
    for stage in range(PK_HEADS + 1):
        first = stage < PK_HEADS
        second = stage > 0
        init = []
        n_st = 2 * n_col + 2
        if first:
            for j in range(2):
                cols = slice(stage * PK_QDIM + j * half, stage * PK_QDIM + (j + 1) * half)
                init += sort_columns(_bdot_nt(keys_ref[j], q_scr[:, cols])) + [zero, zero]
        if second:
            cand0, cidx = candidates(*tops)
            init += [cand0, zero, zero]

        def body(a, st, first=first, second=second, cidx=cidx if second else None):
            st = list(st)
            out = []
            if first:
                out += top1(a, st[0:n_st]) + top1(a, st[n_st:2 * n_st])
                st = st[2 * n_st:]
            if second:
                out += top2(a, st[0], cidx, st[1], st[2])
            return tuple(out)

        res = lax.fori_loop(0, PK_TOPK // 2, lambda i, st, body=body: body(2 * i + 1, body(2 * i, st)), tuple(init))
        if second:
            cv, ce = res[-2], res[-1]
            p = jnp.exp(cv - cv[0:1, :])
            rows = slice((stage - 1) * PK_TOPK, stage * PK_TOPK)
            gate_ref[rows, :] = p / jnp.sum(p, axis=0, keepdims=True)
            e_scr[rows, :] = ce * ROW_WORDS
        if first:
            tops = (res[n_st - 2], res[n_st - 1], res[2 * n_st - 2], res[2 * n_st - 1])
    eidx_ref[...] = e_scr[...].T.astype(jnp.int32)


def _peer_select(h2, wq_bf, keys_bf, tm=128):
    t = h2.shape[0]
    tm = min(tm, t)
    flat = _peer_piece_ids()
    return pl.pallas_call(
        _peer_select_kernel,
        name="peer_select",
        grid=(t // tm,),
        in_specs=[pl.BlockSpec((tm, D_MODEL), lambda i: (i, 0)), _full(wq_bf.shape), _full(keys_bf.shape),
                  _full(flat.shape)],
        out_specs=[pl.BlockSpec((tm, PK_SLOTS), lambda i: (i, 0)),
                   pl.BlockSpec((PK_SLOTS, tm), lambda i: (0, i))],
        out_shape=[jax.ShapeDtypeStruct((t, PK_SLOTS), jnp.int32),
                   jax.ShapeDtypeStruct((PK_SLOTS, t), F32)],
        scratch_shapes=[pltpu.VMEM((tm, PK_HEADS * PK_QDIM), BF16), pltpu.VMEM((PK_SLOTS, tm), F32)],
        compiler_params=_cparams("parallel"),
    )(h2, wq_bf, keys_bf, flat)


def _pack_rows(w):
    e, d = w.shape
    wb = lax.bitcast_convert_type(w.astype(BF16), jnp.uint16).astype(jnp.uint32)
    packed = (wb[:, d // 2:] << 16) | wb[:, :d // 2]
    return lax.bitcast_convert_type(packed, jnp.int32).reshape(e * ROW_WORDS, 128)


def _unpack_lo(w):
    return pltpu.bitcast(w << 16, F32)


def _unpack_hi(w):
    return pltpu.bitcast(w & jnp.int32(-65536), F32)


def _gather_group(idx_ref, t0, tb, tab_ref, tiles_ref, first_tile):
    for u in range(PK_GROUP):
        row_ref = idx_ref.at[jnp.minimum(t0 + u, tb - 1)]
        for r in range(PK_SLOTS):
            i = pl.multiple_of(row_ref[r], ROW_WORDS)
            tiles_ref[first_tile + u, pl.ds(r * ROW_WORDS, ROW_WORDS), :] = tab_ref[pl.ds(i, ROW_WORDS), :]


def _group_pipeline(tb, idx_ref, tab_ref, tiles_ref, dense_group):
    _gather_group(idx_ref, 0, tb, tab_ref, tiles_ref, 0)

    def trip(i, carry):
        t0 = 2 * PK_GROUP * i
        for half in range(2):
            base = pl.multiple_of(t0 + half * PK_GROUP, PK_GROUP)
            _gather_group(idx_ref, base + PK_GROUP, tb, tab_ref, tiles_ref, (1 - half) * PK_GROUP)
            dense_group(base, half * PK_GROUP)
        return carry

    lax.fori_loop(0, tb // (2 * PK_GROUP), trip, 0)


def _lane_block(base):
    return pl.multiple_of((base // 128) * 128, 128)


def _peer_score_kernel(idx_ref, x_ref, gate_ref, tab_ref, act_ref, tiles_ref):
    tb = x_ref.shape[0]
    lane = lax.broadcasted_iota(jnp.int32, (PK_SLOTS, 128), 1)
    half = D_MODEL // 2

    def dense_group(base, first_tile):
        xg = x_ref[pl.ds(base, PK_GROUP), :]
        blk = _lane_block(base)
        cols = act_ref[:, pl.ds(blk, 128)]
        for u in range(PK_GROUP):
            acc = jnp.zeros((PK_SLOTS, 128), F32)
            for s in range(ROW_WORDS):
                w = tiles_ref[first_tile + u, pl.ds(s, PK_SLOTS, stride=ROW_WORDS), :]
                acc = (acc + _unpack_lo(w) * xg[u:u + 1, s * 128:(s + 1) * 128]
                       + _unpack_hi(w) * xg[u:u + 1, half + s * 128:half + (s + 1) * 128])
            cols = jnp.where(lane == base + u - blk, jnp.sum(acc, axis=-1, keepdims=True), cols)
        act_ref[:, pl.ds(blk, 128)] = cols

    act_ref[...] = jnp.zeros(act_ref.shape, F32)
    _group_pipeline(tb, idx_ref, tab_ref, tiles_ref, dense_group)
    act_ref[...] = _gelu(act_ref[...]) * gate_ref[...]


def _peer_score(eidx, x2, gate, tab, tb=512):
    t = x2.shape[0]
    tb = min(tb, t)
    slots = pl.BlockSpec((PK_SLOTS, tb), lambda i: (0, i))
    return pl.pallas_call(
        _peer_score_kernel,
        name="peer_score",
        grid=(t // tb,),
        in_specs=[pl.BlockSpec((tb, PK_SLOTS), lambda i: (i, 0), memory_space=pltpu.SMEM),
                  pl.BlockSpec((tb, D_MODEL), lambda i: (i, 0)), slots, _full(tab.shape)],
        out_specs=slots,
        out_shape=jax.ShapeDtypeStruct((PK_SLOTS, t), F32),
        scratch_shapes=[pltpu.VMEM((2 * PK_GROUP, PK_SLOTS * ROW_WORDS, 128), jnp.int32)],
        compiler_params=_cparams("parallel"),
    )(eidx, x2, gate, tab)


def _peer_combine_kernel(idx_ref, act_ref, h_ref, g_ref, b_ref, tab_ref, out_ref, y_ref, tiles_ref):
    tb = h_ref.shape[0]
    lane = lax.broadcasted_iota(jnp.int32, (PK_SLOTS, 128), 1)

    def dense_group(base, first_tile):
        blk = _lane_block(base)
        act = act_ref[:, pl.ds(blk, 128)]
        rows = []
        for u in range(PK_GROUP):
            a = jnp.sum(jnp.where(lane == base + u - blk, act, 0.0), axis=-1, keepdims=True)
            lo, hi = [], []
            for s in range(ROW_WORDS):
                w = tiles_ref[first_tile + u, pl.ds(s, PK_SLOTS, stride=ROW_WORDS), :]
                lo.append(jnp.sum(a * _unpack_lo(w), axis=0, keepdims=True))
                hi.append(jnp.sum(a * _unpack_hi(w), axis=0, keepdims=True))
            rows.append(jnp.concatenate(lo + hi, axis=1))
        y_ref[pl.ds(base, PK_GROUP), :] = jnp.concatenate(rows, axis=0)

    _group_pipeline(tb, idx_ref, tab_ref, tiles_ref, dense_group)
    out_ref[...] = _layer_norm_rows(ALPHA * h_ref[...] + y_ref[...], g_ref[...], b_ref[...])


def _peer_combine(eidx, act, h2, g, b, tab, tb=512):
    t = h2.shape[0]
    tb = min(tb, t)
    tok = pl.BlockSpec((tb, D_MODEL), lambda i: (i, 0))
    return pl.pallas_call(
        _peer_combine_kernel,
        name="peer_combine",
        grid=(t // tb,),
        in_specs=[pl.BlockSpec((tb, PK_SLOTS), lambda i: (i, 0), memory_space=pltpu.SMEM),
                  pl.BlockSpec((PK_SLOTS, tb), lambda i: (0, i)),
                  tok, _full((1, D_MODEL)), _full((1, D_MODEL)), _full(tab.shape)],
        out_specs=tok,
        out_shape=jax.ShapeDtypeStruct((t, D_MODEL), F32),
        scratch_shapes=[pltpu.VMEM((tb, D_MODEL), F32),
                        pltpu.VMEM((2 * PK_GROUP, PK_SLOTS * ROW_WORDS, 128), jnp.int32)],
        compiler_params=_cparams("parallel"),
    )(eidx, act, h2, g.reshape(1, -1), b.reshape(1, -1), tab)


def peer_layer(h2, w_query, sub_keys, u, v, g, b):
    eidx, gate = _peer_select(h2, w_query.astype(BF16), sub_keys.astype(BF16))
    act = _peer_score(eidx, h2, gate, _pack_rows(u))
    return _peer_combine(eidx, act, h2, g, b, _pack_rows(v))


def kernel(x, mem, s5_lam_re, s5_lam_im, s5_log_dt, s5_b_re, s5_b_im, s5_c_re, s5_c_im, s5_d, s5_w_glu, s5_b_glu, da_w_qkv, da_lambda, da_subln_g, da_w_o, m2_w_in, m2_conv_w, m2_conv_b, m2_dt_bias, m2_a_log, m2_d, m2_norm_g, m2_w_out, ml_w_in, ml_conv_w, ml_conv_b, ml_w_q, ml_w_k, ml_w_v, ml_w_gates, ml_b_gates, ml_norm_g, ml_skip, ml_w_down, xa_w_q, xa_w_kv, xa_w_o, pk_w_query, pk_sub_keys, pk_u, pk_v, ln_g, ln_b):
    bsz, seq, dm = x.shape
    t = bsz * seq
    mem2 = mem.reshape(-1, dm)
    h = x.reshape(t, dm)
    for i in range(DEPTH):
        kind, j = i % 4, i // 4
        g0, b0 = ln_g[i, 0], ln_b[i, 0]
        if kind == 0:
            h = s5_layer(h.reshape(bsz, seq, dm), s5_lam_re[j], s5_lam_im[j], s5_log_dt[j], s5_b_re[j], s5_b_im[j],
                         s5_c_re[j], s5_c_im[j], s5_d[j], s5_w_glu[j], s5_b_glu[j], g0, b0)
        elif kind == 1:
            h = diff_attention_layer(h, bsz, seq, da_w_qkv[j], da_lambda[j], da_subln_g[j], da_w_o[j], i, g0, b0)
        elif kind == 2:
            h = mamba2_layer(h, bsz, seq, m2_w_in[j], m2_conv_w[j], m2_conv_b[j], m2_dt_bias[j], m2_a_log[j],
                             m2_d[j], m2_norm_g[j], m2_w_out[j], g0, b0)
        else:
            h = mlstm_layer(h, bsz, seq, ml_w_in[j], ml_conv_w[j], ml_conv_b[j], ml_w_q[j], ml_w_k[j], ml_w_v[j],
                            ml_w_gates[j], ml_b_gates[j], ml_norm_g[j], ml_skip[j], ml_w_down[j], g0, b0)
        h = cross_attention_layer(h, mem2, bsz, seq, xa_w_q[i], xa_w_kv[i], xa_w_o[i], ln_g[i, 1], ln_b[i, 1])
        h = peer_layer(h, pk_w_query[i], pk_sub_keys[i], pk_u[i], pk_v[i], ln_g[i, 2], ln_b[i, 2])
    return h.reshape(bsz, seq, dm)
```

```python
import functools
import math

import jax
import jax.numpy as jnp
from jax import lax
from jax.experimental import pallas as pl
from jax.experimental.pallas import tpu as pltpu

F32 = jnp.float32
BF16 = jnp.bfloat16
HIGHEST = lax.Precision.HIGHEST

D_MODEL = 1024
DEPTH = 4
ALPHA = (2 * DEPTH) ** 0.25
LN_EPS = 1e-5
NEG_INF = -1e30
CHUNK = 64

S5_GROUP = 16
S5_STATE = 64
S5_Q = 64

DA_HEAD = 64
DA_HEADS = 8
DA_TILE = 1024
DA_ROWS = 1024

M2_INNER = 2048
M2_HEADS = 32
M2_HEADDIM = 64
M2_GROUPS = 4
M2_STATE = 128
M2_CONV = 4
M2_Q = 256

ML_INNER = 2048
ML_HEADS = 4
ML_HEADDIM = 512
ML_CONV = 4
ML_Q = 256

XA_HEADS = 4
XA_HEADDIM = 256

PK_HEADS = 8
PK_NKEYS = 128
PK_QDIM = 256
PK_TOPK = 16
PK_SLOTS = PK_HEADS * PK_TOPK
ROW_WORDS = D_MODEL // 2 // 128
PK_IDX_RUN = 8
PK_STACK_DEPTH = 4
PK_GROUP = 8

VMEM_LIMIT = 56 * 1024 * 1024


def _cparams(*sem):
    return pltpu.CompilerParams(dimension_semantics=sem, vmem_limit_bytes=VMEM_LIMIT)


def _gelu(x):
    return 0.5 * x * (1.0 + jnp.tanh(math.sqrt(2.0 / math.pi) * (x + 0.044715 * x * x * x)))


def _sigmoid(x):
    return 1.0 / (1.0 + jnp.exp(-x))


def _silu(x):
    return x * _sigmoid(x)


def _softplus(x):
    return jnp.maximum(x, 0.0) + jnp.log(1.0 + jnp.exp(-jnp.abs(x)))


def _layer_norm_rows(v, g, b):
    mu = jnp.mean(v, axis=-1, keepdims=True)
    c = v - mu
    var = jnp.mean(c * c, axis=-1, keepdims=True)
    return c * lax.rsqrt(var + LN_EPS) * g + b


def _bdot(a, b):
    return jnp.dot(a.astype(BF16), b.astype(BF16), preferred_element_type=F32)


def _bdot_nt(a, b):
    return lax.dot_general(a.astype(BF16), b.astype(BF16), (((1,), (1,)), ((), ())), preferred_element_type=F32)


def _bdot_tn(a, b):
    return lax.dot_general(a.astype(BF16), b.astype(BF16), (((0,), (0,)), ((), ())), preferred_element_type=F32)


def _full(shape):
    return pl.BlockSpec(shape, lambda *_: (0,) * len(shape))


def _causal_conv(x, halo, w, b):
    q = x.shape[0]
    k = w.shape[0]
    ext = jnp.concatenate([halo, x], axis=0)
    out = x * w[k - 1:k, :] + b
    for d in range(1, k):
        out = out + ext[8 - d:8 - d + q, :] * w[k - 1 - d:k - d, :]
    return out


def _cumsum_rows(a, tri):
    return jnp.dot(tri, a, precision=HIGHEST, preferred_element_type=F32)


def _linear_kernel(*refs, n_out):
    a = refs[0][...].astype(BF16)
    for w_ref, o_ref in zip(refs[1:1 + n_out], refs[1 + n_out:]):
        o_ref[...] = jnp.dot(a, w_ref[...], preferred_element_type=F32).astype(o_ref.dtype)


def _linear(a, ws, out_dtypes, tm=512):
    m, k = a.shape
    tm = min(tm, m)
    return pl.pallas_call(
        functools.partial(_linear_kernel, n_out=len(ws)),
        name="linear",
        grid=(m // tm,),
        in_specs=[pl.BlockSpec((tm, k), lambda i: (i, 0))] + [_full(w.shape) for w in ws],
        out_specs=[pl.BlockSpec((tm, w.shape[1]), lambda i: (i, 0)) for w in ws],
        out_shape=[jax.ShapeDtypeStruct((m, w.shape[1]), dt) for w, dt in zip(ws, out_dtypes)],
        compiler_params=_cparams("parallel"),
    )(a, *ws)


def _linear_res_ln_kernel(a_ref, w_ref, h_ref, g_ref, b_ref, o_ref):
    y = jnp.dot(a_ref[...].astype(BF16), w_ref[...], preferred_element_type=F32)
    o_ref[...] = _layer_norm_rows(ALPHA * h_ref[...] + y, g_ref[...], b_ref[...])


def _linear_res_ln(a, w, h, g, b, tm=512):
    m, k = a.shape
    tm = min(tm, m)
    return pl.pallas_call(
        _linear_res_ln_kernel,
        name="linear_res_ln",
        grid=(m // tm,),
        in_specs=[pl.BlockSpec((tm, k), lambda i: (i, 0)), _full(w.shape),
                  pl.BlockSpec((tm, D_MODEL), lambda i: (i, 0)), _full((1, D_MODEL)), _full((1, D_MODEL))],
        out_specs=pl.BlockSpec((tm, D_MODEL), lambda i: (i, 0)),
        out_shape=jax.ShapeDtypeStruct((m, D_MODEL), F32),
        compiler_params=_cparams("parallel"),
    )(a, w, h, g.reshape(1, -1), b.reshape(1, -1))


def _s5_tables(lam_re, lam_im, log_dt, b_re, b_im, c_re, c_im):
    q = S5_Q
    g, p = lam_re.shape
    lam = lax.complex(lam_re.astype(F32), lam_im.astype(F32))
    dt = jnp.exp(log_dt.astype(F32))[:, None]
    lam_bar = jnp.exp(lam * dt)
    b_bar = ((lam_bar - 1.0) / lam)[..., None] * lax.complex(b_re.astype(F32), b_im.astype(F32))
    c_mat = lax.complex(c_re.astype(F32), c_im.astype(F32))
    pw = jnp.cumprod(jnp.concatenate([jnp.ones((1, g, p), lam_bar.dtype),
                                      jnp.broadcast_to(lam_bar, (q, g, p))], axis=0), axis=0)
    cp = c_mat[None] * pw[:q, :, None, :]
    taps = (jnp.einsum('tgop,gpi->gtoi', cp.real, b_bar.real, precision=HIGHEST)
            - jnp.einsum('tgop,gpi->gtoi', cp.imag, b_bar.imag, precision=HIGHEST))
    pos = jnp.arange(q)
    tau = pos[None, :] - pos[:, None]
    toep = jnp.where((tau >= 0)[None, :, :, None, None], taps[:, jnp.maximum(tau, 0)], 0.0)
    toep = toep.transpose(0, 1, 4, 2, 3).reshape(g, q * S5_GROUP, q * S5_GROUP)
    wb = pw[q - 1 - pos][:, :, :, None] * b_bar[None]
    wt = jnp.concatenate([wb.real, wb.imag], axis=2).transpose(1, 0, 3, 2).reshape(g, q * S5_GROUP, 2 * p)
    cv = c_mat[None] * pw[1:q + 1, :, None, :]
    vt = jnp.concatenate([cv.real, -cv.imag], axis=3).transpose(1, 3, 0, 2).reshape(g, 2 * p, q * S5_GROUP)
    aq = pw[q]
    a_re = jnp.concatenate([aq.real, aq.real], axis=1).reshape(g, 1, 2 * p)
    a_im = jnp.concatenate([-aq.imag, aq.imag], axis=1).reshape(g, 1, 2 * p)
    return toep.astype(BF16), wt.astype(BF16), vt.astype(BF16), a_re, a_im


def _s5_kernel(x_ref, tt_ref, wt_ref, vt_ref, ar_ref, ai_ref, y_ref, s_scr, h_scr, *, nb):
    x = x_ref[0]
    y_ref[0] = jnp.dot(x, tt_ref[0], preferred_element_type=F32)
    s_scr[...] = jnp.dot(x, wt_ref[0], preferred_element_type=F32)
    a_re = ar_ref[0]
    a_im = ai_ref[0]
    half = s_scr.shape[1] // 2

    def step(c, h):
        rows = pl.ds(pl.multiple_of(c * nb, nb), nb)
        h_scr[rows, :] = h
        return a_re * h + a_im * pltpu.roll(h, half, axis=1) + s_scr[rows, :]

    lax.fori_loop(0, x.shape[0] // nb, step, jnp.zeros((nb, s_scr.shape[1]), F32))
    y_ref[0] += jnp.dot(h_scr[...].astype(BF16), vt_ref[0], preferred_element_type=F32)


def _s5_scan(x3, lam_re, lam_im, log_dt, b_re, b_im, c_re, c_im):
    bsz, seq, dm = x3.shape
    g = dm // S5_GROUP
    q = S5_Q
    nc = seq // q
    m = nc * bsz
    toep, wt, vt, a_re, a_im = _s5_tables(lam_re, lam_im, log_dt, b_re, b_im, c_re, c_im)
    xg = x3.reshape(bsz, nc, q, g, S5_GROUP).transpose(3, 1, 0, 2, 4).reshape(g, m, q * S5_GROUP).astype(BF16)
    w = q * S5_GROUP
    y = pl.pallas_call(
        functools.partial(_s5_kernel, nb=bsz),
        name="s5_scan",
        grid=(g,),
        in_specs=[pl.BlockSpec((1, m, w), lambda i: (i, 0, 0)),
                  pl.BlockSpec((1, w, w), lambda i: (i, 0, 0)),
                  pl.BlockSpec((1, w, 2 * S5_STATE), lambda i: (i, 0, 0)),
                  pl.BlockSpec((1, 2 * S5_STATE, w), lambda i: (i, 0, 0)),
                  pl.BlockSpec((1, 1, 2 * S5_STATE), lambda i: (i, 0, 0)),
                  pl.BlockSpec((1, 1, 2 * S5_STATE), lambda i: (i, 0, 0))],
        out_specs=pl.BlockSpec((1, m, w), lambda i: (i, 0, 0)),
        out_shape=jax.ShapeDtypeStruct((g, m, w), F32),
        scratch_shapes=[pltpu.VMEM((m, 2 * S5_STATE), F32), pltpu.VMEM((m, 2 * S5_STATE), F32)],
        compiler_params=_cparams("parallel"),
    )(xg, toep, wt, vt, a_re, a_im)
    return y.reshape(g, nc, bsz, q, S5_GROUP).transpose(2, 1, 3, 0, 4).reshape(bsz, seq, dm)


def _s5_out_kernel(y_ref, x_ref, d_ref, w_ref, bias_ref, g_ref, b_ref, o_ref):
    x = x_ref[...]
    y = _gelu(y_ref[...] + d_ref[...] * x)
    z = jnp.dot(y.astype(BF16), w_ref[...], preferred_element_type=F32) + bias_ref[...]
    out = z[:, :D_MODEL] * _sigmoid(z[:, D_MODEL:])
    o_ref[...] = _layer_norm_rows(ALPHA * x + out, g_ref[...], b_ref[...])


def s5_layer(x3, lam_re, lam_im, log_dt, b_re, b_im, c_re, c_im, d_skip, w_glu, b_glu, g, b, tm=512):
    bsz, seq, dm = x3.shape
    t = bsz * seq
    tm = min(tm, t)
    y = _s5_scan(x3, lam_re, lam_im, log_dt, b_re, b_im, c_re, c_im).reshape(t, dm)
    row = pl.BlockSpec((tm, dm), lambda i: (i, 0))
    return pl.pallas_call(
        _s5_out_kernel,
        name="s5_out",
        grid=(t // tm,),
        in_specs=[row, row, _full((1, dm)), _full((dm, 2 * dm)), _full((1, 2 * dm)), _full((1, dm)), _full((1, dm))],
        out_specs=row,
        out_shape=jax.ShapeDtypeStruct((t, dm), F32),
        compiler_params=_cparams("parallel"),
    )(y, x3.reshape(t, dm), d_skip.reshape(1, -1), w_glu.astype(BF16), b_glu.reshape(1, -1),
      g.reshape(1, -1), b.reshape(1, -1))


def _diff_attn_kernel(qi_ref, kj_ref, sc_ref, q_ref, k_ref, v_ref, rel_ref, g_ref, o_ref, m_scr, l_scr, acc_scr):
    hh = pl.program_id(1)
    pair = pl.program_id(2)
    qi = qi_ref[pair]
    kj = kj_ref[pair]
    tq = q_ref.shape[0]
    slope = sc_ref[0, hh]

    @pl.when(kj == 0)
    def _():
        m_scr[...] = jnp.full(m_scr.shape, NEG_INF, F32)
        l_scr[...] = jnp.zeros(l_scr.shape, F32)
        acc_scr[...] = jnp.zeros(acc_scr.shape, F32)

    def accumulate(bias_fn, shift):
        k = k_ref[...]
        v = v_ref[...]
        rb = min(DA_ROWS, tq)
        lane = lax.broadcasted_iota(jnp.int32, (rb, q_ref.shape[1]), 1)
        old = [(m_scr[j], l_scr[j], acc_scr[j]) for j in range(2)]
        new = [([], [], []) for _ in range(2)]
        blocks = [slice(blk * rb, (blk + 1) * rb) for blk in range(tq // rb)]
        scores = {}
        for rows in blocks:
            q = q_ref[rows, :]
            for j in range(2):
                qj = jnp.where((lane // DA_HEAD) == j, q, jnp.zeros_like(q)) * jnp.asarray(DA_HEAD ** -0.5, q.dtype)
                scores[rows.start, j] = lax.dot_general(qj, k, (((1,), (1,)), ((), ())), preferred_element_type=F32)
        width = m_scr.shape[2]
        for rows in blocks:
            for j in range(2):
                s = bias_fn(scores[rows.start, j], rows)
                m_old = old[j][0][rows, :]
                m_new = jnp.maximum(m_old, jnp.broadcast_to(jnp.max(s, axis=-1, keepdims=True), (rb, width)) + shift)
                p = jnp.exp(s - jnp.concatenate([m_new - shift] * (s.shape[1] // width), axis=1))
                corr = jnp.exp(m_old - m_new)
                new[j][0].append(m_new)
                new[j][1].append(corr * old[j][1][rows, :]
                                 + jnp.broadcast_to(jnp.sum(p, axis=-1, keepdims=True), (rb, width)))
                new[j][2].append(corr * old[j][2][rows, :] + jnp.dot(p.astype(BF16), v, preferred_element_type=F32))
        for j in range(2):
            m_scr[j] = jnp.concatenate(new[j][0], axis=0)
            l_scr[j] = jnp.concatenate(new[j][1], axis=0)
            acc_scr[j] = jnp.concatenate(new[j][2], axis=0)

    @pl.when(kj < qi)
    def _():
        accumulate(lambda s, rows: s + rel_ref[rows, :] * (-slope), -slope * ((qi - kj) * tq).astype(F32))

    @pl.when(kj == qi)
    def _():
        def masked(s, rows):
            row = lax.broadcasted_iota(jnp.int32, s.shape, 0) + rows.start
            col = lax.broadcasted_iota(jnp.int32, s.shape, 1)
            allowed = (col // CHUNK) <= (row // CHUNK)
            return jnp.where(allowed, s + jnp.abs(rel_ref[rows, :]) * (-slope), NEG_INF)

        accumulate(masked, 0.0)
        lam_full = sc_ref[1, 0]
        out_scale = sc_ref[1, 1]
        o = acc_scr[0] / l_scr[0] - lam_full * (acc_scr[1] / l_scr[1])
        o = o * lax.rsqrt(jnp.mean(o * o, axis=-1, keepdims=True) + LN_EPS)
        o_ref[...] = (o * g_ref[...] * out_scale).astype(o_ref.dtype)


def _diff_attention(qkv, scalars, subln_g, bsz, seq):
    t = bsz * seq
    tq = min(DA_TILE, seq)
    nq = seq // tq
    n_h = DA_HEADS
    pairs = [(i, j) for i in range(nq) for j in range(i + 1)]
    qi_of = jnp.array([i for i, _ in pairs], jnp.int32)
    kj_of = jnp.array([j for _, j in pairs], jnp.int32)
    pos = jnp.arange(tq, dtype=F32)
    rel = pos[:, None] - pos[None, :]
    grid_spec = pltpu.PrefetchScalarGridSpec(
        num_scalar_prefetch=2,
        grid=(bsz, n_h, len(pairs)),
        in_specs=[pl.BlockSpec(memory_space=pltpu.SMEM),
                  pl.BlockSpec((tq, 2 * DA_HEAD), lambda b, h, p, qi, kj: (b * nq + qi[p], h)),
                  pl.BlockSpec((tq, 2 * DA_HEAD), lambda b, h, p, qi, kj: (b * nq + kj[p], n_h + h)),
                  pl.BlockSpec((tq, 2 * DA_HEAD), lambda b, h, p, qi, kj: (b * nq + kj[p], 2 * n_h + h)),
                  pl.BlockSpec((tq, tq), lambda b, h, p, qi, kj: (0, 0)),
                  pl.BlockSpec((1, 2 * DA_HEAD), lambda b, h, p, qi, kj: (0, 0))],
        out_specs=pl.BlockSpec((tq, 2 * DA_HEAD), lambda b, h, p, qi, kj: (b * nq + qi[p], h)),
        scratch_shapes=[pltpu.VMEM((2, tq, 2 * DA_HEAD), F32), pltpu.VMEM((2, tq, 2 * DA_HEAD), F32),
                        pltpu.VMEM((2, tq, 2 * DA_HEAD), F32)])
    return pl.pallas_call(
        _diff_attn_kernel,
        name="diff_attention",
        grid_spec=grid_spec,
        out_shape=jax.ShapeDtypeStruct((t, D_MODEL), BF16),
        compiler_params=_cparams("parallel", "parallel", "arbitrary"),
    )(qi_of, kj_of, scalars, qkv, qkv, qkv, rel, subln_g.reshape(1, -1))


def diff_attention_layer(h2, bsz, seq, w_qkv, lam, subln_g, w_o, layer_idx, g, b):
    lam_init = 0.8 - 0.6 * math.exp(-0.3 * layer_idx)
    lf = lam.astype(F32)
    lam_full = jnp.exp(jnp.sum(lf[0] * lf[1])) - jnp.exp(jnp.sum(lf[2] * lf[3])) + lam_init
    slopes = 2.0 ** (-8.0 * jnp.arange(1, DA_HEADS + 1, dtype=F32) / DA_HEADS)
    scalars = jnp.stack([slopes, jnp.zeros((DA_HEADS,), F32).at[0].set(lam_full).at[1].set(1.0 - lam_init)])
    (qkv,) = _linear(h2, [w_qkv.astype(BF16)], [BF16])
    o = _diff_attention(qkv, scalars, subln_g, bsz, seq)
    return _linear_res_ln(o, w_o.astype(BF16), h2, g, b)


def _mamba_kernel(z_ref, xbc_ref, dt_ref, h_ref, cw_ref, cb_ref, dtb_ref, a_ref, d_ref, ng_ref, wo_ref,
                  lg_ref, lb_ref, o_ref, state_scr, halo_scr, y_scr):
    q = z_ref.shape[0]
    n = M2_STATE

    @pl.when(pl.program_id(1) == 0)
    def _():
        state_scr[...] = jnp.zeros(state_scr.shape, F32)
        halo_scr[...] = jnp.zeros(halo_scr.shape, F32)

    xbc_raw = xbc_ref[...]
    xbc = _silu(_causal_conv(xbc_raw, halo_scr[...], cw_ref[...], cb_ref[...]))
    halo_scr[...] = xbc_raw[q - 8:, :]
    row = lax.broadcasted_iota(jnp.int32, (q, q), 0)
    col = lax.broadcasted_iota(jnp.int32, (q, q), 1)
    lower = row >= col
    tri = lower.astype(F32)
    dt = _softplus(dt_ref[...] + dtb_ref[...])
    a_cs = _cumsum_rows(dt * a_ref[...], tri)
    a_cs_t = a_cs.T
    a_last = a_cs[q - 1:q, :]
    e_cs = jnp.exp(a_cs)
    e_dec = jnp.exp(a_last - a_cs)
    e_last = jnp.exp(a_last)
    lane = lax.broadcasted_iota(jnp.int32, (q, 2 * M2_HEADDIM), 1)
    first = lane < M2_HEADDIM
    srow = lax.broadcasted_iota(jnp.int32, (2 * M2_HEADDIM, n), 0)
    heads_per_group = M2_HEADS // M2_GROUPS
    for grp in range(M2_GROUPS):
        bm = xbc[:, M2_INNER + grp * n:M2_INNER + (grp + 1) * n]
        cm = xbc[:, M2_INNER + M2_GROUPS * n + grp * n:M2_INNER + M2_GROUPS * n + (grp + 1) * n]
        cb = _bdot_nt(cm, bm)
        for pair in range(heads_per_group // 2):
            h0 = grp * heads_per_group + 2 * pair
            cols = slice(h0 * M2_HEADDIM, (h0 + 2) * M2_HEADDIM)
            xs = xbc[:, cols]
            dtp = jnp.where(first, dt[:, h0:h0 + 1], dt[:, h0 + 1:h0 + 2])
            xdt = xs * dtp
            y = jnp.zeros((q, 2 * M2_HEADDIM), F32)
            for s in range(2):
                hd = h0 + s
                seg = a_cs[:, hd:hd + 1] - a_cs_t[hd:hd + 1, :]
                lmat = jnp.exp(jnp.where(lower, seg, NEG_INF))
                part = jnp.where(first if s == 0 else jnp.logical_not(first), xdt, 0.0)
                y = y + _bdot(cb * lmat, part)
            state = state_scr[pl.ds(h0 * M2_HEADDIM, 2 * M2_HEADDIM), :]
            y_off = _bdot_nt(cm, state)
            y = y + y_off * jnp.where(first, e_cs[:, h0:h0 + 1], e_cs[:, h0 + 1:h0 + 2])
            dec = jnp.where(first, e_dec[:, h0:h0 + 1], e_dec[:, h0 + 1:h0 + 2])
            grow = jnp.where(srow < M2_HEADDIM, e_last[:, h0:h0 + 1], e_last[:, h0 + 1:h0 + 2])
            state_scr[pl.ds(h0 * M2_HEADDIM, 2 * M2_HEADDIM), :] = state * grow + _bdot_tn(xdt * dec, bm)
            y_scr[:, cols] = y + d_ref[:, cols] * xs
    gate = _silu(z_ref[...])
    y = y_scr[...] * gate
    gw = M2_INNER // M2_GROUPS
    parts = []
    for grp in range(M2_GROUPS):
        yg = y[:, grp * gw:(grp + 1) * gw]
        parts.append(yg * lax.rsqrt(jnp.mean(yg * yg, axis=-1, keepdims=True) + LN_EPS))
    yn = jnp.concatenate(parts, axis=1) * ng_ref[...]
    out = jnp.dot(yn.astype(BF16), wo_ref[...], preferred_element_type=F32)
    o_ref[...] = _layer_norm_rows(ALPHA * h_ref[...] + out, lg_ref[...], lb_ref[...])


def mamba2_layer(h2, bsz, seq, w_in, conv_w, conv_b, dt_bias, a_log, d_skip, norm_g, w_out, g, b):
    t = bsz * seq
    q = min(M2_Q, seq)
    nc = seq // q
    xbc_w = M2_INNER + 2 * M2_GROUPS * M2_STATE
    wb = w_in.astype(BF16)
    hpad = 128 - M2_HEADS
    pad_heads = lambda a: jnp.pad(a.astype(F32).reshape(1, -1), ((0, 0), (0, hpad)))
    z, xbc, dt = _linear(h2, [wb[:, :M2_INNER], wb[:, M2_INNER:M2_INNER + xbc_w],
                              jnp.pad(wb[:, M2_INNER + xbc_w:], ((0, 0), (0, hpad)))],
                         [F32, F32, F32], tm=256)
    d_cols = jnp.repeat(d_skip.astype(F32), M2_HEADDIM).reshape(1, -1)
    rowspec = lambda w: pl.BlockSpec((q, w), lambda i, j: (i * nc + j, 0))
    return pl.pallas_call(
        _mamba_kernel,
        name="mamba2",
        grid=(bsz, nc),
        in_specs=[rowspec(M2_INNER), rowspec(xbc_w), rowspec(128), rowspec(D_MODEL),
                  _full((M2_CONV, xbc_w)), _full((1, xbc_w)), _full((1, 128)), _full((1, 128)),
                  _full((1, M2_INNER)), _full((1, M2_INNER)), _full((M2_INNER, D_MODEL)),
                  _full((1, D_MODEL)), _full((1, D_MODEL))],
        out_specs=rowspec(D_MODEL),
        out_shape=jax.ShapeDtypeStruct((t, D_MODEL), F32),
        scratch_shapes=[pltpu.VMEM((M2_HEADS * M2_HEADDIM, M2_STATE), F32),
                        pltpu.VMEM((8, xbc_w), F32),
                        pltpu.VMEM((q, M2_INNER), F32)],
        compiler_params=_cparams("parallel", "arbitrary"),
    )(z, xbc, dt, h2, conv_w.astype(F32), conv_b.reshape(1, -1), pad_heads(dt_bias),
      pad_heads(-jnp.exp(a_log.astype(F32))), d_cols, norm_g.reshape(1, -1), w_out.astype(BF16),
      g.reshape(1, -1), b.reshape(1, -1))


def _mlstm_proj_kernel(xm_ref, cw_ref, cb_ref, wq_ref, wk_ref, wv_ref, wg_ref, bg_ref,
                       xc_ref, q_ref, k_ref, v_ref, gate_ref, halo_scr):
    tq = xm_ref.shape[0]

    @pl.when(pl.program_id(1) == 0)
    def _():
        halo_scr[...] = jnp.zeros(halo_scr.shape, F32)

    xm = xm_ref[...]
    xc = _silu(_causal_conv(xm, halo_scr[...], cw_ref[...], cb_ref[...]))
    halo_scr[...] = xm[tq - 8:, :]
    xc_ref[...] = xc
    gates = jnp.zeros((tq, bg_ref.shape[1]), F32) + bg_ref[...]
    for hd in range(ML_HEADS):
        cols = slice(hd * ML_HEADDIM, (hd + 1) * ML_HEADDIM)
        qh = _bdot(xc[:, cols], wq_ref[hd])
        kh = _bdot(xc[:, cols], wk_ref[hd]) * ML_HEADDIM ** -0.5
        vh = _bdot(xm[:, cols], wv_ref[hd])
        q_ref[:, cols] = qh.astype(BF16)
        k_ref[:, cols] = kh.astype(BF16)
        v_ref[:, cols] = vh.astype(BF16)
        gates = gates + _bdot(qh, wg_ref[0, cols, :]) + _bdot(kh, wg_ref[1, cols, :]) + _bdot(vh, wg_ref[2, cols, :])
    gate_ref[...] = gates


def _mlstm_cell_kernel(q_ref, k_ref, v_ref, gate_ref, h_ref, c_scr, n_scr, m_scr):
    hd = pl.program_id(1)
    qn = q_ref.shape[0]

    @pl.when(pl.program_id(2) == 0)
    def _():
        c_scr[...] = jnp.zeros(c_scr.shape, F32)
        n_scr[...] = jnp.zeros(n_scr.shape, F32)
        m_scr[...] = jnp.zeros(m_scr.shape, F32)

    q = q_ref[...]
    k = k_ref[...]
    v = v_ref[...]
    gates = gate_ref[...]
    glane = lax.broadcasted_iota(jnp.int32, gates.shape, 1)
    gsub = lax.broadcasted_iota(jnp.int32, (gates.shape[1], qn), 0)
    row = lax.broadcasted_iota(jnp.int32, (qn, qn), 0)
    col = lax.broadcasted_iota(jnp.int32, (qn, qn), 1)
    lower = row >= col
    logf = jnp.minimum(gates, 0.0) - jnp.log(1.0 + jnp.exp(-jnp.abs(gates)))
    cum = _cumsum_rows(logf, lower.astype(F32))
    ii = jnp.sum(jnp.where(glane == hd, gates, 0.0), axis=-1, keepdims=True)
    bcs = jnp.sum(jnp.where(glane == ML_HEADS + hd, cum, 0.0), axis=-1, keepdims=True)
    ii_row = jnp.sum(jnp.where(gsub == hd, gates.T, 0.0), axis=0, keepdims=True)
    bcs_row = jnp.sum(jnp.where(gsub == ML_HEADS + hd, cum.T, 0.0), axis=0, keepdims=True)
    m_prev = m_scr[0:1, 0:1]
    dmat = jnp.where(lower, bcs - bcs_row + ii_row, NEG_INF)
    inter = bcs + m_prev
    m_row = jnp.maximum(jnp.max(dmat, axis=-1, keepdims=True), inter)
    s = _bdot_nt(q, k) * jnp.exp(dmat - m_row)
    w_inter = jnp.exp(inter - m_row)
    num = _bdot(s, v) + w_inter * _bdot(q, c_scr[...])
    den = jnp.sum(s, axis=-1, keepdims=True) + w_inter * jnp.sum(q.astype(F32) * n_scr[0:1, :], axis=-1, keepdims=True)
    h_ref[...] = num / jnp.maximum(jnp.abs(den), jnp.exp(-m_row))
    b_last = bcs[qn - 1:qn, :]
    gdec = b_last - bcs + ii
    m_new = jnp.maximum(b_last + m_prev, jnp.max(gdec, axis=0, keepdims=True))
    wk = jnp.exp(gdec - m_new)
    decay = jnp.exp(b_last + m_prev - m_new)
    kw = k.astype(F32) * wk
    c_scr[...] = decay * c_scr[...] + _bdot_tn(kw, v)
    n_scr[...] = decay * n_scr[...] + jnp.sum(kw, axis=0, keepdims=True)
    m_scr[...] = jnp.zeros(m_scr.shape, F32) + m_new


def _mlstm_out_kernel(hc_ref, xc_ref, og_ref, h_ref, ng_ref, sk_ref, wd_ref, lg_ref, lb_ref, o_ref):
    hc = hc_ref[...]
    parts = []
    for hd in range(ML_HEADS):
        x = hc[:, hd * ML_HEADDIM:(hd + 1) * ML_HEADDIM]
        mu = jnp.mean(x, axis=-1, keepdims=True)
        c = x - mu
        parts.append(c * lax.rsqrt(jnp.mean(c * c, axis=-1, keepdims=True) + LN_EPS))
    hn = jnp.concatenate(parts, axis=1) * ng_ref[...] + sk_ref[...] * xc_ref[...]
    out = hn * _sigmoid(og_ref[...])
    y = jnp.dot(out.astype(BF16), wd_ref[...], preferred_element_type=F32)
    o_ref[...] = _layer_norm_rows(ALPHA * h_ref[...] + y, lg_ref[...], lb_ref[...])


def mlstm_layer(h2, bsz, seq, w_in, conv_w, conv_b, w_q, w_k, w_v, w_gates, b_gates, norm_g, skip, w_down, g, b):
    t = bsz * seq
    wb = w_in.astype(BF16)
    xm, og = _linear(h2, [wb[:, :ML_INNER], wb[:, ML_INNER:]], [F32, F32])
    tq = min(256, seq)
    nt = seq // tq
    rows = lambda w: pl.BlockSpec((tq, w), lambda i, j: (i * nt + j, 0))
    gpad = 128 - 2 * ML_HEADS
    wg = jnp.pad(w_gates.astype(BF16), ((0, 0), (0, 0), (0, gpad)))
    bg = jnp.pad(b_gates.astype(F32).reshape(1, -1), ((0, 0), (0, gpad)))
    xc, q, k, v, gates = pl.pallas_call(
        _mlstm_proj_kernel,
        name="mlstm_proj",
        grid=(bsz, nt),
        in_specs=[rows(ML_INNER), _full((ML_CONV, ML_INNER)), _full((1, ML_INNER)),
                  _full(w_q.shape), _full(w_k.shape), _full(w_v.shape), _full(wg.shape), _full(bg.shape)],
        out_specs=[rows(ML_INNER), rows(ML_INNER), rows(ML_INNER), rows(ML_INNER), rows(128)],
        out_shape=[jax.ShapeDtypeStruct((t, ML_INNER), F32)] + [jax.ShapeDtypeStruct((t, ML_INNER), BF16)] * 3
        + [jax.ShapeDtypeStruct((t, 128), F32)],
        scratch_shapes=[pltpu.VMEM((8, ML_INNER), F32)],
        compiler_params=_cparams("parallel", "arbitrary"),
    )(xm, conv_w.astype(F32), conv_b.reshape(1, -1), w_q.astype(BF16), w_k.astype(BF16), w_v.astype(BF16), wg, bg)
    qn = min(ML_Q, seq)
    nc = seq // qn
    head = pl.BlockSpec((qn, ML_HEADDIM), lambda i, hd, j: (i * nc + j, hd))
    hc = pl.pallas_call(
        _mlstm_cell_kernel,
        name="mlstm_cell",
        grid=(bsz, ML_HEADS, nc),
        in_specs=[head, head, head, pl.BlockSpec((qn, 128), lambda i, hd, j: (i * nc + j, 0))],
        out_specs=head,
        out_shape=jax.ShapeDtypeStruct((t, ML_INNER), F32),
        scratch_shapes=[pltpu.VMEM((ML_HEADDIM, ML_HEADDIM), F32), pltpu.VMEM((8, ML_HEADDIM), F32),
                        pltpu.VMEM((8, 128), F32)],
        compiler_params=_cparams("parallel", "parallel", "arbitrary"),
    )(q, k, v, gates)
    tm = min(512, t)
    row = lambda w: pl.BlockSpec((tm, w), lambda i: (i, 0))
    return pl.pallas_call(
        _mlstm_out_kernel,
        name="mlstm_out",
        grid=(t // tm,),
        in_specs=[row(ML_INNER), row(ML_INNER), row(ML_INNER), row(D_MODEL), _full((1, ML_INNER)),
                  _full((1, ML_INNER)), _full((ML_INNER, D_MODEL)), _full((1, D_MODEL)), _full((1, D_MODEL))],
        out_specs=row(D_MODEL),
        out_shape=jax.ShapeDtypeStruct((t, D_MODEL), F32),
        compiler_params=_cparams("parallel"),
    )(hc, xc, og, h2, norm_g.reshape(1, -1), skip.reshape(1, -1), w_down.astype(BF16),
      g.reshape(1, -1), b.reshape(1, -1))


def _cross_attn_kernel(h_ref, k_ref, v_ref, wq_ref, wo_ref, g_ref, b_ref, o_ref):
    h = h_ref[...]
    q = jnp.dot(h.astype(BF16), wq_ref[...], preferred_element_type=F32)
    parts = []
    for hd in range(XA_HEADS):
        cols = slice(hd * XA_HEADDIM, (hd + 1) * XA_HEADDIM)
        s = _bdot_nt(q[:, cols], k_ref[:, cols]) * XA_HEADDIM ** -0.5
        p = jnp.exp(s - jnp.max(s, axis=-1, keepdims=True))
        p = p / jnp.sum(p, axis=-1, keepdims=True)
        parts.append(_bdot(p, v_ref[:, cols]))
    o = jnp.concatenate(parts, axis=1)
    y = jnp.dot(o.astype(BF16), wo_ref[...], preferred_element_type=F32)
    o_ref[...] = _layer_norm_rows(ALPHA * h + y, g_ref[...], b_ref[...])


def cross_attention_layer(h2, mem2, bsz, seq, w_q, w_kv, w_o, g, b, tq=512):
    t = bsz * seq
    mlen = mem2.shape[0] // bsz
    tq = min(tq, seq)
    nq = seq // tq
    (kv,) = _linear(mem2, [w_kv.astype(BF16)], [BF16], tm=256)
    return pl.pallas_call(
        _cross_attn_kernel,
        name="cross_attention",
        grid=(bsz, nq),
        in_specs=[pl.BlockSpec((tq, D_MODEL), lambda i, j: (i * nq + j, 0)),
                  pl.BlockSpec((mlen, D_MODEL), lambda i, j: (i, 0)),
                  pl.BlockSpec((mlen, D_MODEL), lambda i, j: (i, 1)),
                  _full((D_MODEL, D_MODEL)), _full((D_MODEL, D_MODEL)), _full((1, D_MODEL)), _full((1, D_MODEL))],
        out_specs=pl.BlockSpec((tq, D_MODEL), lambda i, j: (i * nq + j, 0)),
        out_shape=jax.ShapeDtypeStruct((t, D_MODEL), F32),
        compiler_params=_cparams("parallel", "parallel"),
    )(h2, kv, kv, w_q.astype(BF16), w_o.astype(BF16), g.reshape(1, -1), b.reshape(1, -1))


_PK_PIECES = (('a', 0, 0), ('a', 0, 8), ('a', 1, 0), ('a', 2, 0), ('a', 3, 0),
              ('b', 0, 8), ('b', 0, 0), ('b', 1, 0), ('b', 2, 0))
_PK_INVALID = 1 << 20


def _peer_piece_ids():
    ids = []
    seen = set()
    for kind, fixed, off in _PK_PIECES:
        for i in range(8):
            a, b = (fixed, off + i) if kind == 'a' else (off + i, fixed)
            ok = (a + 1) * (b + 1) <= PK_TOPK and (a, b) not in seen
            seen.add((a, b))
            ids.append(a * PK_TOPK + b if ok else _PK_INVALID)
    assert sum(i != _PK_INVALID for i in ids) == sum((a + 1) * (b + 1) <= PK_TOPK
                                                     for a in range(PK_TOPK) for b in range(PK_TOPK))
    return jnp.array(ids, jnp.int32).reshape(-1, 1)


def _peer_select_kernel(h_ref, wq_ref, keys_ref, flat_ref, eidx_ref, gate_ref, q_scr, e_scr):
    tm = h_ref.shape[0]
    q_scr[...] = jnp.dot(h_ref[...].astype(BF16), wq_ref[...], preferred_element_type=F32).astype(BF16)
    key_id = lax.broadcasted_iota(jnp.int32, (PK_NKEYS, tm), 0)
    rank_id = lax.broadcasted_iota(jnp.int32, (PK_TOPK, tm), 0)
    flat = jnp.broadcast_to(flat_ref[...], (flat_ref.shape[0], tm))
    neg = jnp.float32(-jnp.inf)
    half = PK_QDIM // 2
    zero = jnp.zeros((PK_TOPK, tm), F32)

    n_col = PK_NKEYS // 8
    depth = PK_STACK_DEPTH

    def sort_columns(s):
        vals = [s[8 * v:8 * v + 8, :] for v in range(n_col)]
        keys = [key_id[8 * v:8 * v + 8, :] for v in range(n_col)]
        for base in range(0, n_col, depth):
            for rnd in range(depth):
                for i in range(base + rnd % 2, base + depth - 1, 2):
                    swap = vals[i + 1] > vals[i]
                    vals[i], vals[i + 1] = jnp.where(swap, vals[i + 1], vals[i]), jnp.where(swap, vals[i], vals[i + 1])
                    keys[i], keys[i + 1] = jnp.where(swap, keys[i + 1], keys[i]), jnp.where(swap, keys[i], keys[i + 1])
        return vals + keys

    def top1(a, st):
        vals, keys, top_v, top_i = list(st[:n_col]), list(st[n_col:2 * n_col]), st[2 * n_col], st[2 * n_col + 1]
        heads = range(0, n_col, depth)
        best = functools.reduce(jnp.maximum, [vals[h] for h in heads])
        m = jnp.max(best, axis=0, keepdims=True)
        first_key = functools.reduce(jnp.minimum, [jnp.where(vals[h] == m, keys[h], PK_NKEYS) for h in heads])
        idx = jnp.min(first_key, axis=0, keepdims=True)
        for h in heads:
            win = keys[h] == idx
            for k in range(h, h + depth - 1):
                vals[k] = jnp.where(win, vals[k + 1], vals[k])
                keys[k] = jnp.where(win, keys[k + 1], keys[k])
            vals[h + depth - 1] = jnp.where(win, neg, vals[h + depth - 1])
        sel = rank_id == a
        return vals + keys + [jnp.where(sel, m, top_v), jnp.where(sel, idx.astype(F32), top_i)]

    def candidates(vals0, idxs0, vals1, idxs1):
        rep = lambda x, r: jnp.broadcast_to(x[r:r + 1, :], (8, tm))
        v, c = [], []
        for kind, fixed, off in _PK_PIECES:
            if kind == 'a':
                v.append(rep(vals0, fixed) + vals1[off:off + 8, :])
                c.append(rep(idxs0, fixed) * PK_NKEYS + idxs1[off:off + 8, :])
            else:
                v.append(vals0[off:off + 8, :] + rep(vals1, fixed))
                c.append(idxs0[off:off + 8, :] * PK_NKEYS + rep(idxs1, fixed))
        cand = jnp.where(flat == _PK_INVALID, neg, jnp.concatenate(v, axis=0))
        return cand, jnp.concatenate(c, axis=0)

    def top2(kk, cand, cidx, cv, ce):
        m = jnp.max(cand, axis=0, keepdims=True)
        pos = jnp.min(jnp.where(cand == m, flat, _PK_INVALID), axis=0, keepdims=True)
        hit = flat == pos
        e = jnp.max(jnp.where(hit, cidx, -1.0), axis=0, keepdims=True)
        sel = rank_id == kk
        return jnp.where(hit, neg, cand), jnp.where(sel, m, cv), jnp.where(sel, e, ce)

    tops = None
    for stage in range(PK_HEADS + 1):
        first = stage < PK_HEADS
        second = stage > 0
        init = []
        n_st = 2 * n_col + 2
        if first:
            for j in range(2):
                cols = slice(stage * PK_QDIM + j * half, stage * PK_QDIM + (j + 1) * half)
                init += sort_columns(_bdot_nt(keys_ref[j], q_scr[:, cols])) + [zero, zero]
        if second:
            cand0, cidx = candidates(*tops)
            init += [cand0, zero, zero]

        def body(a, st, first=first, second=second, cidx=cidx if second else None):
            st = list(st)
            out = []
            if first:
                out += top1(a, st[0:n_st]) + top1(a, st[n_st:2 * n_st])
                st = st[2 * n_st:]
            if second:
                out += top2(a, st[0], cidx, st[1], st[2])
            return tuple(out)

        res = lax.fori_loop(0, PK_TOPK // 2, lambda i, st, body=body: body(2 * i + 1, body(2 * i, st)), tuple(init))
        if second:
            cv, ce = res[-2], res[-1]
            p = jnp.exp(cv - cv[0:1, :])
            rows = slice((stage - 1) * PK_TOPK, stage * PK_TOPK)
            gate_ref[rows, :] = p / jnp.sum(p, axis=0, keepdims=True)
            e_scr[rows, :] = ce * ROW_WORDS
        if first:
            tops = (res[n_st - 2], res[n_st - 1], res[2 * n_st - 2], res[2 * n_st - 1])
    eidx_ref[...] = e_scr[...].T.astype(jnp.int32)


def _peer_select(h2, wq_bf, keys_bf, tm=128):
    t = h2.shape[0]
    tm = min(tm, t)
    flat = _peer_piece_ids()
    return pl.pallas_call(
        _peer_select_kernel,
        name="peer_select",
        grid=(t // tm,),
        in_specs=[pl.BlockSpec((tm, D_MODEL), lambda i: (i, 0)), _full(wq_bf.shape), _full(keys_bf.shape),
                  _full(flat.shape)],
        out_specs=[pl.BlockSpec((tm, PK_SLOTS), lambda i: (i, 0)),
                   pl.BlockSpec((PK_SLOTS, tm), lambda i: (0, i))],
        out_shape=[jax.ShapeDtypeStruct((t, PK_SLOTS), jnp.int32),
                   jax.ShapeDtypeStruct((PK_SLOTS, t), F32)],
        scratch_shapes=[pltpu.VMEM((tm, PK_HEADS * PK_QDIM), BF16), pltpu.VMEM((PK_SLOTS, tm), F32)],
        compiler_params=_cparams("parallel"),
    )(h2, wq_bf, keys_bf, flat)


def _pack_rows(w):
    e, d = w.shape
    wb = lax.bitcast_convert_type(w.astype(BF16), jnp.uint16).astype(jnp.uint32)
    packed = (wb[:, d // 2:] << 16) | wb[:, :d // 2]
    return lax.bitcast_convert_type(packed, jnp.int32).reshape(e * ROW_WORDS, 128)


def _unpack_lo(w):
    return pltpu.bitcast(w << 16, F32)


def _unpack_hi(w):
    return pltpu.bitcast(w & jnp.int32(-65536), F32)


def _gather_group(idx_ref, t0, tb, tab_ref, tiles_ref, first_tile):
    for u in range(PK_GROUP):
        start = jnp.minimum(t0 + u, tb - 1) * PK_SLOTS
        for r0 in range(0, PK_SLOTS, PK_IDX_RUN):
            run_ref = idx_ref.at[pl.ds(start + r0, PK_IDX_RUN)]
            for j in range(PK_IDX_RUN):
                i = pl.multiple_of(run_ref[j], ROW_WORDS)
                tiles_ref[first_tile + u, pl.ds((r0 + j) * ROW_WORDS, ROW_WORDS), :] = tab_ref[pl.ds(i, ROW_WORDS), :]


def _group_pipeline(tb, idx_ref, tab_ref, tiles_ref, dense_group):
    _gather_group(idx_ref, 0, tb, tab_ref, tiles_ref, 0)

    def trip(i, carry):
        t0 = 2 * PK_GROUP * i
        for half in range(2):
            base = pl.multiple_of(t0 + half * PK_GROUP, PK_GROUP)
            _gather_group(idx_ref, base + PK_GROUP, tb, tab_ref, tiles_ref, (1 - half) * PK_GROUP)
            dense_group(base, half * PK_GROUP)
        return carry

    lax.fori_loop(0, tb // (2 * PK_GROUP), trip, 0)


def _lane_block(base):
    return pl.multiple_of((base // 128) * 128, 128)


def _peer_score_kernel(idx_ref, x_ref, gate_ref, tab_ref, act_ref, tiles_ref):
    tb = x_ref.shape[0]
    lane = lax.broadcasted_iota(jnp.int32, (PK_SLOTS, 128), 1)
    half = D_MODEL // 2

    def dense_group(base, first_tile):
        xg = x_ref[pl.ds(base, PK_GROUP), :]
        blk = _lane_block(base)
        cols = act_ref[:, pl.ds(blk, 128)]
        for u in range(PK_GROUP):
            acc = jnp.zeros((PK_SLOTS, 128), F32)
            for s in range(ROW_WORDS):
                w = tiles_ref[first_tile + u, pl.ds(s, PK_SLOTS, stride=ROW_WORDS), :]
                acc = (acc + _unpack_lo(w) * xg[u:u + 1, s * 128:(s + 1) * 128]
                       + _unpack_hi(w) * xg[u:u + 1, half + s * 128:half + (s + 1) * 128])
            cols = jnp.where(lane == base + u - blk, jnp.sum(acc, axis=-1, keepdims=True), cols)
        act_ref[:, pl.ds(blk, 128)] = cols

    act_ref[...] = jnp.zeros(act_ref.shape, F32)
    _group_pipeline(tb, idx_ref, tab_ref, tiles_ref, dense_group)
    act_ref[...] = _gelu(act_ref[...]) * gate_ref[...]


def _peer_score(eidx, x2, gate, tab, tb=512):
    t = x2.shape[0]
    tb = min(tb, t)
    slots = pl.BlockSpec((PK_SLOTS, tb), lambda i: (0, i))
    return pl.pallas_call(
        _peer_score_kernel,
        name="peer_score",
        grid=(t // tb,),
        in_specs=[pl.BlockSpec((tb * PK_SLOTS,), lambda i: (i,), memory_space=pltpu.SMEM),
                  pl.BlockSpec((tb, D_MODEL), lambda i: (i, 0)), slots, _full(tab.shape)],
        out_specs=slots,
        out_shape=jax.ShapeDtypeStruct((PK_SLOTS, t), F32),
        scratch_shapes=[pltpu.VMEM((2 * PK_GROUP, PK_SLOTS * ROW_WORDS, 128), jnp.int32)],
        compiler_params=_cparams("parallel"),
    )(eidx.reshape(-1), x2, gate, tab)


def _peer_combine_kernel(idx_ref, act_ref, h_ref, g_ref, b_ref, tab_ref, out_ref, y_ref, tiles_ref):
    tb = h_ref.shape[0]
    lane = lax.broadcasted_iota(jnp.int32, (PK_SLOTS, 128), 1)

    def dense_group(base, first_tile):
        blk = _lane_block(base)
        act = act_ref[:, pl.ds(blk, 128)]
        rows = []
        for u in range(PK_GROUP):
            a = jnp.sum(jnp.where(lane == base + u - blk, act, 0.0), axis=-1, keepdims=True)
            lo, hi = [], []
            for s in range(ROW_WORDS):
                w = tiles_ref[first_tile + u, pl.ds(s, PK_SLOTS, stride=ROW_WORDS), :]
                lo.append(jnp.sum(a * _unpack_lo(w), axis=0, keepdims=True))
                hi.append(jnp.sum(a * _unpack_hi(w), axis=0, keepdims=True))
            rows.append(jnp.concatenate(lo + hi, axis=1))
        y_ref[pl.ds(base, PK_GROUP), :] = jnp.concatenate(rows, axis=0)

    _group_pipeline(tb, idx_ref, tab_ref, tiles_ref, dense_group)
    out_ref[...] = _layer_norm_rows(ALPHA * h_ref[...] + y_ref[...], g_ref[...], b_ref[...])


def _peer_combine(eidx, act, h2, g, b, tab, tb=512):
    t = h2.shape[0]
    tb = min(tb, t)
    tok = pl.BlockSpec((tb, D_MODEL), lambda i: (i, 0))
    return pl.pallas_call(
        _peer_combine_kernel,
        name="peer_combine",
        grid=(t // tb,),
        in_specs=[pl.BlockSpec((tb * PK_SLOTS,), lambda i: (i,), memory_space=pltpu.SMEM),
                  pl.BlockSpec((PK_SLOTS, tb), lambda i: (0, i)),
                  tok, _full((1, D_MODEL)), _full((1, D_MODEL)), _full(tab.shape)],
        out_specs=tok,
        out_shape=jax.ShapeDtypeStruct((t, D_MODEL), F32),
        scratch_shapes=[pltpu.VMEM((tb, D_MODEL), F32),
                        pltpu.VMEM((2 * PK_GROUP, PK_SLOTS * ROW_WORDS, 128), jnp.int32)],
        compiler_params=_cparams("parallel"),
    )(eidx.reshape(-1), act, h2, g.reshape(1, -1), b.reshape(1, -1), tab)


def peer_layer(h2, w_query, sub_keys, u, v, g, b):
    eidx, gate = _peer_select(h2, w_query.astype(BF16), sub_keys.astype(BF16))
    act = _peer_score(eidx, h2, gate, _pack_rows(u))
    return _peer_combine(eidx, act, h2, g, b, _pack_rows(v))


def kernel(x, mem, s5_lam_re, s5_lam_im, s5_log_dt, s5_b_re, s5_b_im, s5_c_re, s5_c_im, s5_d, s5_w_glu, s5_b_glu, da_w_qkv, da_lambda, da_subln_g, da_w_o, m2_w_in, m2_conv_w, m2_conv_b, m2_dt_bias, m2_a_log, m2_d, m2_norm_g, m2_w_out, ml_w_in, ml_conv_w, ml_conv_b, ml_w_q, ml_w_k, ml_w_v, ml_w_gates, ml_b_gates, ml_norm_g, ml_skip, ml_w_down, xa_w_q, xa_w_kv, xa_w_o, pk_w_query, pk_sub_keys, pk_u, pk_v, ln_g, ln_b):
    bsz, seq, dm = x.shape
    t = bsz * seq
    mem2 = mem.reshape(-1, dm)
    h = x.reshape(t, dm)
    for i in range(DEPTH):
        kind, j = i % 4, i // 4
        g0, b0 = ln_g[i, 0], ln_b[i, 0]
        if kind == 0:
            h = s5_layer(h.reshape(bsz, seq, dm), s5_lam_re[j], s5_lam_im[j], s5_log_dt[j], s5_b_re[j], s5_b_im[j],
                         s5_c_re[j], s5_c_im[j], s5_d[j], s5_w_glu[j], s5_b_glu[j], g0, b0)
        elif kind == 1:
            h = diff_attention_layer(h, bsz, seq, da_w_qkv[j], da_lambda[j], da_subln_g[j], da_w_o[j], i, g0, b0)
        elif kind == 2:
            h = mamba2_layer(h, bsz, seq, m2_w_in[j], m2_conv_w[j], m2_conv_b[j], m2_dt_bias[j], m2_a_log[j],
                             m2_d[j], m2_norm_g[j], m2_w_out[j], g0, b0)
        else:
            h = mlstm_layer(h, bsz, seq, ml_w_in[j], ml_conv_w[j], ml_conv_b[j], ml_w_q[j], ml_w_k[j], ml_w_v[j],
                            ml_w_gates[j], ml_b_gates[j], ml_norm_g[j], ml_skip[j], ml_w_down[j], g0, b0)
        h = cross_attention_layer(h, mem2, bsz, seq, xa_w_q[i], xa_w_kv[i], xa_w_o[i], ln_g[i, 1], ln_b[i, 1])
        h = peer_layer(h, pk_w_query[i], pk_sub_keys[i], pk_u[i], pk_v[i], ln_g[i, 2], ln_b[i, 2])
    return h.reshape(bsz, seq, dm)
```

```python
import functools
import math

import jax
import jax.numpy as jnp
from jax import lax
from jax.experimental import pallas as pl
from jax.experimental.pallas import tpu as pltpu

F32 = jnp.float32
BF16 = jnp.bfloat16
HIGHEST = lax.Precision.HIGHEST

D_MODEL = 1024
DEPTH = 4
ALPHA = (2 * DEPTH) ** 0.25
LN_EPS = 1e-5
NEG_INF = -1e30
CHUNK = 64

S5_GROUP = 16
S5_STATE = 64
S5_Q = 64

DA_HEAD = 64
DA_HEADS = 8
DA_TILE = 1024
DA_ROWS = 1024

M2_INNER = 2048
M2_HEADS = 32
M2_HEADDIM = 64
M2_GROUPS = 4
M2_STATE = 128
M2_CONV = 4
M2_Q = 256

ML_INNER = 2048
ML_HEADS = 4
ML_HEADDIM = 512
ML_CONV = 4
ML_Q = 256

XA_HEADS = 4
XA_HEADDIM = 256

PK_HEADS = 8
PK_NKEYS = 128
PK_QDIM = 256
PK_TOPK = 16
PK_SLOTS = PK_HEADS * PK_TOPK
ROW_WORDS = D_MODEL // 2 // 128
PK_IDX_RUN = 8
PK_STACK_DEPTH = 4
PK_GROUP = 8

VMEM_LIMIT = 56 * 1024 * 1024


def _cparams(*sem):
    return pltpu.CompilerParams(dimension_semantics=sem, vmem_limit_bytes=VMEM_LIMIT)


def _gelu(x):
    return 0.5 * x * (1.0 + jnp.tanh(math.sqrt(2.0 / math.pi) * (x + 0.044715 * x * x * x)))


def _sigmoid(x):
    return 1.0 / (1.0 + jnp.exp(-x))


def _silu(x):
    return x * _sigmoid(x)


def _softplus(x):
    return jnp.maximum(x, 0.0) + jnp.log(1.0 + jnp.exp(-jnp.abs(x)))


def _layer_norm_rows(v, g, b):
    mu = jnp.mean(v, axis=-1, keepdims=True)
    c = v - mu
    var = jnp.mean(c * c, axis=-1, keepdims=True)
    return c * lax.rsqrt(var + LN_EPS) * g + b


def _bdot(a, b):
    return jnp.dot(a.astype(BF16), b.astype(BF16), preferred_element_type=F32)


def _bdot_nt(a, b):
    return lax.dot_general(a.astype(BF16), b.astype(BF16), (((1,), (1,)), ((), ())), preferred_element_type=F32)


def _bdot_tn(a, b):
    return lax.dot_general(a.astype(BF16), b.astype(BF16), (((0,), (0,)), ((), ())), preferred_element_type=F32)


def _full(shape):
    return pl.BlockSpec(shape, lambda *_: (0,) * len(shape))


def _causal_conv(x, halo, w, b):
    q = x.shape[0]
    k = w.shape[0]
    ext = jnp.concatenate([halo, x], axis=0)
    out = x * w[k - 1:k, :] + b
    for d in range(1, k):
        out = out + ext[8 - d:8 - d + q, :] * w[k - 1 - d:k - d, :]
    return out


def _cumsum_rows(a, tri):
    return jnp.dot(tri, a, precision=HIGHEST, preferred_element_type=F32)


def _linear_kernel(*refs, n_out):
    a = refs[0][...].astype(BF16)
    for w_ref, o_ref in zip(refs[1:1 + n_out], refs[1 + n_out:]):
        o_ref[...] = jnp.dot(a, w_ref[...], preferred_element_type=F32).astype(o_ref.dtype)


def _linear(a, ws, out_dtypes, tm=512):
    m, k = a.shape
    tm = min(tm, m)
    return pl.pallas_call(
        functools.partial(_linear_kernel, n_out=len(ws)),
        name="linear",
        grid=(m // tm,),
        in_specs=[pl.BlockSpec((tm, k), lambda i: (i, 0))] + [_full(w.shape) for w in ws],
        out_specs=[pl.BlockSpec((tm, w.shape[1]), lambda i: (i, 0)) for w in ws],
        out_shape=[jax.ShapeDtypeStruct((m, w.shape[1]), dt) for w, dt in zip(ws, out_dtypes)],
        compiler_params=_cparams("parallel"),
    )(a, *ws)


def _linear_res_ln_kernel(a_ref, w_ref, h_ref, g_ref, b_ref, o_ref):
    y = jnp.dot(a_ref[...].astype(BF16), w_ref[...], preferred_element_type=F32)
    o_ref[...] = _layer_norm_rows(ALPHA * h_ref[...] + y, g_ref[...], b_ref[...])


def _linear_res_ln(a, w, h, g, b, tm=512):
    m, k = a.shape
    tm = min(tm, m)
    return pl.pallas_call(
        _linear_res_ln_kernel,
        name="linear_res_ln",
        grid=(m // tm,),
        in_specs=[pl.BlockSpec((tm, k), lambda i: (i, 0)), _full(w.shape),
                  pl.BlockSpec((tm, D_MODEL), lambda i: (i, 0)), _full((1, D_MODEL)), _full((1, D_MODEL))],
        out_specs=pl.BlockSpec((tm, D_MODEL), lambda i: (i, 0)),
        out_shape=jax.ShapeDtypeStruct((m, D_MODEL), F32),
        compiler_params=_cparams("parallel"),
    )(a, w, h, g.reshape(1, -1), b.reshape(1, -1))


def _s5_tables(lam_re, lam_im, log_dt, b_re, b_im, c_re, c_im):
    q = S5_Q
    g, p = lam_re.shape
    lam = lax.complex(lam_re.astype(F32), lam_im.astype(F32))
    dt = jnp.exp(log_dt.astype(F32))[:, None]
    lam_bar = jnp.exp(lam * dt)
    b_bar = ((lam_bar - 1.0) / lam)[..., None] * lax.complex(b_re.astype(F32), b_im.astype(F32))
    c_mat = lax.complex(c_re.astype(F32), c_im.astype(F32))
    pw = jnp.cumprod(jnp.concatenate([jnp.ones((1, g, p), lam_bar.dtype),
                                      jnp.broadcast_to(lam_bar, (q, g, p))], axis=0), axis=0)
    cp = c_mat[None] * pw[:q, :, None, :]
    taps = (jnp.einsum('tgop,gpi->gtoi', cp.real, b_bar.real, precision=HIGHEST)
            - jnp.einsum('tgop,gpi->gtoi', cp.imag, b_bar.imag, precision=HIGHEST))
    pos = jnp.arange(q)
    tau = pos[None, :] - pos[:, None]
    toep = jnp.where((tau >= 0)[None, :, :, None, None], taps[:, jnp.maximum(tau, 0)], 0.0)
    toep = toep.transpose(0, 1, 4, 2, 3).reshape(g, q * S5_GROUP, q * S5_GROUP)
    wb = pw[q - 1 - pos][:, :, :, None] * b_bar[None]
    wt = jnp.concatenate([wb.real, wb.imag], axis=2).transpose(1, 0, 3, 2).reshape(g, q * S5_GROUP, 2 * p)
    cv = c_mat[None] * pw[1:q + 1, :, None, :]
    vt = jnp.concatenate([cv.real, -cv.imag], axis=3).transpose(1, 3, 0, 2).reshape(g, 2 * p, q * S5_GROUP)
    aq = pw[q]
    a_re = jnp.concatenate([aq.real, aq.real], axis=1).reshape(g, 1, 2 * p)
    a_im = jnp.concatenate([-aq.imag, aq.imag], axis=1).reshape(g, 1, 2 * p)
    return toep.astype(BF16), wt.astype(BF16), vt.astype(BF16), a_re, a_im


def _s5_kernel(x_ref, tt_ref, wt_ref, vt_ref, ar_ref, ai_ref, y_ref, s_scr, h_scr, *, nb):
    x = x_ref[0]
    y_ref[0] = jnp.dot(x, tt_ref[0], preferred_element_type=F32)
    s_scr[...] = jnp.dot(x, wt_ref[0], preferred_element_type=F32)
    a_re = ar_ref[0]
    a_im = ai_ref[0]
    half = s_scr.shape[1] // 2

    def step(c, h):
        rows = pl.ds(pl.multiple_of(c * nb, nb), nb)
        h_scr[rows, :] = h
        return a_re * h + a_im * pltpu.roll(h, half, axis=1) + s_scr[rows, :]

    lax.fori_loop(0, x.shape[0] // nb, step, jnp.zeros((nb, s_scr.shape[1]), F32))
    y_ref[0] += jnp.dot(h_scr[...].astype(BF16), vt_ref[0], preferred_element_type=F32)


def _s5_scan(x3, lam_re, lam_im, log_dt, b_re, b_im, c_re, c_im):
    bsz, seq, dm = x3.shape
    g = dm // S5_GROUP
    q = S5_Q
    nc = seq // q
    m = nc * bsz
    toep, wt, vt, a_re, a_im = _s5_tables(lam_re, lam_im, log_dt, b_re, b_im, c_re, c_im)
    xg = x3.reshape(bsz, nc, q, g, S5_GROUP).transpose(3, 1, 0, 2, 4).reshape(g, m, q * S5_GROUP).astype(BF16)
    w = q * S5_GROUP
    y = pl.pallas_call(
        functools.partial(_s5_kernel, nb=bsz),
        name="s5_scan",
        grid=(g,),
        in_specs=[pl.BlockSpec((1, m, w), lambda i: (i, 0, 0)),
                  pl.BlockSpec((1, w, w), lambda i: (i, 0, 0)),
                  pl.BlockSpec((1, w, 2 * S5_STATE), lambda i: (i, 0, 0)),
                  pl.BlockSpec((1, 2 * S5_STATE, w), lambda i: (i, 0, 0)),
                  pl.BlockSpec((1, 1, 2 * S5_STATE), lambda i: (i, 0, 0)),
                  pl.BlockSpec((1, 1, 2 * S5_STATE), lambda i: (i, 0, 0))],
        out_specs=pl.BlockSpec((1, m, w), lambda i: (i, 0, 0)),
        out_shape=jax.ShapeDtypeStruct((g, m, w), F32),
        scratch_shapes=[pltpu.VMEM((m, 2 * S5_STATE), F32), pltpu.VMEM((m, 2 * S5_STATE), F32)],
        compiler_params=_cparams("parallel"),
    )(xg, toep, wt, vt, a_re, a_im)
    return y.reshape(g, nc, bsz, q, S5_GROUP).transpose(2, 1, 3, 0, 4).reshape(bsz, seq, dm)


def _s5_out_kernel(y_ref, x_ref, d_ref, w_ref, bias_ref, g_ref, b_ref, o_ref):
    x = x_ref[...]
    y = _gelu(y_ref[...] + d_ref[...] * x)
    z = jnp.dot(y.astype(BF16), w_ref[...], preferred_element_type=F32) + bias_ref[...]
    out = z[:, :D_MODEL] * _sigmoid(z[:, D_MODEL:])
    o_ref[...] = _layer_norm_rows(ALPHA * x + out, g_ref[...], b_ref[...])


def s5_layer(x3, lam_re, lam_im, log_dt, b_re, b_im, c_re, c_im, d_skip, w_glu, b_glu, g, b, tm=512):
    bsz, seq, dm = x3.shape
    t = bsz * seq
    tm = min(tm, t)
    y = _s5_scan(x3, lam_re, lam_im, log_dt, b_re, b_im, c_re, c_im).reshape(t, dm)
    row = pl.BlockSpec((tm, dm), lambda i: (i, 0))
    return pl.pallas_call(
        _s5_out_kernel,
        name="s5_out",
        grid=(t // tm,),
        in_specs=[row, row, _full((1, dm)), _full((dm, 2 * dm)), _full((1, 2 * dm)), _full((1, dm)), _full((1, dm))],
        out_specs=row,
        out_shape=jax.ShapeDtypeStruct((t, dm), F32),
        compiler_params=_cparams("parallel"),
    )(y, x3.reshape(t, dm), d_skip.reshape(1, -1), w_glu.astype(BF16), b_glu.reshape(1, -1),
      g.reshape(1, -1), b.reshape(1, -1))


def _diff_attn_kernel(qi_ref, kj_ref, sc_ref, q_ref, k_ref, v_ref, rel_ref, g_ref, o_ref, m_scr, l_scr, acc_scr):
    hh = pl.program_id(1)
    pair = pl.program_id(2)
    qi = qi_ref[pair]
    kj = kj_ref[pair]
    tq = q_ref.shape[0]
    slope = sc_ref[0, hh]

    @pl.when(kj == 0)
    def _():
        m_scr[...] = jnp.full(m_scr.shape, NEG_INF, F32)
        l_scr[...] = jnp.zeros(l_scr.shape, F32)
        acc_scr[...] = jnp.zeros(acc_scr.shape, F32)

    def accumulate(bias_fn, shift):
        k = k_ref[...]
        v = v_ref[...]
        rb = min(DA_ROWS, tq)
        lane = lax.broadcasted_iota(jnp.int32, (rb, q_ref.shape[1]), 1)
        old = [(m_scr[j], l_scr[j], acc_scr[j]) for j in range(2)]
        new = [([], [], []) for _ in range(2)]
        blocks = [slice(blk * rb, (blk + 1) * rb) for blk in range(tq // rb)]
        scores = {}
        for rows in blocks:
            q = q_ref[rows, :]
            for j in range(2):
                qj = jnp.where((lane // DA_HEAD) == j, q, jnp.zeros_like(q)) * jnp.asarray(DA_HEAD ** -0.5, q.dtype)
                scores[rows.start, j] = lax.dot_general(qj, k, (((1,), (1,)), ((), ())), preferred_element_type=F32)
        width = m_scr.shape[2]
        for rows in blocks:
            for j in range(2):
                s = bias_fn(scores[rows.start, j], rows)
                m_old = old[j][0][rows, :]
                m_new = jnp.maximum(m_old, jnp.broadcast_to(jnp.max(s, axis=-1, keepdims=True), (rb, width)) + shift)
                p = jnp.exp(s - jnp.concatenate([m_new - shift] * (s.shape[1] // width), axis=1))
                corr = jnp.exp(m_old - m_new)
                new[j][0].append(m_new)
                new[j][1].append(corr * old[j][1][rows, :]
                                 + jnp.broadcast_to(jnp.sum(p, axis=-1, keepdims=True), (rb, width)))
                new[j][2].append(corr * old[j][2][rows, :] + jnp.dot(p.astype(BF16), v, preferred_element_type=F32))
        for j in range(2):
            m_scr[j] = jnp.concatenate(new[j][0], axis=0)
            l_scr[j] = jnp.concatenate(new[j][1], axis=0)
            acc_scr[j] = jnp.concatenate(new[j][2], axis=0)

    @pl.when(kj < qi)
    def _():
        accumulate(lambda s, rows: s + rel_ref[rows, :] * (-slope), -slope * ((qi - kj) * tq).astype(F32))

    @pl.when(kj == qi)
    def _():
        def masked(s, rows):
            row = lax.broadcasted_iota(jnp.int32, s.shape, 0) + rows.start
            col = lax.broadcasted_iota(jnp.int32, s.shape, 1)
            allowed = (col // CHUNK) <= (row // CHUNK)
            return jnp.where(allowed, s + jnp.abs(rel_ref[rows, :]) * (-slope), NEG_INF)

        accumulate(masked, 0.0)
        lam_full = sc_ref[1, 0]
        out_scale = sc_ref[1, 1]
        o = acc_scr[0] / l_scr[0] - lam_full * (acc_scr[1] / l_scr[1])
        o = o * lax.rsqrt(jnp.mean(o * o, axis=-1, keepdims=True) + LN_EPS)
        o_ref[...] = (o * g_ref[...] * out_scale).astype(o_ref.dtype)


def _diff_attention(qkv, scalars, subln_g, bsz, seq):
    t = bsz * seq
    tq = min(DA_TILE, seq)
    nq = seq // tq
    n_h = DA_HEADS
    pairs = [(i, j) for i in range(nq) for j in range(i + 1)]
    qi_of = jnp.array([i for i, _ in pairs], jnp.int32)
    kj_of = jnp.array([j for _, j in pairs], jnp.int32)
    pos = jnp.arange(tq, dtype=F32)
    rel = pos[:, None] - pos[None, :]
    grid_spec = pltpu.PrefetchScalarGridSpec(
        num_scalar_prefetch=2,
        grid=(bsz, n_h, len(pairs)),
        in_specs=[pl.BlockSpec(memory_space=pltpu.SMEM),
                  pl.BlockSpec((tq, 2 * DA_HEAD), lambda b, h, p, qi, kj: (b * nq + qi[p], h)),
                  pl.BlockSpec((tq, 2 * DA_HEAD), lambda b, h, p, qi, kj: (b * nq + kj[p], n_h + h)),
                  pl.BlockSpec((tq, 2 * DA_HEAD), lambda b, h, p, qi, kj: (b * nq + kj[p], 2 * n_h + h)),
                  pl.BlockSpec((tq, tq), lambda b, h, p, qi, kj: (0, 0)),
                  pl.BlockSpec((1, 2 * DA_HEAD), lambda b, h, p, qi, kj: (0, 0))],
        out_specs=pl.BlockSpec((tq, 2 * DA_HEAD), lambda b, h, p, qi, kj: (b * nq + qi[p], h)),
        scratch_shapes=[pltpu.VMEM((2, tq, 2 * DA_HEAD), F32), pltpu.VMEM((2, tq, 2 * DA_HEAD), F32),
                        pltpu.VMEM((2, tq, 2 * DA_HEAD), F32)])
    return pl.pallas_call(
        _diff_attn_kernel,
        name="diff_attention",
        grid_spec=grid_spec,
        out_shape=jax.ShapeDtypeStruct((t, D_MODEL), BF16),
        compiler_params=_cparams("parallel", "parallel", "arbitrary"),
    )(qi_of, kj_of, scalars, qkv, qkv, qkv, rel, subln_g.reshape(1, -1))


def diff_attention_layer(h2, bsz, seq, w_qkv, lam, subln_g, w_o, layer_idx, g, b):
    lam_init = 0.8 - 0.6 * math.exp(-0.3 * layer_idx)
    lf = lam.astype(F32)
    lam_full = jnp.exp(jnp.sum(lf[0] * lf[1])) - jnp.exp(jnp.sum(lf[2] * lf[3])) + lam_init
    slopes = 2.0 ** (-8.0 * jnp.arange(1, DA_HEADS + 1, dtype=F32) / DA_HEADS)
    scalars = jnp.stack([slopes, jnp.zeros((DA_HEADS,), F32).at[0].set(lam_full).at[1].set(1.0 - lam_init)])
    (qkv,) = _linear(h2, [w_qkv.astype(BF16)], [BF16])
    o = _diff_attention(qkv, scalars, subln_g, bsz, seq)
    return _linear_res_ln(o, w_o.astype(BF16), h2, g, b)


def _mamba_kernel(z_ref, xbc_ref, dt_ref, h_ref, cw_ref, cb_ref, dtb_ref, a_ref, d_ref, ng_ref, wo_ref,
                  lg_ref, lb_ref, o_ref, state_scr, halo_scr, y_scr):
    q = z_ref.shape[0]
    n = M2_STATE

    @pl.when(pl.program_id(1) == 0)
    def _():
        state_scr[...] = jnp.zeros(state_scr.shape, F32)
        halo_scr[...] = jnp.zeros(halo_scr.shape, F32)

    xbc_raw = xbc_ref[...]
    xbc = _silu(_causal_conv(xbc_raw, halo_scr[...], cw_ref[...], cb_ref[...]))
    halo_scr[...] = xbc_raw[q - 8:, :]
    row = lax.broadcasted_iota(jnp.int32, (q, q), 0)
    col = lax.broadcasted_iota(jnp.int32, (q, q), 1)
    lower = row >= col
    tri = lower.astype(F32)
    dt = _softplus(dt_ref[...] + dtb_ref[...])
    a_cs = _cumsum_rows(dt * a_ref[...], tri)
    a_cs_t = a_cs.T
    a_last = a_cs[q - 1:q, :]
    e_cs = jnp.exp(a_cs)
    e_dec = jnp.exp(a_last - a_cs)
    e_last = jnp.exp(a_last)
    lane = lax.broadcasted_iota(jnp.int32, (q, 2 * M2_HEADDIM), 1)
    first = lane < M2_HEADDIM
    srow = lax.broadcasted_iota(jnp.int32, (2 * M2_HEADDIM, n), 0)
    heads_per_group = M2_HEADS // M2_GROUPS
    for grp in range(M2_GROUPS):
        bm = xbc[:, M2_INNER + grp * n:M2_INNER + (grp + 1) * n]
        cm = xbc[:, M2_INNER + M2_GROUPS * n + grp * n:M2_INNER + M2_GROUPS * n + (grp + 1) * n]
        cb = _bdot_nt(cm, bm)
        for pair in range(heads_per_group // 2):
            h0 = grp * heads_per_group + 2 * pair
            cols = slice(h0 * M2_HEADDIM, (h0 + 2) * M2_HEADDIM)
            xs = xbc[:, cols]
            dtp = jnp.where(first, dt[:, h0:h0 + 1], dt[:, h0 + 1:h0 + 2])
            xdt = xs * dtp
            y = jnp.zeros((q, 2 * M2_HEADDIM), F32)
            for s in range(2):
                hd = h0 + s
                seg = a_cs[:, hd:hd + 1] - a_cs_t[hd:hd + 1, :]
                lmat = jnp.exp(jnp.where(lower, seg, NEG_INF))
                part = jnp.where(first if s == 0 else jnp.logical_not(first), xdt, 0.0)
                y = y + _bdot(cb * lmat, part)
            state = state_scr[pl.ds(h0 * M2_HEADDIM, 2 * M2_HEADDIM), :]
            y_off = _bdot_nt(cm, state)
            y = y + y_off * jnp.where(first, e_cs[:, h0:h0 + 1], e_cs[:, h0 + 1:h0 + 2])
            dec = jnp.where(first, e_dec[:, h0:h0 + 1], e_dec[:, h0 + 1:h0 + 2])
            grow = jnp.where(srow < M2_HEADDIM, e_last[:, h0:h0 + 1], e_last[:, h0 + 1:h0 + 2])
            state_scr[pl.ds(h0 * M2_HEADDIM, 2 * M2_HEADDIM), :] = state * grow + _bdot_tn(xdt * dec, bm)
            y_scr[:, cols] = y + d_ref[:, cols] * xs
    gate = _silu(z_ref[...])
    y = y_scr[...] * gate
    gw = M2_INNER // M2_GROUPS
    parts = []
    for grp in range(M2_GROUPS):
        yg = y[:, grp * gw:(grp + 1) * gw]
        parts.append(yg * lax.rsqrt(jnp.mean(yg * yg, axis=-1, keepdims=True) + LN_EPS))
    yn = jnp.concatenate(parts, axis=1) * ng_ref[...]
    out = jnp.dot(yn.astype(BF16), wo_ref[...], preferred_element_type=F32)
    o_ref[...] = _layer_norm_rows(ALPHA * h_ref[...] + out, lg_ref[...], lb_ref[...])


def mamba2_layer(h2, bsz, seq, w_in, conv_w, conv_b, dt_bias, a_log, d_skip, norm_g, w_out, g, b):
    t = bsz * seq
    q = min(M2_Q, seq)
    nc = seq // q
    xbc_w = M2_INNER + 2 * M2_GROUPS * M2_STATE
    wb = w_in.astype(BF16)
    hpad = 128 - M2_HEADS
    pad_heads = lambda a: jnp.pad(a.astype(F32).reshape(1, -1), ((0, 0), (0, hpad)))
    z, xbc, dt = _linear(h2, [wb[:, :M2_INNER], wb[:, M2_INNER:M2_INNER + xbc_w],
                              jnp.pad(wb[:, M2_INNER + xbc_w:], ((0, 0), (0, hpad)))],
                         [F32, F32, F32], tm=256)
    d_cols = jnp.repeat(d_skip.astype(F32), M2_HEADDIM).reshape(1, -1)
    rowspec = lambda w: pl.BlockSpec((q, w), lambda i, j: (i * nc + j, 0))
    return pl.pallas_call(
        _mamba_kernel,
        name="mamba2",
        grid=(bsz, nc),
        in_specs=[rowspec(M2_INNER), rowspec(xbc_w), rowspec(128), rowspec(D_MODEL),
                  _full((M2_CONV, xbc_w)), _full((1, xbc_w)), _full((1, 128)), _full((1, 128)),
                  _full((1, M2_INNER)), _full((1, M2_INNER)), _full((M2_INNER, D_MODEL)),
                  _full((1, D_MODEL)), _full((1, D_MODEL))],
        out_specs=rowspec(D_MODEL),
        out_shape=jax.ShapeDtypeStruct((t, D_MODEL), F32),
        scratch_shapes=[pltpu.VMEM((M2_HEADS * M2_HEADDIM, M2_STATE), F32),
                        pltpu.VMEM((8, xbc_w), F32),
                        pltpu.VMEM((q, M2_INNER), F32)],
        compiler_params=_cparams("parallel", "arbitrary"),
    )(z, xbc, dt, h2, conv_w.astype(F32), conv_b.reshape(1, -1), pad_heads(dt_bias),
      pad_heads(-jnp.exp(a_log.astype(F32))), d_cols, norm_g.reshape(1, -1), w_out.astype(BF16),
      g.reshape(1, -1), b.reshape(1, -1))


def _mlstm_proj_kernel(xm_ref, cw_ref, cb_ref, wq_ref, wk_ref, wv_ref, wg_ref, bg_ref,
                       xc_ref, q_ref, k_ref, v_ref, gate_ref, halo_scr):
    tq = xm_ref.shape[0]

    @pl.when(pl.program_id(1) == 0)
    def _():
        halo_scr[...] = jnp.zeros(halo_scr.shape, F32)

    xm = xm_ref[...]
    xc = _silu(_causal_conv(xm, halo_scr[...], cw_ref[...], cb_ref[...]))
    halo_scr[...] = xm[tq - 8:, :]
    xc_ref[...] = xc
    gates = jnp.zeros((tq, bg_ref.shape[1]), F32) + bg_ref[...]
    for hd in range(ML_HEADS):
        cols = slice(hd * ML_HEADDIM, (hd + 1) * ML_HEADDIM)
        qh = _bdot(xc[:, cols], wq_ref[hd])
        kh = _bdot(xc[:, cols], wk_ref[hd]) * ML_HEADDIM ** -0.5
        vh = _bdot(xm[:, cols], wv_ref[hd])
        q_ref[:, cols] = qh.astype(BF16)
        k_ref[:, cols] = kh.astype(BF16)
        v_ref[:, cols] = vh.astype(BF16)
        gates = gates + _bdot(qh, wg_ref[0, cols, :]) + _bdot(kh, wg_ref[1, cols, :]) + _bdot(vh, wg_ref[2, cols, :])
    gate_ref[...] = gates


def _mlstm_cell_kernel(q_ref, k_ref, v_ref, gate_ref, h_ref, c_scr, n_scr, m_scr):
    hd = pl.program_id(1)
    qn = q_ref.shape[0]

    @pl.when(pl.program_id(2) == 0)
    def _():
        c_scr[...] = jnp.zeros(c_scr.shape, F32)
        n_scr[...] = jnp.zeros(n_scr.shape, F32)
        m_scr[...] = jnp.zeros(m_scr.shape, F32)

    q = q_ref[...]
    k = k_ref[...]
    v = v_ref[...]
    gates = gate_ref[...]
    glane = lax.broadcasted_iota(jnp.int32, gates.shape, 1)
    gsub = lax.broadcasted_iota(jnp.int32, (gates.shape[1], qn), 0)
    row = lax.broadcasted_iota(jnp.int32, (qn, qn), 0)
    col = lax.broadcasted_iota(jnp.int32, (qn, qn), 1)
    lower = row >= col
    logf = jnp.minimum(gates, 0.0) - jnp.log(1.0 + jnp.exp(-jnp.abs(gates)))
    cum = _cumsum_rows(logf, lower.astype(F32))
    ii = jnp.sum(jnp.where(glane == hd, gates, 0.0), axis=-1, keepdims=True)
    bcs = jnp.sum(jnp.where(glane == ML_HEADS + hd, cum, 0.0), axis=-1, keepdims=True)
    ii_row = jnp.sum(jnp.where(gsub == hd, gates.T, 0.0), axis=0, keepdims=True)
    bcs_row = jnp.sum(jnp.where(gsub == ML_HEADS + hd, cum.T, 0.0), axis=0, keepdims=True)
    m_prev = m_scr[0:1, 0:1]
    dmat = jnp.where(lower, bcs - bcs_row + ii_row, NEG_INF)
    inter = bcs + m_prev
    m_row = jnp.maximum(jnp.max(dmat, axis=-1, keepdims=True), inter)
    s = _bdot_nt(q, k) * jnp.exp(dmat - m_row)
    w_inter = jnp.exp(inter - m_row)
    num = _bdot(s, v) + w_inter * _bdot(q, c_scr[...])
    den = jnp.sum(s, axis=-1, keepdims=True) + w_inter * jnp.sum(q.astype(F32) * n_scr[0:1, :], axis=-1, keepdims=True)
    h_ref[...] = num / jnp.maximum(jnp.abs(den), jnp.exp(-m_row))
    b_last = bcs[qn - 1:qn, :]
    gdec = b_last - bcs + ii
    m_new = jnp.maximum(b_last + m_prev, jnp.max(gdec, axis=0, keepdims=True))
    wk = jnp.exp(gdec - m_new)
    decay = jnp.exp(b_last + m_prev - m_new)
    kw = k.astype(F32) * wk
    c_scr[...] = decay * c_scr[...] + _bdot_tn(kw, v)
    n_scr[...] = decay * n_scr[...] + jnp.sum(kw, axis=0, keepdims=True)
    m_scr[...] = jnp.zeros(m_scr.shape, F32) + m_new


def _mlstm_out_kernel(hc_ref, xc_ref, og_ref, h_ref, ng_ref, sk_ref, wd_ref, lg_ref, lb_ref, o_ref):
    hc = hc_ref[...]
    parts = []
    for hd in range(ML_HEADS):
        x = hc[:, hd * ML_HEADDIM:(hd + 1) * ML_HEADDIM]
        mu = jnp.mean(x, axis=-1, keepdims=True)
        c = x - mu
        parts.append(c * lax.rsqrt(jnp.mean(c * c, axis=-1, keepdims=True) + LN_EPS))
    hn = jnp.concatenate(parts, axis=1) * ng_ref[...] + sk_ref[...] * xc_ref[...]
    out = hn * _sigmoid(og_ref[...])
    y = jnp.dot(out.astype(BF16), wd_ref[...], preferred_element_type=F32)
    o_ref[...] = _layer_norm_rows(ALPHA * h_ref[...] + y, lg_ref[...], lb_ref[...])


def mlstm_layer(h2, bsz, seq, w_in, conv_w, conv_b, w_q, w_k, w_v, w_gates, b_gates, norm_g, skip, w_down, g, b):
    t = bsz * seq
    wb = w_in.astype(BF16)
    xm, og = _linear(h2, [wb[:, :ML_INNER], wb[:, ML_INNER:]], [F32, F32])
    tq = min(256, seq)
    nt = seq // tq
    rows = lambda w: pl.BlockSpec((tq, w), lambda i, j: (i * nt + j, 0))
    gpad = 128 - 2 * ML_HEADS
    wg = jnp.pad(w_gates.astype(BF16), ((0, 0), (0, 0), (0, gpad)))
    bg = jnp.pad(b_gates.astype(F32).reshape(1, -1), ((0, 0), (0, gpad)))
    xc, q, k, v, gates = pl.pallas_call(
        _mlstm_proj_kernel,
        name="mlstm_proj",
        grid=(bsz, nt),
        in_specs=[rows(ML_INNER), _full((ML_CONV, ML_INNER)), _full((1, ML_INNER)),
                  _full(w_q.shape), _full(w_k.shape), _full(w_v.shape), _full(wg.shape), _full(bg.shape)],
        out_specs=[rows(ML_INNER), rows(ML_INNER), rows(ML_INNER), rows(ML_INNER), rows(128)],
        out_shape=[jax.ShapeDtypeStruct((t, ML_INNER), F32)] + [jax.ShapeDtypeStruct((t, ML_INNER), BF16)] * 3
        + [jax.ShapeDtypeStruct((t, 128), F32)],
        scratch_shapes=[pltpu.VMEM((8, ML_INNER), F32)],
        compiler_params=_cparams("parallel", "arbitrary"),
    )(xm, conv_w.astype(F32), conv_b.reshape(1, -1), w_q.astype(BF16), w_k.astype(BF16), w_v.astype(BF16), wg, bg)
    qn = min(ML_Q, seq)
    nc = seq // qn
    head = pl.BlockSpec((qn, ML_HEADDIM), lambda i, hd, j: (i * nc + j, hd))
    hc = pl.pallas_call(
        _mlstm_cell_kernel,
        name="mlstm_cell",
        grid=(bsz, ML_HEADS, nc),
        in_specs=[head, head, head, pl.BlockSpec((qn, 128), lambda i, hd, j: (i * nc + j, 0))],
        out_specs=head,
        out_shape=jax.ShapeDtypeStruct((t, ML_INNER), F32),
        scratch_shapes=[pltpu.VMEM((ML_HEADDIM, ML_HEADDIM), F32), pltpu.VMEM((8, ML_HEADDIM), F32),
                        pltpu.VMEM((8, 128), F32)],
        compiler_params=_cparams("parallel", "parallel", "arbitrary"),
    )(q, k, v, gates)
    tm = min(512, t)
    row = lambda w: pl.BlockSpec((tm, w), lambda i: (i, 0))
    return pl.pallas_call(
        _mlstm_out_kernel,
        name="mlstm_out",
        grid=(t // tm,),
        in_specs=[row(ML_INNER), row(ML_INNER), row(ML_INNER), row(D_MODEL), _full((1, ML_INNER)),
                  _full((1, ML_INNER)), _full((ML_INNER, D_MODEL)), _full((1, D_MODEL)), _full((1, D_MODEL))],
        out_specs=row(D_MODEL),
        out_shape=jax.ShapeDtypeStruct((t, D_MODEL), F32),
        compiler_params=_cparams("parallel"),
    )(hc, xc, og, h2, norm_g.reshape(1, -1), skip.reshape(1, -1), w_down.astype(BF16),
      g.reshape(1, -1), b.reshape(1, -1))


def _cross_attn_kernel(h_ref, k_ref, v_ref, wq_ref, wo_ref, g_ref, b_ref, o_ref):
    h = h_ref[...]
    q = jnp.dot(h.astype(BF16), wq_ref[...], preferred_element_type=F32)
    parts = []
    for hd in range(XA_HEADS):
        cols = slice(hd * XA_HEADDIM, (hd + 1) * XA_HEADDIM)
        s = _bdot_nt(q[:, cols], k_ref[:, cols]) * XA_HEADDIM ** -0.5
        p = jnp.exp(s - jnp.max(s, axis=-1, keepdims=True))
        p = p / jnp.sum(p, axis=-1, keepdims=True)
        parts.append(_bdot(p, v_ref[:, cols]))
    o = jnp.concatenate(parts, axis=1)
    y = jnp.dot(o.astype(BF16), wo_ref[...], preferred_element_type=F32)
    o_ref[...] = _layer_norm_rows(ALPHA * h + y, g_ref[...], b_ref[...])


def cross_attention_layer(h2, mem2, bsz, seq, w_q, w_kv, w_o, g, b, tq=512):
    t = bsz * seq
    mlen = mem2.shape[0] // bsz
    tq = min(tq, seq)
    nq = seq // tq
    (kv,) = _linear(mem2, [w_kv.astype(BF16)], [BF16], tm=256)
    return pl.pallas_call(
        _cross_attn_kernel,
        name="cross_attention",
        grid=(bsz, nq),
        in_specs=[pl.BlockSpec((tq, D_MODEL), lambda i, j: (i * nq + j, 0)),
                  pl.BlockSpec((mlen, D_MODEL), lambda i, j: (i, 0)),
                  pl.BlockSpec((mlen, D_MODEL), lambda i, j: (i, 1)),
                  _full((D_MODEL, D_MODEL)), _full((D_MODEL, D_MODEL)), _full((1, D_MODEL)), _full((1, D_MODEL))],
        out_specs=pl.BlockSpec((tq, D_MODEL), lambda i, j: (i * nq + j, 0)),
        out_shape=jax.ShapeDtypeStruct((t, D_MODEL), F32),
        compiler_params=_cparams("parallel", "parallel"),
    )(h2, kv, kv, w_q.astype(BF16), w_o.astype(BF16), g.reshape(1, -1), b.reshape(1, -1))


_PK_PIECES = (('a', 0, 0), ('a', 0, 8), ('a', 1, 0), ('a', 2, 0), ('a', 3, 0),
              ('b', 0, 8), ('b', 0, 0), ('b', 1, 0), ('b', 2, 0))
_PK_INVALID = 1 << 20


def _peer_piece_ids():
    ids = []
    seen = set()
    for kind, fixed, off in _PK_PIECES:
        for i in range(8):
            a, b = (fixed, off + i) if kind == 'a' else (off + i, fixed)
            ok = (a + 1) * (b + 1) <= PK_TOPK and (a, b) not in seen
            seen.add((a, b))
            ids.append(a * PK_TOPK + b if ok else _PK_INVALID)
    assert sum(i != _PK_INVALID for i in ids) == sum((a + 1) * (b + 1) <= PK_TOPK
                                                     for a in range(PK_TOPK) for b in range(PK_TOPK))
    return jnp.array(ids, jnp.int32).reshape(-1, 1)


def _peer_select_kernel(h_ref, wq_ref, keys_ref, flat_ref, eidx_ref, gate_ref, q_scr, e_scr):
    tm = h_ref.shape[0]
    q_scr[...] = jnp.dot(h_ref[...].astype(BF16), wq_ref[...], preferred_element_type=F32).astype(BF16)
    key_id = lax.broadcasted_iota(jnp.int32, (PK_NKEYS, tm), 0)
    rank_id = lax.broadcasted_iota(jnp.int32, (PK_TOPK, tm), 0)
    flat = jnp.broadcast_to(flat_ref[...], (flat_ref.shape[0], tm))
    neg = jnp.float32(-jnp.inf)
    half = PK_QDIM // 2
    zero = jnp.zeros((PK_TOPK, tm), F32)

    n_col = PK_NKEYS // 8
    depth = PK_STACK_DEPTH

    def sort_columns(s):
        vals = [s[8 * v:8 * v + 8, :] for v in range(n_col)]
        keys = [key_id[8 * v:8 * v + 8, :] for v in range(n_col)]
        for base in range(0, n_col, depth):
            for rnd in range(depth):
                for i in range(base + rnd % 2, base + depth - 1, 2):
                    swap = vals[i + 1] > vals[i]
                    vals[i], vals[i + 1] = jnp.where(swap, vals[i + 1], vals[i]), jnp.where(swap, vals[i], vals[i + 1])
                    keys[i], keys[i + 1] = jnp.where(swap, keys[i + 1], keys[i]), jnp.where(swap, keys[i], keys[i + 1])
        return vals + keys

    def top1(a, st):
        vals, keys, top_v, top_i = list(st[:n_col]), list(st[n_col:2 * n_col]), st[2 * n_col], st[2 * n_col + 1]
        heads = range(0, n_col, depth)
        best = functools.reduce(jnp.maximum, [vals[h] for h in heads])
        m = jnp.max(best, axis=0, keepdims=True)
        first_key = functools.reduce(jnp.minimum, [jnp.where(vals[h] == m, keys[h], PK_NKEYS) for h in heads])
        idx = jnp.min(first_key, axis=0, keepdims=True)
        for h in heads:
            win = keys[h] == idx
            for k in range(h, h + depth - 1):
                vals[k] = jnp.where(win, vals[k + 1], vals[k])
                keys[k] = jnp.where(win, keys[k + 1], keys[k])
            vals[h + depth - 1] = jnp.where(win, neg, vals[h + depth - 1])
        sel = rank_id == a
        return vals + keys + [jnp.where(sel, m, top_v), jnp.where(sel, idx.astype(F32), top_i)]

    def candidates(vals0, idxs0, vals1, idxs1):
        rep = lambda x, r: jnp.broadcast_to(x[r:r + 1, :], (8, tm))
        v, c = [], []
        for kind, fixed, off in _PK_PIECES:
            if kind == 'a':
                v.append(rep(vals0, fixed) + vals1[off:off + 8, :])
                c.append(rep(idxs0, fixed) * PK_NKEYS + idxs1[off:off + 8, :])
            else:
                v.append(vals0[off:off + 8, :] + rep(vals1, fixed))
                c.append(idxs0[off:off + 8, :] * PK_NKEYS + rep(idxs1, fixed))
        cand = jnp.where(flat == _PK_INVALID, neg, jnp.concatenate(v, axis=0))
        return cand, jnp.concatenate(c, axis=0)

    def top2(kk, cand, cidx, cv, ce):
        m = jnp.max(cand, axis=0, keepdims=True)
        pos = jnp.min(jnp.where(cand == m, flat, _PK_INVALID), axis=0, keepdims=True)
        hit = flat == pos
        e = jnp.max(jnp.where(hit, cidx, -1.0), axis=0, keepdims=True)
        sel = rank_id == kk
        return jnp.where(hit, neg, cand), jnp.where(sel, m, cv), jnp.where(sel, e, ce)

    tops = None
    for stage in range(PK_HEADS + 1):
        first = stage < PK_HEADS
        second = stage > 0
        init = []
        n_st = 2 * n_col + 2
        if first:
            for j in range(2):
                cols = slice(stage * PK_QDIM + j * half, stage * PK_QDIM + (j + 1) * half)
                init += sort_columns(_bdot_nt(keys_ref[j], q_scr[:, cols])) + [zero, zero]
        if second:
            cand0, cidx = candidates(*tops)
            init += [cand0, zero, zero]

        def body(a, st, first=first, second=second, cidx=cidx if second else None):
            st = list(st)
            out = []
            if first:
                out += top1(a, st[0:n_st]) + top1(a, st[n_st:2 * n_st])
                st = st[2 * n_st:]
            if second:
                out += top2(a, st[0], cidx, st[1], st[2])
            return tuple(out)

        res = lax.fori_loop(0, PK_TOPK // 4, lambda i, st, body=body: body(4 * i + 3, body(4 * i + 2, body(
            4 * i + 1, body(4 * i, st)))), tuple(init))
        if second:
            cv, ce = res[-2], res[-1]
            p = jnp.exp(cv - cv[0:1, :])
            rows = slice((stage - 1) * PK_TOPK, stage * PK_TOPK)
            gate_ref[rows, :] = p / jnp.sum(p, axis=0, keepdims=True)
            e_scr[rows, :] = ce * ROW_WORDS
        if first:
            tops = (res[n_st - 2], res[n_st - 1], res[2 * n_st - 2], res[2 * n_st - 1])
    eidx_ref[...] = e_scr[...].T.astype(jnp.int32)


def _peer_select(h2, wq_bf, keys_bf, tm=128):
    t = h2.shape[0]
    tm = min(tm, t)
    flat = _peer_piece_ids()
    return pl.pallas_call(
        _peer_select_kernel,
        name="peer_select",
        grid=(t // tm,),
        in_specs=[pl.BlockSpec((tm, D_MODEL), lambda i: (i, 0)), _full(wq_bf.shape), _full(keys_bf.shape),
                  _full(flat.shape)],
        out_specs=[pl.BlockSpec((tm, PK_SLOTS), lambda i: (i, 0)),
                   pl.BlockSpec((PK_SLOTS, tm), lambda i: (0, i))],
        out_shape=[jax.ShapeDtypeStruct((t, PK_SLOTS), jnp.int32),
                   jax.ShapeDtypeStruct((PK_SLOTS, t), F32)],
        scratch_shapes=[pltpu.VMEM((tm, PK_HEADS * PK_QDIM), BF16), pltpu.VMEM((PK_SLOTS, tm), F32)],
        compiler_params=_cparams("parallel"),
    )(h2, wq_bf, keys_bf, flat)


def _pack_rows(w):
    e, d = w.shape
    wb = lax.bitcast_convert_type(w.astype(BF16), jnp.uint16).astype(jnp.uint32)
    packed = (wb[:, d // 2:] << 16) | wb[:, :d // 2]
    return lax.bitcast_convert_type(packed, jnp.int32).reshape(e * ROW_WORDS, 128)


def _unpack_lo(w):
    return pltpu.bitcast(w << 16, F32)


def _unpack_hi(w):
    return pltpu.bitcast(w & jnp.int32(-65536), F32)


def _gather_group(idx_ref, t0, tb, tab_ref, tiles_ref, first_tile):
    for u in range(PK_GROUP):
        start = jnp.minimum(t0 + u, tb - 1) * PK_SLOTS
        for r0 in range(0, PK_SLOTS, PK_IDX_RUN):
            run_ref = idx_ref.at[pl.ds(start + r0, PK_IDX_RUN)]
            for j in range(PK_IDX_RUN):
                i = pl.multiple_of(run_ref[j], ROW_WORDS)
                tiles_ref[first_tile + u, pl.ds((r0 + j) * ROW_WORDS, ROW_WORDS), :] = tab_ref[pl.ds(i, ROW_WORDS), :]


def _group_pipeline(tb, idx_ref, tab_ref, tiles_ref, dense_group):
    _gather_group(idx_ref, 0, tb, tab_ref, tiles_ref, 0)

    def trip(i, carry):
        t0 = 2 * PK_GROUP * i
        for half in range(2):
            base = pl.multiple_of(t0 + half * PK_GROUP, PK_GROUP)
            _gather_group(idx_ref, base + PK_GROUP, tb, tab_ref, tiles_ref, (1 - half) * PK_GROUP)
            dense_group(base, half * PK_GROUP)
        return carry

    lax.fori_loop(0, tb // (2 * PK_GROUP), trip, 0)


def _lane_block(base):
    return pl.multiple_of((base // 128) * 128, 128)


def _peer_score_kernel(idx_ref, x_ref, gate_ref, tab_ref, act_ref, tiles_ref):
    tb = x_ref.shape[0]
    lane = lax.broadcasted_iota(jnp.int32, (PK_SLOTS, 128), 1)
    half = D_MODEL // 2

    def dense_group(base, first_tile):
        xg = x_ref[pl.ds(base, PK_GROUP), :]
        blk = _lane_block(base)
        cols = act_ref[:, pl.ds(blk, 128)]
        for u in range(PK_GROUP):
            acc = jnp.zeros((PK_SLOTS, 128), F32)
            for s in range(ROW_WORDS):
                w = tiles_ref[first_tile + u, pl.ds(s, PK_SLOTS, stride=ROW_WORDS), :]
                acc = (acc + _unpack_lo(w) * xg[u:u + 1, s * 128:(s + 1) * 128]
                       + _unpack_hi(w) * xg[u:u + 1, half + s * 128:half + (s + 1) * 128])
            cols = jnp.where(lane == base + u - blk, jnp.sum(acc, axis=-1, keepdims=True), cols)
        act_ref[:, pl.ds(blk, 128)] = cols

    act_ref[...] = jnp.zeros(act_ref.shape, F32)
    _group_pipeline(tb, idx_ref, tab_ref, tiles_ref, dense_group)
    act_ref[...] = _gelu(act_ref[...]) * gate_ref[...]


def _peer_score(eidx, x2, gate, tab, tb=512):
    t = x2.shape[0]
    tb = min(tb, t)
    slots = pl.BlockSpec((PK_SLOTS, tb), lambda i: (0, i))
    return pl.pallas_call(
        _peer_score_kernel,
        name="peer_score",
        grid=(t // tb,),
        in_specs=[pl.BlockSpec((tb * PK_SLOTS,), lambda i: (i,), memory_space=pltpu.SMEM),
                  pl.BlockSpec((tb, D_MODEL), lambda i: (i, 0)), slots, _full(tab.shape)],
        out_specs=slots,
        out_shape=jax.ShapeDtypeStruct((PK_SLOTS, t), F32),
        scratch_shapes=[pltpu.VMEM((2 * PK_GROUP, PK_SLOTS * ROW_WORDS, 128), jnp.int32)],
        compiler_params=_cparams("parallel"),
    )(eidx.reshape(-1), x2, gate, tab)


def _peer_combine_kernel(idx_ref, act_ref, h_ref, g_ref, b_ref, tab_ref, out_ref, y_ref, tiles_ref):
    tb = h_ref.shape[0]
    lane = lax.broadcasted_iota(jnp.int32, (PK_SLOTS, 128), 1)

    def dense_group(base, first_tile):
        blk = _lane_block(base)
        act = act_ref[:, pl.ds(blk, 128)]
        rows = []
        for u in range(PK_GROUP):
            a = jnp.sum(jnp.where(lane == base + u - blk, act, 0.0), axis=-1, keepdims=True)
            lo, hi = [], []
            for s in range(ROW_WORDS):
                w = tiles_ref[first_tile + u, pl.ds(s, PK_SLOTS, stride=ROW_WORDS), :]
                lo.append(jnp.sum(a * _unpack_lo(w), axis=0, keepdims=True))
                hi.append(jnp.sum(a * _unpack_hi(w), axis=0, keepdims=True))
            rows.append(jnp.concatenate(lo + hi, axis=1))
        y_ref[pl.ds(base, PK_GROUP), :] = jnp.concatenate(rows, axis=0)

    _group_pipeline(tb, idx_ref, tab_ref, tiles_ref, dense_group)
    out_ref[...] = _layer_norm_rows(ALPHA * h_ref[...] + y_ref[...], g_ref[...], b_ref[...])


def _peer_combine(eidx, act, h2, g, b, tab, tb=512):
    t = h2.shape[0]
    tb = min(tb, t)
    tok = pl.BlockSpec((tb, D_MODEL), lambda i: (i, 0))
    return pl.pallas_call(
        _peer_combine_kernel,
        name="peer_combine",
        grid=(t // tb,),
        in_specs=[pl.BlockSpec((tb * PK_SLOTS,), lambda i: (i,), memory_space=pltpu.SMEM),
                  pl.BlockSpec((PK_SLOTS, tb), lambda i: (0, i)),
                  tok, _full((1, D_MODEL)), _full((1, D_MODEL)), _full(tab.shape)],
        out_specs=tok,
        out_shape=jax.ShapeDtypeStruct((t, D_MODEL), F32),
        scratch_shapes=[pltpu.VMEM((tb, D_MODEL), F32),
                        pltpu.VMEM((2 * PK_GROUP, PK_SLOTS * ROW_WORDS, 128), jnp.int32)],
        compiler_params=_cparams("parallel"),
    )(eidx.reshape(-1), act, h2, g.reshape(1, -1), b.reshape(1, -1), tab)


def peer_layer(h2, w_query, sub_keys, u, v, g, b):
    eidx, gate = _peer_select(h2, w_query.astype(BF16), sub_keys.astype(BF16))
    act = _peer_score(eidx, h2, gate, _pack_rows(u))
    return _peer_combine(eidx, act, h2, g, b, _pack_rows(v))


def kernel(x, mem, s5_lam_re, s5_lam_im, s5_log_dt, s5_b_re, s5_b_im, s5_c_re, s5_c_im, s5_d, s5_w_glu, s5_b_glu, da_w_qkv, da_lambda, da_subln_g, da_w_o, m2_w_in, m2_conv_w, m2_conv_b, m2_dt_bias, m2_a_log, m2_d, m2_norm_g, m2_w_out, ml_w_in, ml_conv_w, ml_conv_b, ml_w_q, ml_w_k, ml_w_v, ml_w_gates, ml_b_gates, ml_norm_g, ml_skip, ml_w_down, xa_w_q, xa_w_kv, xa_w_o, pk_w_query, pk_sub_keys, pk_u, pk_v, ln_g, ln_b):
    bsz, seq, dm = x.shape
    t = bsz * seq
    mem2 = mem.reshape(-1, dm)
    h = x.reshape(t, dm)
    for i in range(DEPTH):
        kind, j = i % 4, i // 4
        g0, b0 = ln_g[i, 0], ln_b[i, 0]
        if kind == 0:
            h = s5_layer(h.reshape(bsz, seq, dm), s5_lam_re[j], s5_lam_im[j], s5_log_dt[j], s5_b_re[j], s5_b_im[j],
                         s5_c_re[j], s5_c_im[j], s5_d[j], s5_w_glu[j], s5_b_glu[j], g0, b0)
        elif kind == 1:
            h = diff_attention_layer(h, bsz, seq, da_w_qkv[j], da_lambda[j], da_subln_g[j], da_w_o[j], i, g0, b0)
        elif kind == 2:
            h = mamba2_layer(h, bsz, seq, m2_w_in[j], m2_conv_w[j], m2_conv_b[j], m2_dt_bias[j], m2_a_log[j],
                             m2_d[j], m2_norm_g[j], m2_w_out[j], g0, b0)
        else:
            h = mlstm_layer(h, bsz, seq, ml_w_in[j], ml_conv_w[j], ml_conv_b[j], ml_w_q[j], ml_w_k[j], ml_w_v[j],
                            ml_w_gates[j], ml_b_gates[j], ml_norm_g[j], ml_skip[j], ml_w_down[j], g0, b0)
        h = cross_attention_layer(h, mem2, bsz, seq, xa_w_q[i], xa_w_kv[i], xa_w_o[i], ln_g[i, 1], ln_b[i, 1])
        h = peer_layer(h, pk_w_query[i], pk_sub_keys[i], pk_u[i], pk_v[i], ln_g[i, 2], ln_b[i, 2])
    return h.reshape(bsz, seq, dm)
```

```python
import functools
import math

import jax
import jax.numpy as jnp
from jax import lax
from jax.experimental import pallas as pl
from jax.experimental.pallas import tpu as pltpu

F32 = jnp.float32
BF16 = jnp.bfloat16
HIGHEST = lax.Precision.HIGHEST

D_MODEL = 1024
DEPTH = 4
ALPHA = (2 * DEPTH) ** 0.25
LN_EPS = 1e-5
NEG_INF = -1e30
CHUNK = 64

S5_GROUP = 16
S5_STATE = 64
S5_Q = 64

DA_HEAD = 64
DA_HEADS = 8
DA_TILE = 1024
DA_ROWS = 1024

M2_INNER = 2048
M2_HEADS = 32
M2_HEADDIM = 64
M2_GROUPS = 4
M2_STATE = 128
M2_CONV = 4
M2_Q = 256

ML_INNER = 2048
ML_HEADS = 4
ML_HEADDIM = 512
ML_CONV = 4
ML_Q = 256

XA_HEADS = 4
XA_HEADDIM = 256

PK_HEADS = 8
PK_NKEYS = 128
PK_QDIM = 256
PK_TOPK = 16
PK_SLOTS = PK_HEADS * PK_TOPK
ROW_WORDS = D_MODEL // 2 // 128
PK_IDX_RUN = 8
PK_STACK_DEPTH = 4
PK_TOPK_UNROLL = 8
PK_GROUP = 8

VMEM_LIMIT = 56 * 1024 * 1024


def _cparams(*sem):
    return pltpu.CompilerParams(dimension_semantics=sem, vmem_limit_bytes=VMEM_LIMIT)


def _gelu(x):
    return 0.5 * x * (1.0 + jnp.tanh(math.sqrt(2.0 / math.pi) * (x + 0.044715 * x * x * x)))


def _sigmoid(x):
    return 1.0 / (1.0 + jnp.exp(-x))


def _silu(x):
    return x * _sigmoid(x)


def _softplus(x):
    return jnp.maximum(x, 0.0) + jnp.log(1.0 + jnp.exp(-jnp.abs(x)))


def _layer_norm_rows(v, g, b):
    mu = jnp.mean(v, axis=-1, keepdims=True)
    c = v - mu
    var = jnp.mean(c * c, axis=-1, keepdims=True)
    return c * lax.rsqrt(var + LN_EPS) * g + b


def _bdot(a, b):
    return jnp.dot(a.astype(BF16), b.astype(BF16), preferred_element_type=F32)


def _bdot_nt(a, b):
    return lax.dot_general(a.astype(BF16), b.astype(BF16), (((1,), (1,)), ((), ())), preferred_element_type=F32)


def _bdot_tn(a, b):
    return lax.dot_general(a.astype(BF16), b.astype(BF16), (((0,), (0,)), ((), ())), preferred_element_type=F32)


def _full(shape):
    return pl.BlockSpec(shape, lambda *_: (0,) * len(shape))


def _causal_conv(x, halo, w, b):
    q = x.shape[0]
    k = w.shape[0]
    ext = jnp.concatenate([halo, x], axis=0)
    out = x * w[k - 1:k, :] + b
    for d in range(1, k):
        out = out + ext[8 - d:8 - d + q, :] * w[k - 1 - d:k - d, :]
    return out


def _cumsum_rows(a, tri):
    return jnp.dot(tri, a, precision=HIGHEST, preferred_element_type=F32)


def _linear_kernel(*refs, n_out):
    a = refs[0][...].astype(BF16)
    for w_ref, o_ref in zip(refs[1:1 + n_out], refs[1 + n_out:]):
        o_ref[...] = jnp.dot(a, w_ref[...], preferred_element_type=F32).astype(o_ref.dtype)


def _linear(a, ws, out_dtypes, tm=512):
    m, k = a.shape
    tm = min(tm, m)
    return pl.pallas_call(
        functools.partial(_linear_kernel, n_out=len(ws)),
        name="linear",
        grid=(m // tm,),
        in_specs=[pl.BlockSpec((tm, k), lambda i: (i, 0))] + [_full(w.shape) for w in ws],
        out_specs=[pl.BlockSpec((tm, w.shape[1]), lambda i: (i, 0)) for w in ws],
        out_shape=[jax.ShapeDtypeStruct((m, w.shape[1]), dt) for w, dt in zip(ws, out_dtypes)],
        compiler_params=_cparams("parallel"),
    )(a, *ws)


def _linear_res_ln_kernel(a_ref, w_ref, h_ref, g_ref, b_ref, o_ref):
    y = jnp.dot(a_ref[...].astype(BF16), w_ref[...], preferred_element_type=F32)
    o_ref[...] = _layer_norm_rows(ALPHA * h_ref[...] + y, g_ref[...], b_ref[...])


def _linear_res_ln(a, w, h, g, b, tm=512):
    m, k = a.shape
    tm = min(tm, m)
    return pl.pallas_call(
        _linear_res_ln_kernel,
        name="linear_res_ln",
        grid=(m // tm,),
        in_specs=[pl.BlockSpec((tm, k), lambda i: (i, 0)), _full(w.shape),
                  pl.BlockSpec((tm, D_MODEL), lambda i: (i, 0)), _full((1, D_MODEL)), _full((1, D_MODEL))],
        out_specs=pl.BlockSpec((tm, D_MODEL), lambda i: (i, 0)),
        out_shape=jax.ShapeDtypeStruct((m, D_MODEL), F32),
        compiler_params=_cparams("parallel"),
    )(a, w, h, g.reshape(1, -1), b.reshape(1, -1))


def _s5_tables(lam_re, lam_im, log_dt, b_re, b_im, c_re, c_im):
    q = S5_Q
    g, p = lam_re.shape
    lam = lax.complex(lam_re.astype(F32), lam_im.astype(F32))
    dt = jnp.exp(log_dt.astype(F32))[:, None]
    lam_bar = jnp.exp(lam * dt)
    b_bar = ((lam_bar - 1.0) / lam)[..., None] * lax.complex(b_re.astype(F32), b_im.astype(F32))
    c_mat = lax.complex(c_re.astype(F32), c_im.astype(F32))
    pw = jnp.cumprod(jnp.concatenate([jnp.ones((1, g, p), lam_bar.dtype),
                                      jnp.broadcast_to(lam_bar, (q, g, p))], axis=0), axis=0)
    cp = c_mat[None] * pw[:q, :, None, :]
    taps = (jnp.einsum('tgop,gpi->gtoi', cp.real, b_bar.real, precision=HIGHEST)
            - jnp.einsum('tgop,gpi->gtoi', cp.imag, b_bar.imag, precision=HIGHEST))
    pos = jnp.arange(q)
    tau = pos[None, :] - pos[:, None]
    toep = jnp.where((tau >= 0)[None, :, :, None, None], taps[:, jnp.maximum(tau, 0)], 0.0)
    toep = toep.transpose(0, 1, 4, 2, 3).reshape(g, q * S5_GROUP, q * S5_GROUP)
    wb = pw[q - 1 - pos][:, :, :, None] * b_bar[None]
    wt = jnp.concatenate([wb.real, wb.imag], axis=2).transpose(1, 0, 3, 2).reshape(g, q * S5_GROUP, 2 * p)
    cv = c_mat[None] * pw[1:q + 1, :, None, :]
    vt = jnp.concatenate([cv.real, -cv.imag], axis=3).transpose(1, 3, 0, 2).reshape(g, 2 * p, q * S5_GROUP)
    aq = pw[q]
    a_re = jnp.concatenate([aq.real, aq.real], axis=1).reshape(g, 1, 2 * p)
    a_im = jnp.concatenate([-aq.imag, aq.imag], axis=1).reshape(g, 1, 2 * p)
    return toep.astype(BF16), wt.astype(BF16), vt.astype(BF16), a_re, a_im


def _s5_kernel(x_ref, tt_ref, wt_ref, vt_ref, ar_ref, ai_ref, y_ref, s_scr, h_scr, *, nb):
    x = x_ref[0]
    y_ref[0] = jnp.dot(x, tt_ref[0], preferred_element_type=F32)
    s_scr[...] = jnp.dot(x, wt_ref[0], preferred_element_type=F32)
    a_re = ar_ref[0]
    a_im = ai_ref[0]
    half = s_scr.shape[1] // 2

    def step(c, h):
        rows = pl.ds(pl.multiple_of(c * nb, nb), nb)
        h_scr[rows, :] = h
        return a_re * h + a_im * pltpu.roll(h, half, axis=1) + s_scr[rows, :]

    lax.fori_loop(0, x.shape[0] // nb, step, jnp.zeros((nb, s_scr.shape[1]), F32))
    y_ref[0] += jnp.dot(h_scr[...].astype(BF16), vt_ref[0], preferred_element_type=F32)


def _s5_scan(x3, lam_re, lam_im, log_dt, b_re, b_im, c_re, c_im):
    bsz, seq, dm = x3.shape
    g = dm // S5_GROUP
    q = S5_Q
    nc = seq // q
    m = nc * bsz
    toep, wt, vt, a_re, a_im = _s5_tables(lam_re, lam_im, log_dt, b_re, b_im, c_re, c_im)
    xg = x3.reshape(bsz, nc, q, g, S5_GROUP).transpose(3, 1, 0, 2, 4).reshape(g, m, q * S5_GROUP).astype(BF16)
    w = q * S5_GROUP
    y = pl.pallas_call(
        functools.partial(_s5_kernel, nb=bsz),
        name="s5_scan",
        grid=(g,),
        in_specs=[pl.BlockSpec((1, m, w), lambda i: (i, 0, 0)),
                  pl.BlockSpec((1, w, w), lambda i: (i, 0, 0)),
                  pl.BlockSpec((1, w, 2 * S5_STATE), lambda i: (i, 0, 0)),
                  pl.BlockSpec((1, 2 * S5_STATE, w), lambda i: (i, 0, 0)),
                  pl.BlockSpec((1, 1, 2 * S5_STATE), lambda i: (i, 0, 0)),
                  pl.BlockSpec((1, 1, 2 * S5_STATE), lambda i: (i, 0, 0))],
        out_specs=pl.BlockSpec((1, m, w), lambda i: (i, 0, 0)),
        out_shape=jax.ShapeDtypeStruct((g, m, w), F32),
        scratch_shapes=[pltpu.VMEM((m, 2 * S5_STATE), F32), pltpu.VMEM((m, 2 * S5_STATE), F32)],
        compiler_params=_cparams("parallel"),
    )(xg, toep, wt, vt, a_re, a_im)
    return y.reshape(g, nc, bsz, q, S5_GROUP).transpose(2, 1, 3, 0, 4).reshape(bsz, seq, dm)


def _s5_out_kernel(y_ref, x_ref, d_ref, w_ref, bias_ref, g_ref, b_ref, o_ref):
    x = x_ref[...]
    y = _gelu(y_ref[...] + d_ref[...] * x)
    z = jnp.dot(y.astype(BF16), w_ref[...], preferred_element_type=F32) + bias_ref[...]
    out = z[:, :D_MODEL] * _sigmoid(z[:, D_MODEL:])
    o_ref[...] = _layer_norm_rows(ALPHA * x + out, g_ref[...], b_ref[...])


def s5_layer(x3, lam_re, lam_im, log_dt, b_re, b_im, c_re, c_im, d_skip, w_glu, b_glu, g, b, tm=512):
    bsz, seq, dm = x3.shape
    t = bsz * seq
    tm = min(tm, t)
    y = _s5_scan(x3, lam_re, lam_im, log_dt, b_re, b_im, c_re, c_im).reshape(t, dm)
    row = pl.BlockSpec((tm, dm), lambda i: (i, 0))
    return pl.pallas_call(
        _s5_out_kernel,
        name="s5_out",
        grid=(t // tm,),
        in_specs=[row, row, _full((1, dm)), _full((dm, 2 * dm)), _full((1, 2 * dm)), _full((1, dm)), _full((1, dm))],
        out_specs=row,
        out_shape=jax.ShapeDtypeStruct((t, dm), F32),
        compiler_params=_cparams("parallel"),
    )(y, x3.reshape(t, dm), d_skip.reshape(1, -1), w_glu.astype(BF16), b_glu.reshape(1, -1),
      g.reshape(1, -1), b.reshape(1, -1))


def _diff_attn_kernel(qi_ref, kj_ref, sc_ref, q_ref, k_ref, v_ref, rel_ref, g_ref, o_ref, m_scr, l_scr, acc_scr):
    hh = pl.program_id(1)
    pair = pl.program_id(2)
    qi = qi_ref[pair]
    kj = kj_ref[pair]
    tq = q_ref.shape[0]
    slope = sc_ref[0, hh]

    @pl.when(kj == 0)
    def _():
        m_scr[...] = jnp.full(m_scr.shape, NEG_INF, F32)
        l_scr[...] = jnp.zeros(l_scr.shape, F32)
        acc_scr[...] = jnp.zeros(acc_scr.shape, F32)

    def accumulate(bias_fn, shift):
        k = k_ref[...]
        v = v_ref[...]
        rb = min(DA_ROWS, tq)
        lane = lax.broadcasted_iota(jnp.int32, (rb, q_ref.shape[1]), 1)
        old = [(m_scr[j], l_scr[j], acc_scr[j]) for j in range(2)]
        new = [([], [], []) for _ in range(2)]
        blocks = [slice(blk * rb, (blk + 1) * rb) for blk in range(tq // rb)]
        scores = {}
        for rows in blocks:
            q = q_ref[rows, :]
            for j in range(2):
                qj = jnp.where((lane // DA_HEAD) == j, q, jnp.zeros_like(q)) * jnp.asarray(DA_HEAD ** -0.5, q.dtype)
                scores[rows.start, j] = lax.dot_general(qj, k, (((1,), (1,)), ((), ())), preferred_element_type=F32)
        width = m_scr.shape[2]
        for rows in blocks:
            for j in range(2):
                s = bias_fn(scores[rows.start, j], rows)
                m_old = old[j][0][rows, :]
                m_new = jnp.maximum(m_old, jnp.broadcast_to(jnp.max(s, axis=-1, keepdims=True), (rb, width)) + shift)
                p = jnp.exp(s - jnp.concatenate([m_new - shift] * (s.shape[1] // width), axis=1))
                corr = jnp.exp(m_old - m_new)
                new[j][0].append(m_new)
                new[j][1].append(corr * old[j][1][rows, :]
                                 + jnp.broadcast_to(jnp.sum(p, axis=-1, keepdims=True), (rb, width)))
                new[j][2].append(corr * old[j][2][rows, :] + jnp.dot(p.astype(BF16), v, preferred_element_type=F32))
        for j in range(2):
            m_scr[j] = jnp.concatenate(new[j][0], axis=0)
            l_scr[j] = jnp.concatenate(new[j][1], axis=0)
            acc_scr[j] = jnp.concatenate(new[j][2], axis=0)

    @pl.when(kj < qi)
    def _():
        accumulate(lambda s, rows: s + rel_ref[rows, :] * (-slope), -slope * ((qi - kj) * tq).astype(F32))

    @pl.when(kj == qi)
    def _():
        def masked(s, rows):
            row = lax.broadcasted_iota(jnp.int32, s.shape, 0) + rows.start
            col = lax.broadcasted_iota(jnp.int32, s.shape, 1)
            allowed = (col // CHUNK) <= (row // CHUNK)
            return jnp.where(allowed, s + jnp.abs(rel_ref[rows, :]) * (-slope), NEG_INF)

        accumulate(masked, 0.0)
        lam_full = sc_ref[1, 0]
        out_scale = sc_ref[1, 1]
        o = acc_scr[0] / l_scr[0] - lam_full * (acc_scr[1] / l_scr[1])
        o = o * lax.rsqrt(jnp.mean(o * o, axis=-1, keepdims=True) + LN_EPS)
        o_ref[...] = (o * g_ref[...] * out_scale).astype(o_ref.dtype)


def _diff_attention(qkv, scalars, subln_g, bsz, seq):
    t = bsz * seq
    tq = min(DA_TILE, seq)
    nq = seq // tq
    n_h = DA_HEADS
    pairs = [(i, j) for i in range(nq) for j in range(i + 1)]
    qi_of = jnp.array([i for i, _ in pairs], jnp.int32)
    kj_of = jnp.array([j for _, j in pairs], jnp.int32)
    pos = jnp.arange(tq, dtype=F32)
    rel = pos[:, None] - pos[None, :]
    grid_spec = pltpu.PrefetchScalarGridSpec(
        num_scalar_prefetch=2,
        grid=(bsz, n_h, len(pairs)),
        in_specs=[pl.BlockSpec(memory_space=pltpu.SMEM),
                  pl.BlockSpec((tq, 2 * DA_HEAD), lambda b, h, p, qi, kj: (b * nq + qi[p], h)),
                  pl.BlockSpec((tq, 2 * DA_HEAD), lambda b, h, p, qi, kj: (b * nq + kj[p], n_h + h)),
                  pl.BlockSpec((tq, 2 * DA_HEAD), lambda b, h, p, qi, kj: (b * nq + kj[p], 2 * n_h + h)),
                  pl.BlockSpec((tq, tq), lambda b, h, p, qi, kj: (0, 0)),
                  pl.BlockSpec((1, 2 * DA_HEAD), lambda b, h, p, qi, kj: (0, 0))],
        out_specs=pl.BlockSpec((tq, 2 * DA_HEAD), lambda b, h, p, qi, kj: (b * nq + qi[p], h)),
        scratch_shapes=[pltpu.VMEM((2, tq, 2 * DA_HEAD), F32), pltpu.VMEM((2, tq, 2 * DA_HEAD), F32),
                        pltpu.VMEM((2, tq, 2 * DA_HEAD), F32)])
    return pl.pallas_call(
        _diff_attn_kernel,
        name="diff_attention",
        grid_spec=grid_spec,
        out_shape=jax.ShapeDtypeStruct((t, D_MODEL), BF16),
        compiler_params=_cparams("parallel", "parallel", "arbitrary"),
    )(qi_of, kj_of, scalars, qkv, qkv, qkv, rel, subln_g.reshape(1, -1))


def diff_attention_layer(h2, bsz, seq, w_qkv, lam, subln_g, w_o, layer_idx, g, b):
    lam_init = 0.8 - 0.6 * math.exp(-0.3 * layer_idx)
    lf = lam.astype(F32)
    lam_full = jnp.exp(jnp.sum(lf[0] * lf[1])) - jnp.exp(jnp.sum(lf[2] * lf[3])) + lam_init
    slopes = 2.0 ** (-8.0 * jnp.arange(1, DA_HEADS + 1, dtype=F32) / DA_HEADS)
    scalars = jnp.stack([slopes, jnp.zeros((DA_HEADS,), F32).at[0].set(lam_full).at[1].set(1.0 - lam_init)])
    (qkv,) = _linear(h2, [w_qkv.astype(BF16)], [BF16])
    o = _diff_attention(qkv, scalars, subln_g, bsz, seq)
    return _linear_res_ln(o, w_o.astype(BF16), h2, g, b)


def _mamba_kernel(z_ref, xbc_ref, dt_ref, h_ref, cw_ref, cb_ref, dtb_ref, a_ref, d_ref, ng_ref, wo_ref,
                  lg_ref, lb_ref, o_ref, state_scr, halo_scr, y_scr):
    q = z_ref.shape[0]
    n = M2_STATE

    @pl.when(pl.program_id(1) == 0)
    def _():
        state_scr[...] = jnp.zeros(state_scr.shape, F32)
        halo_scr[...] = jnp.zeros(halo_scr.shape, F32)

    xbc_raw = xbc_ref[...]
    xbc = _silu(_causal_conv(xbc_raw, halo_scr[...], cw_ref[...], cb_ref[...]))
    halo_scr[...] = xbc_raw[q - 8:, :]
    row = lax.broadcasted_iota(jnp.int32, (q, q), 0)
    col = lax.broadcasted_iota(jnp.int32, (q, q), 1)
    lower = row >= col
    tri = lower.astype(F32)
    dt = _softplus(dt_ref[...] + dtb_ref[...])
    a_cs = _cumsum_rows(dt * a_ref[...], tri)
    a_cs_t = a_cs.T
    a_last = a_cs[q - 1:q, :]
    e_cs = jnp.exp(a_cs)
    e_dec = jnp.exp(a_last - a_cs)
    e_last = jnp.exp(a_last)
    lane = lax.broadcasted_iota(jnp.int32, (q, 2 * M2_HEADDIM), 1)
    first = lane < M2_HEADDIM
    srow = lax.broadcasted_iota(jnp.int32, (2 * M2_HEADDIM, n), 0)
    heads_per_group = M2_HEADS // M2_GROUPS
    for grp in range(M2_GROUPS):
        bm = xbc[:, M2_INNER + grp * n:M2_INNER + (grp + 1) * n]
        cm = xbc[:, M2_INNER + M2_GROUPS * n + grp * n:M2_INNER + M2_GROUPS * n + (grp + 1) * n]
        cb = _bdot_nt(cm, bm)
        for pair in range(heads_per_group // 2):
            h0 = grp * heads_per_group + 2 * pair
            cols = slice(h0 * M2_HEADDIM, (h0 + 2) * M2_HEADDIM)
            xs = xbc[:, cols]
            dtp = jnp.where(first, dt[:, h0:h0 + 1], dt[:, h0 + 1:h0 + 2])
            xdt = xs * dtp
            y = jnp.zeros((q, 2 * M2_HEADDIM), F32)
            for s in range(2):
                hd = h0 + s
                seg = a_cs[:, hd:hd + 1] - a_cs_t[hd:hd + 1, :]
                lmat = jnp.exp(jnp.where(lower, seg, NEG_INF))
                part = jnp.where(first if s == 0 else jnp.logical_not(first), xdt, 0.0)
                y = y + _bdot(cb * lmat, part)
            state = state_scr[pl.ds(h0 * M2_HEADDIM, 2 * M2_HEADDIM), :]
            y_off = _bdot_nt(cm, state)
            y = y + y_off * jnp.where(first, e_cs[:, h0:h0 + 1], e_cs[:, h0 + 1:h0 + 2])
            dec = jnp.where(first, e_dec[:, h0:h0 + 1], e_dec[:, h0 + 1:h0 + 2])
            grow = jnp.where(srow < M2_HEADDIM, e_last[:, h0:h0 + 1], e_last[:, h0 + 1:h0 + 2])
            state_scr[pl.ds(h0 * M2_HEADDIM, 2 * M2_HEADDIM), :] = state * grow + _bdot_tn(xdt * dec, bm)
            y_scr[:, cols] = y + d_ref[:, cols] * xs
    gate = _silu(z_ref[...])
    y = y_scr[...] * gate
    gw = M2_INNER // M2_GROUPS
    parts = []
    for grp in range(M2_GROUPS):
        yg = y[:, grp * gw:(grp + 1) * gw]
        parts.append(yg * lax.rsqrt(jnp.mean(yg * yg, axis=-1, keepdims=True) + LN_EPS))
    yn = jnp.concatenate(parts, axis=1) * ng_ref[...]
    out = jnp.dot(yn.astype(BF16), wo_ref[...], preferred_element_type=F32)
    o_ref[...] = _layer_norm_rows(ALPHA * h_ref[...] + out, lg_ref[...], lb_ref[...])


def mamba2_layer(h2, bsz, seq, w_in, conv_w, conv_b, dt_bias, a_log, d_skip, norm_g, w_out, g, b):
    t = bsz * seq
    q = min(M2_Q, seq)
    nc = seq // q
    xbc_w = M2_INNER + 2 * M2_GROUPS * M2_STATE
    wb = w_in.astype(BF16)
    hpad = 128 - M2_HEADS
    pad_heads = lambda a: jnp.pad(a.astype(F32).reshape(1, -1), ((0, 0), (0, hpad)))
    z, xbc, dt = _linear(h2, [wb[:, :M2_INNER], wb[:, M2_INNER:M2_INNER + xbc_w],
                              jnp.pad(wb[:, M2_INNER + xbc_w:], ((0, 0), (0, hpad)))],
                         [F32, F32, F32], tm=256)
    d_cols = jnp.repeat(d_skip.astype(F32), M2_HEADDIM).reshape(1, -1)
    rowspec = lambda w: pl.BlockSpec((q, w), lambda i, j: (i * nc + j, 0))
    return pl.pallas_call(
        _mamba_kernel,
        name="mamba2",
        grid=(bsz, nc),
        in_specs=[rowspec(M2_INNER), rowspec(xbc_w), rowspec(128), rowspec(D_MODEL),
                  _full((M2_CONV, xbc_w)), _full((1, xbc_w)), _full((1, 128)), _full((1, 128)),
                  _full((1, M2_INNER)), _full((1, M2_INNER)), _full((M2_INNER, D_MODEL)),
                  _full((1, D_MODEL)), _full((1, D_MODEL))],
        out_specs=rowspec(D_MODEL),
        out_shape=jax.ShapeDtypeStruct((t, D_MODEL), F32),
        scratch_shapes=[pltpu.VMEM((M2_HEADS * M2_HEADDIM, M2_STATE), F32),
                        pltpu.VMEM((8, xbc_w), F32),
                        pltpu.VMEM((q, M2_INNER), F32)],
        compiler_params=_cparams("parallel", "arbitrary"),
    )(z, xbc, dt, h2, conv_w.astype(F32), conv_b.reshape(1, -1), pad_heads(dt_bias),
      pad_heads(-jnp.exp(a_log.astype(F32))), d_cols, norm_g.reshape(1, -1), w_out.astype(BF16),
      g.reshape(1, -1), b.reshape(1, -1))


def _mlstm_proj_kernel(xm_ref, cw_ref, cb_ref, wq_ref, wk_ref, wv_ref, wg_ref, bg_ref,
                       xc_ref, q_ref, k_ref, v_ref, gate_ref, halo_scr):
    tq = xm_ref.shape[0]

    @pl.when(pl.program_id(1) == 0)
    def _():
        halo_scr[...] = jnp.zeros(halo_scr.shape, F32)

    xm = xm_ref[...]
    xc = _silu(_causal_conv(xm, halo_scr[...], cw_ref[...], cb_ref[...]))
    halo_scr[...] = xm[tq - 8:, :]
    xc_ref[...] = xc
    gates = jnp.zeros((tq, bg_ref.shape[1]), F32) + bg_ref[...]
    for hd in range(ML_HEADS):
        cols = slice(hd * ML_HEADDIM, (hd + 1) * ML_HEADDIM)
        qh = _bdot(xc[:, cols], wq_ref[hd])
        kh = _bdot(xc[:, cols], wk_ref[hd]) * ML_HEADDIM ** -0.5
        vh = _bdot(xm[:, cols], wv_ref[hd])
        q_ref[:, cols] = qh.astype(BF16)
        k_ref[:, cols] = kh.astype(BF16)
        v_ref[:, cols] = vh.astype(BF16)
        gates = gates + _bdot(qh, wg_ref[0, cols, :]) + _bdot(kh, wg_ref[1, cols, :]) + _bdot(vh, wg_ref[2, cols, :])
    gate_ref[...] = gates


def _mlstm_cell_kernel(q_ref, k_ref, v_ref, gate_ref, h_ref, c_scr, n_scr, m_scr):
    hd = pl.program_id(1)
    qn = q_ref.shape[0]

    @pl.when(pl.program_id(2) == 0)
    def _():
        c_scr[...] = jnp.zeros(c_scr.shape, F32)
        n_scr[...] = jnp.zeros(n_scr.shape, F32)
        m_scr[...] = jnp.zeros(m_scr.shape, F32)

    q = q_ref[...]
    k = k_ref[...]
    v = v_ref[...]
    gates = gate_ref[...]
    glane = lax.broadcasted_iota(jnp.int32, gates.shape, 1)
    gsub = lax.broadcasted_iota(jnp.int32, (gates.shape[1], qn), 0)
    row = lax.broadcasted_iota(jnp.int32, (qn, qn), 0)
    col = lax.broadcasted_iota(jnp.int32, (qn, qn), 1)
    lower = row >= col
    logf = jnp.minimum(gates, 0.0) - jnp.log(1.0 + jnp.exp(-jnp.abs(gates)))
    cum = _cumsum_rows(logf, lower.astype(F32))
    ii = jnp.sum(jnp.where(glane == hd, gates, 0.0), axis=-1, keepdims=True)
    bcs = jnp.sum(jnp.where(glane == ML_HEADS + hd, cum, 0.0), axis=-1, keepdims=True)
    ii_row = jnp.sum(jnp.where(gsub == hd, gates.T, 0.0), axis=0, keepdims=True)
    bcs_row = jnp.sum(jnp.where(gsub == ML_HEADS + hd, cum.T, 0.0), axis=0, keepdims=True)
    m_prev = m_scr[0:1, 0:1]
    dmat = jnp.where(lower, bcs - bcs_row + ii_row, NEG_INF)
    inter = bcs + m_prev
    m_row = jnp.maximum(jnp.max(dmat, axis=-1, keepdims=True), inter)
    s = _bdot_nt(q, k) * jnp.exp(dmat - m_row)
    w_inter = jnp.exp(inter - m_row)
    num = _bdot(s, v) + w_inter * _bdot(q, c_scr[...])
    den = jnp.sum(s, axis=-1, keepdims=True) + w_inter * jnp.sum(q.astype(F32) * n_scr[0:1, :], axis=-1, keepdims=True)
    h_ref[...] = num / jnp.maximum(jnp.abs(den), jnp.exp(-m_row))
    b_last = bcs[qn - 1:qn, :]
    gdec = b_last - bcs + ii
    m_new = jnp.maximum(b_last + m_prev, jnp.max(gdec, axis=0, keepdims=True))
    wk = jnp.exp(gdec - m_new)
    decay = jnp.exp(b_last + m_prev - m_new)
    kw = k.astype(F32) * wk
    c_scr[...] = decay * c_scr[...] + _bdot_tn(kw, v)
    n_scr[...] = decay * n_scr[...] + jnp.sum(kw, axis=0, keepdims=True)
    m_scr[...] = jnp.zeros(m_scr.shape, F32) + m_new


def _mlstm_out_kernel(hc_ref, xc_ref, og_ref, h_ref, ng_ref, sk_ref, wd_ref, lg_ref, lb_ref, o_ref):
    hc = hc_ref[...]
    parts = []
    for hd in range(ML_HEADS):
        x = hc[:, hd * ML_HEADDIM:(hd + 1) * ML_HEADDIM]
        mu = jnp.mean(x, axis=-1, keepdims=True)
        c = x - mu
        parts.append(c * lax.rsqrt(jnp.mean(c * c, axis=-1, keepdims=True) + LN_EPS))
    hn = jnp.concatenate(parts, axis=1) * ng_ref[...] + sk_ref[...] * xc_ref[...]
    out = hn * _sigmoid(og_ref[...])
    y = jnp.dot(out.astype(BF16), wd_ref[...], preferred_element_type=F32)
    o_ref[...] = _layer_norm_rows(ALPHA * h_ref[...] + y, lg_ref[...], lb_ref[...])


def mlstm_layer(h2, bsz, seq, w_in, conv_w, conv_b, w_q, w_k, w_v, w_gates, b_gates, norm_g, skip, w_down, g, b):
    t = bsz * seq
    wb = w_in.astype(BF16)
    xm, og = _linear(h2, [wb[:, :ML_INNER], wb[:, ML_INNER:]], [F32, F32])
    tq = min(256, seq)
    nt = seq // tq
    rows = lambda w: pl.BlockSpec((tq, w), lambda i, j: (i * nt + j, 0))
    gpad = 128 - 2 * ML_HEADS
    wg = jnp.pad(w_gates.astype(BF16), ((0, 0), (0, 0), (0, gpad)))
    bg = jnp.pad(b_gates.astype(F32).reshape(1, -1), ((0, 0), (0, gpad)))
    xc, q, k, v, gates = pl.pallas_call(
        _mlstm_proj_kernel,
        name="mlstm_proj",
        grid=(bsz, nt),
        in_specs=[rows(ML_INNER), _full((ML_CONV, ML_INNER)), _full((1, ML_INNER)),
                  _full(w_q.shape), _full(w_k.shape), _full(w_v.shape), _full(wg.shape), _full(bg.shape)],
        out_specs=[rows(ML_INNER), rows(ML_INNER), rows(ML_INNER), rows(ML_INNER), rows(128)],
        out_shape=[jax.ShapeDtypeStruct((t, ML_INNER), F32)] + [jax.ShapeDtypeStruct((t, ML_INNER), BF16)] * 3
        + [jax.ShapeDtypeStruct((t, 128), F32)],
        scratch_shapes=[pltpu.VMEM((8, ML_INNER), F32)],
        compiler_params=_cparams("parallel", "arbitrary"),
    )(xm, conv_w.astype(F32), conv_b.reshape(1, -1), w_q.astype(BF16), w_k.astype(BF16), w_v.astype(BF16), wg, bg)
    qn = min(ML_Q, seq)
    nc = seq // qn
    head = pl.BlockSpec((qn, ML_HEADDIM), lambda i, hd, j: (i * nc + j, hd))
    hc = pl.pallas_call(
        _mlstm_cell_kernel,
        name="mlstm_cell",
        grid=(bsz, ML_HEADS, nc),
        in_specs=[head, head, head, pl.BlockSpec((qn, 128), lambda i, hd, j: (i * nc + j, 0))],
        out_specs=head,
        out_shape=jax.ShapeDtypeStruct((t, ML_INNER), F32),
        scratch_shapes=[pltpu.VMEM((ML_HEADDIM, ML_HEADDIM), F32), pltpu.VMEM((8, ML_HEADDIM), F32),
                        pltpu.VMEM((8, 128), F32)],
        compiler_params=_cparams("parallel", "parallel", "arbitrary"),
    )(q, k, v, gates)
    tm = min(512, t)
    row = lambda w: pl.BlockSpec((tm, w), lambda i: (i, 0))
    return pl.pallas_call(
        _mlstm_out_kernel,
        name="mlstm_out",
        grid=(t // tm,),
        in_specs=[row(ML_INNER), row(ML_INNER), row(ML_INNER), row(D_MODEL), _full((1, ML_INNER)),
                  _full((1, ML_INNER)), _full((ML_INNER, D_MODEL)), _full((1, D_MODEL)), _full((1, D_MODEL))],
        out_specs=row(D_MODEL),
        out_shape=jax.ShapeDtypeStruct((t, D_MODEL), F32),
        compiler_params=_cparams("parallel"),
    )(hc, xc, og, h2, norm_g.reshape(1, -1), skip.reshape(1, -1), w_down.astype(BF16),
      g.reshape(1, -1), b.reshape(1, -1))


def _cross_attn_kernel(h_ref, k_ref, v_ref, wq_ref, wo_ref, g_ref, b_ref, o_ref):
    h = h_ref[...]
    q = jnp.dot(h.astype(BF16), wq_ref[...], preferred_element_type=F32)
    parts = []
    for hd in range(XA_HEADS):
        cols = slice(hd * XA_HEADDIM, (hd + 1) * XA_HEADDIM)
        s = _bdot_nt(q[:, cols], k_ref[:, cols]) * XA_HEADDIM ** -0.5
        p = jnp.exp(s - jnp.max(s, axis=-1, keepdims=True))
        p = p / jnp.sum(p, axis=-1, keepdims=True)
        parts.append(_bdot(p, v_ref[:, cols]))
    o = jnp.concatenate(parts, axis=1)
    y = jnp.dot(o.astype(BF16), wo_ref[...], preferred_element_type=F32)
    o_ref[...] = _layer_norm_rows(ALPHA * h + y, g_ref[...], b_ref[...])


def cross_attention_layer(h2, mem2, bsz, seq, w_q, w_kv, w_o, g, b, tq=512):
    t = bsz * seq
    mlen = mem2.shape[0] // bsz
    tq = min(tq, seq)
    nq = seq // tq
    (kv,) = _linear(mem2, [w_kv.astype(BF16)], [BF16], tm=256)
    return pl.pallas_call(
        _cross_attn_kernel,
        name="cross_attention",
        grid=(bsz, nq),
        in_specs=[pl.BlockSpec((tq, D_MODEL), lambda i, j: (i * nq + j, 0)),
                  pl.BlockSpec((mlen, D_MODEL), lambda i, j: (i, 0)),
                  pl.BlockSpec((mlen, D_MODEL), lambda i, j: (i, 1)),
                  _full((D_MODEL, D_MODEL)), _full((D_MODEL, D_MODEL)), _full((1, D_MODEL)), _full((1, D_MODEL))],
        out_specs=pl.BlockSpec((tq, D_MODEL), lambda i, j: (i * nq + j, 0)),
        out_shape=jax.ShapeDtypeStruct((t, D_MODEL), F32),
        compiler_params=_cparams("parallel", "parallel"),
    )(h2, kv, kv, w_q.astype(BF16), w_o.astype(BF16), g.reshape(1, -1), b.reshape(1, -1))


_PK_PIECES = (('a', 0, 0), ('a', 0, 8), ('a', 1, 0), ('a', 2, 0), ('a', 3, 0),
              ('b', 0, 8), ('b', 0, 0), ('b', 1, 0), ('b', 2, 0))
_PK_INVALID = 1 << 20


def _peer_piece_ids():
    ids = []
    seen = set()
    for kind, fixed, off in _PK_PIECES:
        for i in range(8):
            a, b = (fixed, off + i) if kind == 'a' else (off + i, fixed)
            ok = (a + 1) * (b + 1) <= PK_TOPK and (a, b) not in seen
            seen.add((a, b))
            ids.append(a * PK_TOPK + b if ok else _PK_INVALID)
    assert sum(i != _PK_INVALID for i in ids) == sum((a + 1) * (b + 1) <= PK_TOPK
                                                     for a in range(PK_TOPK) for b in range(PK_TOPK))
    return jnp.array(ids, jnp.int32).reshape(-1, 1)


def _peer_select_kernel(h_ref, wq_ref, keys_ref, flat_ref, eidx_ref, gate_ref, q_scr, e_scr):
    tm = h_ref.shape[0]
    q_scr[...] = jnp.dot(h_ref[...].astype(BF16), wq_ref[...], preferred_element_type=F32).astype(BF16)
    key_id = lax.broadcasted_iota(jnp.int32, (PK_NKEYS, tm), 0)
    rank_id = lax.broadcasted_iota(jnp.int32, (PK_TOPK, tm), 0)
    flat = jnp.broadcast_to(flat_ref[...], (flat_ref.shape[0], tm))
    neg = jnp.float32(-jnp.inf)
    half = PK_QDIM // 2
    zero = jnp.zeros((PK_TOPK, tm), F32)

    n_col = PK_NKEYS // 8
    depth = PK_STACK_DEPTH

    def sort_columns(s):
        vals = [s[8 * v:8 * v + 8, :] for v in range(n_col)]
        keys = [key_id[8 * v:8 * v + 8, :] for v in range(n_col)]
        for base in range(0, n_col, depth):
            for rnd in range(depth):
                for i in range(base + rnd % 2, base + depth - 1, 2):
                    swap = vals[i + 1] > vals[i]
                    vals[i], vals[i + 1] = jnp.where(swap, vals[i + 1], vals[i]), jnp.where(swap, vals[i], vals[i + 1])
                    keys[i], keys[i + 1] = jnp.where(swap, keys[i + 1], keys[i]), jnp.where(swap, keys[i], keys[i + 1])
        return vals + keys

    def top1(a, st):
        vals, keys, top_v, top_i = list(st[:n_col]), list(st[n_col:2 * n_col]), st[2 * n_col], st[2 * n_col + 1]
        heads = range(0, n_col, depth)
        best = functools.reduce(jnp.maximum, [vals[h] for h in heads])
        m = jnp.max(best, axis=0, keepdims=True)
        first_key = functools.reduce(jnp.minimum, [jnp.where(vals[h] == m, keys[h], PK_NKEYS) for h in heads])
        idx = jnp.min(first_key, axis=0, keepdims=True)
        for h in heads:
            win = keys[h] == idx
            for k in range(h, h + depth - 1):
                vals[k] = jnp.where(win, vals[k + 1], vals[k])
                keys[k] = jnp.where(win, keys[k + 1], keys[k])
            vals[h + depth - 1] = jnp.where(win, neg, vals[h + depth - 1])
        sel = rank_id == a
        return vals + keys + [jnp.where(sel, m, top_v), jnp.where(sel, idx.astype(F32), top_i)]

    def candidates(vals0, idxs0, vals1, idxs1):
        rep = lambda x, r: jnp.broadcast_to(x[r:r + 1, :], (8, tm))
        v, c = [], []
        for kind, fixed, off in _PK_PIECES:
            if kind == 'a':
                v.append(rep(vals0, fixed) + vals1[off:off + 8, :])
                c.append(rep(idxs0, fixed) * PK_NKEYS + idxs1[off:off + 8, :])
            else:
                v.append(vals0[off:off + 8, :] + rep(vals1, fixed))
                c.append(idxs0[off:off + 8, :] * PK_NKEYS + rep(idxs1, fixed))
        cand = jnp.where(flat == _PK_INVALID, neg, jnp.concatenate(v, axis=0))
        return cand, jnp.concatenate(c, axis=0)

    def top2(kk, cand, cidx, cv, ce):
        m = jnp.max(cand, axis=0, keepdims=True)
        pos = jnp.min(jnp.where(cand == m, flat, _PK_INVALID), axis=0, keepdims=True)
        hit = flat == pos
        e = jnp.max(jnp.where(hit, cidx, -1.0), axis=0, keepdims=True)
        sel = rank_id == kk
        return jnp.where(hit, neg, cand), jnp.where(sel, m, cv), jnp.where(sel, e, ce)

    tops = None
    for stage in range(PK_HEADS + 1):
        first = stage < PK_HEADS
        second = stage > 0
        init = []
        n_st = 2 * n_col + 2
        if first:
            for j in range(2):
                cols = slice(stage * PK_QDIM + j * half, stage * PK_QDIM + (j + 1) * half)
                init += sort_columns(_bdot_nt(keys_ref[j], q_scr[:, cols])) + [zero, zero]
        if second:
            cand0, cidx = candidates(*tops)
            init += [cand0, zero, zero]

        def body(a, st, first=first, second=second, cidx=cidx if second else None):
            st = list(st)
            out = []
            if first:
                out += top1(a, st[0:n_st]) + top1(a, st[n_st:2 * n_st])
                st = st[2 * n_st:]
            if second:
                out += top2(a, st[0], cidx, st[1], st[2])
            return tuple(out)

        def trip(i, st, body=body):
            for step in range(PK_TOPK_UNROLL):
                st = body(PK_TOPK_UNROLL * i + step, st)
            return st

        res = lax.fori_loop(0, PK_TOPK // PK_TOPK_UNROLL, trip, tuple(init))
        if second:
            cv, ce = res[-2], res[-1]
            p = jnp.exp(cv - cv[0:1, :])
            rows = slice((stage - 1) * PK_TOPK, stage * PK_TOPK)
            gate_ref[rows, :] = p / jnp.sum(p, axis=0, keepdims=True)
            e_scr[rows, :] = ce * ROW_WORDS
        if first:
            tops = (res[n_st - 2], res[n_st - 1], res[2 * n_st - 2], res[2 * n_st - 1])
    eidx_ref[...] = e_scr[...].T.astype(jnp.int32)


def _peer_select(h2, wq_bf, keys_bf, tm=128):
    t = h2.shape[0]
    tm = min(tm, t)
    flat = _peer_piece_ids()
    return pl.pallas_call(
        _peer_select_kernel,
        name="peer_select",
        grid=(t // tm,),
        in_specs=[pl.BlockSpec((tm, D_MODEL), lambda i: (i, 0)), _full(wq_bf.shape), _full(keys_bf.shape),
                  _full(flat.shape)],
        out_specs=[pl.BlockSpec((tm, PK_SLOTS), lambda i: (i, 0)),
                   pl.BlockSpec((PK_SLOTS, tm), lambda i: (0, i))],
        out_shape=[jax.ShapeDtypeStruct((t, PK_SLOTS), jnp.int32),
                   jax.ShapeDtypeStruct((PK_SLOTS, t), F32)],
        scratch_shapes=[pltpu.VMEM((tm, PK_HEADS * PK_QDIM), BF16), pltpu.VMEM((PK_SLOTS, tm), F32)],
        compiler_params=_cparams("parallel"),
    )(h2, wq_bf, keys_bf, flat)


def _pack_rows(w):
    e, d = w.shape
    wb = lax.bitcast_convert_type(w.astype(BF16), jnp.uint16).astype(jnp.uint32)
    packed = (wb[:, d // 2:] << 16) | wb[:, :d // 2]
    return lax.bitcast_convert_type(packed, jnp.int32).reshape(e * ROW_WORDS, 128)


def _unpack_lo(w):
    return pltpu.bitcast(w << 16, F32)


def _unpack_hi(w):
    return pltpu.bitcast(w & jnp.int32(-65536), F32)


def _gather_group(idx_ref, t0, tb, tab_ref, tiles_ref, first_tile):
    for u in range(PK_GROUP):
        start = jnp.minimum(t0 + u, tb - 1) * PK_SLOTS
        for r0 in range(0, PK_SLOTS, PK_IDX_RUN):
            run_ref = idx_ref.at[pl.ds(start + r0, PK_IDX_RUN)]
            for j in range(PK_IDX_RUN):
                i = pl.multiple_of(run_ref[j], ROW_WORDS)
                tiles_ref[first_tile + u, pl.ds((r0 + j) * ROW_WORDS, ROW_WORDS), :] = tab_ref[pl.ds(i, ROW_WORDS), :]


def _group_pipeline(tb, idx_ref, tab_ref, tiles_ref, dense_group):
    _gather_group(idx_ref, 0, tb, tab_ref, tiles_ref, 0)

    def trip(i, carry):
        t0 = 2 * PK_GROUP * i
        for half in range(2):
            base = pl.multiple_of(t0 + half * PK_GROUP, PK_GROUP)
            _gather_group(idx_ref, base + PK_GROUP, tb, tab_ref, tiles_ref, (1 - half) * PK_GROUP)
            dense_group(base, half * PK_GROUP)
        return carry

    lax.fori_loop(0, tb // (2 * PK_GROUP), trip, 0)


def _lane_block(base):
    return pl.multiple_of((base // 128) * 128, 128)


def _peer_score_kernel(idx_ref, x_ref, gate_ref, tab_ref, act_ref, tiles_ref):
    tb = x_ref.shape[0]
    lane = lax.broadcasted_iota(jnp.int32, (PK_SLOTS, 128), 1)
    half = D_MODEL // 2

    def dense_group(base, first_tile):
        xg = x_ref[pl.ds(base, PK_GROUP), :]
        blk = _lane_block(base)
        cols = act_ref[:, pl.ds(blk, 128)]
        for u in range(PK_GROUP):
            acc = jnp.zeros((PK_SLOTS, 128), F32)
            for s in range(ROW_WORDS):
                w = tiles_ref[first_tile + u, pl.ds(s, PK_SLOTS, stride=ROW_WORDS), :]
                acc = (acc + _unpack_lo(w) * xg[u:u + 1, s * 128:(s + 1) * 128]
                       + _unpack_hi(w) * xg[u:u + 1, half + s * 128:half + (s + 1) * 128])
            cols = jnp.where(lane == base + u - blk, jnp.sum(acc, axis=-1, keepdims=True), cols)
        act_ref[:, pl.ds(blk, 128)] = cols

    act_ref[...] = jnp.zeros(act_ref.shape, F32)
    _group_pipeline(tb, idx_ref, tab_ref, tiles_ref, dense_group)
    act_ref[...] = _gelu(act_ref[...]) * gate_ref[...]


def _peer_score(eidx, x2, gate, tab, tb=512):
    t = x2.shape[0]
    tb = min(tb, t)
    slots = pl.BlockSpec((PK_SLOTS, tb), lambda i: (0, i))
    return pl.pallas_call(
        _peer_score_kernel,
        name="peer_score",
        grid=(t // tb,),
        in_specs=[pl.BlockSpec((tb * PK_SLOTS,), lambda i: (i,), memory_space=pltpu.SMEM),
                  pl.BlockSpec((tb, D_MODEL), lambda i: (i, 0)), slots, _full(tab.shape)],
        out_specs=slots,
        out_shape=jax.ShapeDtypeStruct((PK_SLOTS, t), F32),
        scratch_shapes=[pltpu.VMEM((2 * PK_GROUP, PK_SLOTS * ROW_WORDS, 128), jnp.int32)],
        compiler_params=_cparams("parallel"),
    )(eidx.reshape(-1), x2, gate, tab)


def _peer_combine_kernel(idx_ref, act_ref, h_ref, g_ref, b_ref, tab_ref, out_ref, y_ref, tiles_ref):
    tb = h_ref.shape[0]
    lane = lax.broadcasted_iota(jnp.int32, (PK_SLOTS, 128), 1)

    def dense_group(base, first_tile):
        blk = _lane_block(base)
        act = act_ref[:, pl.ds(blk, 128)]
        rows = []
        for u in range(PK_GROUP):
            a = jnp.sum(jnp.where(lane == base + u - blk, act, 0.0), axis=-1, keepdims=True)
            lo, hi = [], []
            for s in range(ROW_WORDS):
                w = tiles_ref[first_tile + u, pl.ds(s, PK_SLOTS, stride=ROW_WORDS), :]
                lo.append(jnp.sum(a * _unpack_lo(w), axis=0, keepdims=True))
                hi.append(jnp.sum(a * _unpack_hi(w), axis=0, keepdims=True))
            rows.append(jnp.concatenate(lo + hi, axis=1))
        y_ref[pl.ds(base, PK_GROUP), :] = jnp.concatenate(rows, axis=0)

    _group_pipeline(tb, idx_ref, tab_ref, tiles_ref, dense_group)
    out_ref[...] = _layer_norm_rows(ALPHA * h_ref[...] + y_ref[...], g_ref[...], b_ref[...])


def _peer_combine(eidx, act, h2, g, b, tab, tb=512):
    t = h2.shape[0]
    tb = min(tb, t)
    tok = pl.BlockSpec((tb, D_MODEL), lambda i: (i, 0))
    return pl.pallas_call(
        _peer_combine_kernel,
        name="peer_combine",
        grid=(t // tb,),
        in_specs=[pl.BlockSpec((tb * PK_SLOTS,), lambda i: (i,), memory_space=pltpu.SMEM),
                  pl.BlockSpec((PK_SLOTS, tb), lambda i: (0, i)),
                  tok, _full((1, D_MODEL)), _full((1, D_MODEL)), _full(tab.shape)],
        out_specs=tok,
        out_shape=jax.ShapeDtypeStruct((t, D_MODEL), F32),
        scratch_shapes=[pltpu.VMEM((tb, D_MODEL), F32),
                        pltpu.VMEM((2 * PK_GROUP, PK_SLOTS * ROW_WORDS, 128), jnp.int32)],
        compiler_params=_cparams("parallel"),
    )(eidx.reshape(-1), act, h2, g.reshape(1, -1), b.reshape(1, -1), tab)


def peer_layer(h2, w_query, sub_keys, u, v, g, b):
    eidx, gate = _peer_select(h2, w_query.astype(BF16), sub_keys.astype(BF16))
    act = _peer_score(eidx, h2, gate, _pack_rows(u))
    return _peer_combine(eidx, act, h2, g, b, _pack_rows(v))


def kernel(x, mem, s5_lam_re, s5_lam_im, s5_log_dt, s5_b_re, s5_b_im, s5_c_re, s5_c_im, s5_d, s5_w_glu, s5_b_glu, da_w_qkv, da_lambda, da_subln_g, da_w_o, m2_w_in, m2_conv_w, m2_conv_b, m2_dt_bias, m2_a_log, m2_d, m2_norm_g, m2_w_out, ml_w_in, ml_conv_w, ml_conv_b, ml_w_q, ml_w_k, ml_w_v, ml_w_gates, ml_b_gates, ml_norm_g, ml_skip, ml_w_down, xa_w_q, xa_w_kv, xa_w_o, pk_w_query, pk_sub_keys, pk_u, pk_v, ln_g, ln_b):
    bsz, seq, dm = x.shape
    t = bsz * seq
    mem2 = mem.reshape(-1, dm)
    h = x.reshape(t, dm)
    for i in range(DEPTH):
        kind, j = i % 4, i // 4
        g0, b0 = ln_g[i, 0], ln_b[i, 0]
        if kind == 0:
            h = s5_layer(h.reshape(bsz, seq, dm), s5_lam_re[j], s5_lam_im[j], s5_log_dt[j], s5_b_re[j], s5_b_im[j],
                         s5_c_re[j], s5_c_im[j], s5_d[j], s5_w_glu[j], s5_b_glu[j], g0, b0)
        elif kind == 1:
            h = diff_attention_layer(h, bsz, seq, da_w_qkv[j], da_lambda[j], da_subln_g[j], da_w_o[j], i, g0, b0)
        elif kind == 2:
            h = mamba2_layer(h, bsz, seq, m2_w_in[j], m2_conv_w[j], m2_conv_b[j], m2_dt_bias[j], m2_a_log[j],
                             m2_d[j], m2_norm_g[j], m2_w_out[j], g0, b0)
        else:
            h = mlstm_layer(h, bsz, seq, ml_w_in[j], ml_conv_w[j], ml_conv_b[j], ml_w_q[j], ml_w_k[j], ml_w_v[j],
                            ml_w_gates[j], ml_b_gates[j], ml_norm_g[j], ml_skip[j], ml_w_down[j], g0, b0)
        h = cross_attention_layer(h, mem2, bsz, seq, xa_w_q[i], xa_w_kv[i], xa_w_o[i], ln_g[i, 1], ln_b[i, 1])
        h = peer_layer(h, pk_w_query[i], pk_sub_keys[i], pk_u[i], pk_v[i], ln_g[i, 2], ln_b[i, 2])
    return h.reshape(bsz, seq, dm)
```
